```python
import math
import jax
import jax.numpy as jnp
from jax import lax
import numpy as np

D_MODEL = 1024
BATCH = 8
SEQ = 2048
DEPTH = 1
DEC_BATCH = 128
DEC_SEQ = 1
PAST_LEN = 2048
PAGE_SIZE = 128

N_HEADS = 8
HEAD_DIM = 64
ATTN_WIDTH = N_HEADS * HEAD_DIM
MOBA_BLOCK = 256
MOBA_TOP = 3
Q_CHUNK = 32
ROPE_THETA = 10000.0
SSM_WIDTH = D_MODEL // 2
GROUP_CH = 16
N_GROUPS = SSM_WIDTH // GROUP_CH
STATE_DIM = 64
N_EXPERTS = 32
TOP_K = 4
D_FF = D_MODEL
SWIGLU_ALPHA = 1.702
SWIGLU_LIMIT = 7.0
EPS = 1e-6
IN_WIDTH = 3 * ATTN_WIDTH + SSM_WIDTH + 2 * D_MODEL
SPLITS = (ATTN_WIDTH, 2 * ATTN_WIDTH, 3 * ATTN_WIDTH, 3 * ATTN_WIDTH + SSM_WIDTH,
          3 * ATTN_WIDTH + SSM_WIDTH + D_MODEL)

kernel_name = "moba_s5_gated_moe_decode_step"


def _rmsnorm(x, g):
    xf = x.astype(jnp.float32)
    xf = xf * lax.rsqrt(jnp.mean(xf * xf, axis=-1, keepdims=True) + EPS)
    return (xf * g.astype(jnp.float32)).astype(x.dtype)


def _rope(x, positions):
    half = x.shape[-1] // 2
    inv_freq = ROPE_THETA ** (-jnp.arange(half, dtype=jnp.float32) / half)
    ang = positions.astype(jnp.float32)[:, None] * inv_freq[None, :]
    cos = jnp.cos(ang)[None, :, None, :]
    sin = jnp.sin(ang)[None, :, None, :]
    xf = x.astype(jnp.float32)
    x1, x2 = xf[..., :half], xf[..., half:]
    return jnp.concatenate([x1 * cos - x2 * sin, x2 * cos + x1 * sin], axis=-1).astype(x.dtype)


def _moba_attention(q, k, v, pos0):
    bsz, sq, nh, hd = q.shape
    sk = k.shape[1]
    nblk = -(-sk // MOBA_BLOCK)
    pad = nblk * MOBA_BLOCK - sk
    k_blk = jnp.pad(k, ((0, 0), (0, pad), (0, 0), (0, 0))).reshape(bsz, nblk, MOBA_BLOCK, nh, hd)
    v_blk = jnp.pad(v, ((0, 0), (0, pad), (0, 0), (0, 0))).reshape(bsz, nblk, MOBA_BLOCK, nh, hd)
    k_mean = jnp.mean(k_blk, axis=2, dtype=jnp.float32)
    k_sel = min(MOBA_TOP, nblk)
    scale = 1.0 / math.sqrt(hd)
    qc = math.gcd(Q_CHUNK, sq)
    n_chunks = sq // qc
    qs = q.reshape(bsz, n_chunks, qc, nh, hd).transpose(1, 0, 3, 2, 4)
    ps = (pos0 + jnp.arange(sq, dtype=jnp.int32)).reshape(n_chunks, qc)
    b_idx = jnp.arange(bsz)[:, None, None, None]
    h_idx = jnp.arange(nh)[None, :, None, None]
    blk_ids = jnp.arange(nblk, dtype=jnp.int32)
    in_blk = jnp.arange(MOBA_BLOCK, dtype=jnp.int32)

    def one_chunk(args):
        qb, pb = args
        own = pb // MOBA_BLOCK
        past = blk_ids[None, :] < own[:, None]
        gate = jnp.einsum('bhqd,bnhd->bhqn', qb, k_mean, preferred_element_type=jnp.float32)
        gate = jnp.where(past[None, None], gate, -jnp.inf)
        _, top_idx = lax.top_k(gate, k_sel)
        sel_ok = jnp.take_along_axis(jnp.broadcast_to(past[None, None], gate.shape), top_idx, axis=-1)
        own_b = jnp.broadcast_to(own[None, None, :, None], (bsz, nh, qc, 1)).astype(top_idx.dtype)
        idx = jnp.concatenate([top_idx, own_b], axis=-1)
        kg = k_blk[b_idx, idx, :, h_idx]
        vg = v_blk[b_idx, idx, :, h_idx]
        s = jnp.einsum('bhqd,bhqnkd->bhqnk', qb, kg, preferred_element_type=jnp.float32) * scale
        own_causal = (own[:, None] * MOBA_BLOCK + in_blk[None, :]) <= pb[:, None]
        valid = jnp.concatenate([
            jnp.broadcast_to(sel_ok[..., None], (bsz, nh, qc, k_sel, MOBA_BLOCK)),
            jnp.broadcast_to(own_causal[None, None, :, None, :], (bsz, nh, qc, 1, MOBA_BLOCK))], axis=3)
        s = jnp.where(valid, s, -jnp.inf)
        p = jax.nn.softmax(s.reshape(bsz, nh, qc, -1), axis=-1).reshape(s.shape)
        return jnp.einsum('bhqnk,bhqnkd->bhqd', p, vg, preferred_element_type=jnp.float32).astype(q.dtype)

    out = lax.map(one_chunk, (qs, ps))
    return out.transpose(1, 0, 3, 2, 4).reshape(bsz, sq, nh * hd)


def _s5(u, h_re0, h_im0, a_re, a_im, log_dt, b_re, b_im, c_re, c_im, d_skip):
    bsz, seq, _ = u.shape
    uf = u.astype(jnp.float32).reshape(bsz, seq, N_GROUPS, GROUP_CH)
    lam_re = a_re.astype(jnp.float32)
    lam_im = a_im.astype(jnp.float32)
    dt = jnp.exp(log_dt.astype(jnp.float32))[:, None]
    mag = jnp.exp(lam_re * dt)
    ab_re = mag * jnp.cos(lam_im * dt)
    ab_im = mag * jnp.sin(lam_im * dt)
    den = lam_re * lam_re + lam_im * lam_im
    n_re, n_im = ab_re - 1.0, ab_im
    f_re = (n_re * lam_re + n_im * lam_im) / den
    f_im = (n_im * lam_re - n_re * lam_im) / den
    br, bi = b_re.astype(jnp.float32), b_im.astype(jnp.float32)
    bb_re = f_re[..., None] * br - f_im[..., None] * bi
    bb_im = f_re[..., None] * bi + f_im[..., None] * br
    bu_re = jnp.einsum('bsgc,gpc->bsgp', uf, bb_re)
    bu_im = jnp.einsum('bsgc,gpc->bsgp', uf, bb_im)
    h0r, h0i = h_re0.astype(jnp.float32), h_im0.astype(jnp.float32)
    bu_re = bu_re.at[:, 0].add(ab_re * h0r - ab_im * h0i)
    bu_im = bu_im.at[:, 0].add(ab_re * h0i + ab_im * h0r)
    a_r = jnp.broadcast_to(ab_re, bu_re.shape)
    a_i = jnp.broadcast_to(ab_im, bu_im.shape)

    def combine(left, right):
        alr, ali, blr, bli = left
        arr, ari, brr, bri = right
        return (arr * alr - ari * ali, arr * ali + ari * alr,
                arr * blr - ari * bli + brr, arr * bli + ari * blr + bri)

    _, _, h_re, h_im = lax.associative_scan(combine, (a_r, a_i, bu_re, bu_im), axis=1)
    y = (jnp.einsum('bsgp,gcp->bsgc', h_re, c_re.astype(jnp.float32))
         - jnp.einsum('bsgp,gcp->bsgc', h_im, c_im.astype(jnp.float32)))
    y = y.reshape(bsz, seq, SSM_WIDTH) + d_skip.astype(jnp.float32) * uf.reshape(bsz, seq, SSM_WIDTH)
    return y.astype(u.dtype), h_re[:, -1], h_im[:, -1]


def _moe(x, w_router, b_router, w_gate, b_gate, w_up, b_up, w_down, b_down):
    tok = x.reshape(-1, x.shape[-1])
    logits = jnp.einsum('td,de->te', tok, w_router, preferred_element_type=jnp.float32) + b_router.astype(jnp.float32)
    top_logit, top_idx = lax.top_k(logits, TOP_K)
    top_w = jax.nn.softmax(top_logit, axis=-1)
    combine = jnp.einsum('tk,tke->te', top_w, jax.nn.one_hot(top_idx, N_EXPERTS, dtype=jnp.float32))
    out = jnp.zeros(tok.shape, jnp.float32)
    for e in range(N_EXPERTS):
        gate = tok @ w_gate[e] + b_gate[e]
        up = tok @ w_up[e] + b_up[e]
        gate = jnp.minimum(gate, SWIGLU_LIMIT)
        up = jnp.clip(up, -SWIGLU_LIMIT, SWIGLU_LIMIT)
        hid = (up + 1.0) * (gate * jax.nn.sigmoid(SWIGLU_ALPHA * gate))
        out = out + combine[:, e:e + 1] * (hid @ w_down[e] + b_down[e])
    return out.reshape(x.shape).astype(x.dtype)


def _decoder_layer(x, pos0, k_past, v_past, h_re0, h_im0,
                   norm1_g, w_in, q_norm_g, k_norm_g,
                   ssm_a_re, ssm_a_im, ssm_log_dt, ssm_b_re, ssm_b_im, ssm_c_re, ssm_c_im, ssm_d,
                   w_attn_up, w_ssm_glu, w_out, norm2_g,
                   w_router, b_router, w_gate, b_gate, w_up, b_up, w_down, b_down):
    bsz, seq, _ = x.shape
    positions = pos0 + jnp.arange(seq, dtype=jnp.int32)
    xn = _rmsnorm(x, norm1_g)
    proj = jnp.einsum('bsd,de->bse', xn, w_in)
    q, k, v, u, g_attn, g_ssm = jnp.split(proj, SPLITS, axis=-1)
    heads = (bsz, seq, N_HEADS, HEAD_DIM)
    q = _rope(_rmsnorm(q.reshape(heads), q_norm_g), positions)
    k = _rope(_rmsnorm(k.reshape(heads), k_norm_g), positions)
    v = v.reshape(heads)
    if k_past is None:
        k_all, v_all = k, v
    else:
        k_all = jnp.concatenate([k_past, k.astype(k_past.dtype)], axis=1)
        v_all = jnp.concatenate([v_past, v.astype(v_past.dtype)], axis=1)
    attn = _moba_attention(q, k_all, v_all, pos0)
    y_ssm, h_re, h_im = _s5(u, h_re0, h_im0, ssm_a_re, ssm_a_im, ssm_log_dt,
                            ssm_b_re, ssm_b_im, ssm_c_re, ssm_c_im, ssm_d)
    branch_attn = jnp.einsum('bse,ed->bsd', attn.astype(x.dtype), w_attn_up)
    glu = jnp.einsum('bse,ef->bsf', jax.nn.gelu(y_ssm.astype(x.dtype)), w_ssm_glu)
    glu_a, glu_b = jnp.split(glu, 2, axis=-1)
    branch_ssm = glu_a * jax.nn.sigmoid(glu_b)
    merged = jax.nn.sigmoid(g_attn) * branch_attn + jax.nn.sigmoid(g_ssm) * branch_ssm
    h = x + jnp.einsum('bsd,de->bse', merged, w_out)
    y = h + _moe(_rmsnorm(h, norm2_g), w_router, b_router, w_gate, b_gate, w_up, b_up, w_down, b_down)
    return y, k, v, h_re, h_im


def setup_inputs(seed: int = 0) -> dict:
    key = jax.random.key(seed)
    ks = jax.random.split(key, 32)
    n_pages = PAST_LEN // PAGE_SIZE
    n_used = DEC_BATCH * n_pages
    n_pool = n_used + max(1, n_used // 4)
    f32 = jnp.float32

    def nrm(k, shape, scale):
        return jax.random.normal(k, shape, f32) * scale

    page_table = jax.random.permutation(ks[0], n_pool)[:n_used].reshape(DEC_BATCH, n_pages).astype(jnp.int32)
    a_im_init = math.pi * jnp.arange(STATE_DIM, dtype=f32)
    return {
        "x_prompt": nrm(ks[1], (BATCH, SEQ, D_MODEL), 1.0),
        "x_sample": nrm(ks[2], (DEC_BATCH, DEC_SEQ, D_MODEL), 1.0),
        "cache_k": nrm(ks[3], (DEPTH, n_pool, PAGE_SIZE, N_HEADS, HEAD_DIM), 1.0),
        "cache_v": nrm(ks[4], (DEPTH, n_pool, PAGE_SIZE, N_HEADS, HEAD_DIM), 1.0),
        "state_ssm_re": nrm(ks[5], (DEPTH, DEC_BATCH, N_GROUPS, STATE_DIM), 0.5),
        "state_ssm_im": nrm(ks[6], (DEPTH, DEC_BATCH, N_GROUPS, STATE_DIM), 0.5),
        "page_table": page_table,
        "norm1_g": 1.0 + nrm(ks[7], (DEPTH, D_MODEL), 0.02),
        "w_in": nrm(ks[8], (DEPTH, D_MODEL, IN_WIDTH), D_MODEL ** -0.5),
        "q_norm_g": 1.0 + nrm(ks[9], (DEPTH, HEAD_DIM), 0.02),
        "k_norm_g": 1.0 + nrm(ks[10], (DEPTH, HEAD_DIM), 0.02),
        "ssm_a_re": -0.5 + nrm(ks[11], (DEPTH, N_GROUPS, STATE_DIM), 0.01),
        "ssm_a_im": a_im_init + nrm(ks[12], (DEPTH, N_GROUPS, STATE_DIM), 0.01),
        "ssm_log_dt": jax.random.uniform(ks[13], (DEPTH, N_GROUPS), f32, math.log(1e-3), math.log(1e-1)),
        "ssm_b_re": nrm(ks[14], (DEPTH, N_GROUPS, STATE_DIM, GROUP_CH), (2 * GROUP_CH) ** -0.5),
        "ssm_b_im": nrm(ks[15], (DEPTH, N_GROUPS, STATE_DIM, GROUP_CH), (2 * GROUP_CH) ** -0.5),
        "ssm_c_re": nrm(ks[16], (DEPTH, N_GROUPS, GROUP_CH, STATE_DIM), (2 * STATE_DIM) ** -0.5),
        "ssm_c_im": nrm(ks[17], (DEPTH, N_GROUPS, GROUP_CH, STATE_DIM), (2 * STATE_DIM) ** -0.5),
        "ssm_d": nrm(ks[18], (DEPTH, SSM_WIDTH), 1.0),
        "w_attn_up": nrm(ks[19], (DEPTH, ATTN_WIDTH, D_MODEL), ATTN_WIDTH ** -0.5),
        "w_ssm_glu": nrm(ks[20], (DEPTH, SSM_WIDTH, 2 * D_MODEL), SSM_WIDTH ** -0.5),
        "w_out": nrm(ks[21], (DEPTH, D_MODEL, D_MODEL), D_MODEL ** -0.5),
        "norm2_g": 1.0 + nrm(ks[22], (DEPTH, D_MODEL), 0.02),
        "w_router": nrm(ks[23], (DEPTH, D_MODEL, N_EXPERTS), D_MODEL ** -0.5),
        "b_router": nrm(ks[24], (DEPTH, N_EXPERTS), 0.01),
        "w_gate": nrm(ks[25], (DEPTH, N_EXPERTS, D_MODEL, D_FF), D_MODEL ** -0.5),
        "b_gate": nrm(ks[26], (DEPTH, N_EXPERTS, D_FF), 0.01),
        "w_up": nrm(ks[27], (DEPTH, N_EXPERTS, D_MODEL, D_FF), D_MODEL ** -0.5),
        "b_up": nrm(ks[28], (DEPTH, N_EXPERTS, D_FF), 0.01),
        "w_down": nrm(ks[29], (DEPTH, N_EXPERTS, D_FF, D_MODEL), D_FF ** -0.5),
        "b_down": nrm(ks[30], (DEPTH, N_EXPERTS, D_MODEL), 0.01),
    }


def reference(x_prompt, x_sample, cache_k, cache_v, state_ssm_re, state_ssm_im, page_table,
              norm1_g, w_in, q_norm_g, k_norm_g,
              ssm_a_re, ssm_a_im, ssm_log_dt, ssm_b_re, ssm_b_im, ssm_c_re, ssm_c_im, ssm_d,
              w_attn_up, w_ssm_glu, w_out, norm2_g,
              w_router, b_router, w_gate, b_gate, w_up, b_up, w_down, b_down):
    dec_batch, n_pages = page_table.shape
    past_len = n_pages * cache_k.shape[2]
    weights = (norm1_g, w_in, q_norm_g, k_norm_g,
               ssm_a_re, ssm_a_im, ssm_log_dt, ssm_b_re, ssm_b_im, ssm_c_re, ssm_c_im, ssm_d,
               w_attn_up, w_ssm_glu, w_out, norm2_g,
               w_router, b_router, w_gate, b_gate, w_up, b_up, w_down, b_down)
    y_p, y_s = x_prompt, x_sample
    kp, vp, hrp, hip, ksl, vsl, hrs, his = [], [], [], [], [], [], [], []
    zeros_state = jnp.zeros((x_prompt.shape[0], N_GROUPS, STATE_DIM), jnp.float32)
    for layer in range(DEPTH):
        lw = [w[layer] for w in weights]
        y_p, k_new, v_new, h_re, h_im = _decoder_layer(y_p, 0, None, None, zeros_state, zeros_state, *lw)
        kp.append(k_new); vp.append(v_new); hrp.append(h_re); hip.append(h_im)
        k_past = cache_k[layer][page_table].reshape(dec_batch, past_len, N_HEADS, HEAD_DIM)
        v_past = cache_v[layer][page_table].reshape(dec_batch, past_len, N_HEADS, HEAD_DIM)
        y_s, k_new, v_new, h_re, h_im = _decoder_layer(y_s, past_len, k_past, v_past,
                                                       state_ssm_re[layer], state_ssm_im[layer], *lw)
        ksl.append(k_new); vsl.append(v_new); hrs.append(h_re); his.append(h_im)
    return (y_p, y_s, jnp.stack(kp), jnp.stack(vp), jnp.stack(hrp), jnp.stack(hip),
            jnp.stack(ksl), jnp.stack(vsl), jnp.stack(hrs), jnp.stack(his))
```

```python
import functools
import math

import jax
import jax.numpy as jnp
from jax import lax
from jax.experimental import pallas as pl
from jax.experimental.pallas import tpu as pltpu

F32 = jnp.float32
BF16 = jnp.bfloat16

D_MODEL = 1024
N_HEADS = 8
HEAD_DIM = 64
ATTN_W = N_HEADS * HEAD_DIM
MOBA_BLOCK = 256
MOBA_TOP = 3
ROPE_THETA = 10000.0
SSM_W = 512
GROUP_CH = 16
N_GROUPS = 32
STATE_DIM = 64
N_EXPERTS = 32
TOP_K = 4
SWIGLU_ALPHA = 1.702
SWIGLU_LIMIT = 7.0
EPS = 1e-6
IN_W = 4096
NEG_BIG = -1e9

LANES = 128
ROW_TILE = 512
S5_CHUNK = 32
MOE_TILE = 768
VMEM_LIMIT = 56 * 1024 * 1024


def _cparams(sem):
    return pltpu.CompilerParams(dimension_semantics=sem, vmem_limit_bytes=VMEM_LIMIT)


def _dot(a, b):
    return jnp.dot(a, b, preferred_element_type=F32)


def _dot_nt(a, b):
    return lax.dot_general(a, b, (((1,), (1,)), ((), ())), preferred_element_type=F32)


def _split_hi_lo(a):
    hi = a.astype(BF16)
    lo = (a - hi.astype(F32)).astype(BF16)
    return hi, lo


def _dot_hilo(a, b_bf16):
    hi, lo = _split_hi_lo(a)
    return _dot(hi, b_bf16) + _dot(lo, b_bf16)


def _row_dot_hilo(row, b_bf16):
    return _dot_hilo(jnp.broadcast_to(row, (8, row.shape[1])), b_bf16)[0:1, :]


def _head_indicator():
    c = jnp.arange(ATTN_W)[:, None] // HEAD_DIM
    h = jnp.arange(LANES)[None, :]
    hd = (c == h).astype(BF16)
    return hd, hd.T


def _head_norm_rope(t, gain, cos, sin_signed, hd, hdt):
    ssq = _dot_hilo(t * t, hd)
    rstd = lax.rsqrt(ssq * (1.0 / HEAD_DIM) + EPS)
    t = t * _dot_hilo(rstd, hdt) * gain
    lane = lax.broadcasted_iota(jnp.int32, (t.shape[0], LANES), 1)
    first_half = (lane % HEAD_DIM) < (HEAD_DIM // 2)
    outs = []
    for s in range(ATTN_W // LANES):
        ts = t[:, s * LANES:(s + 1) * LANES]
        partner = jnp.where(first_half,
                            pltpu.roll(ts, LANES - HEAD_DIM // 2, axis=1),
                            pltpu.roll(ts, HEAD_DIM // 2, axis=1))
        outs.append(ts * cos[:, s * LANES:(s + 1) * LANES]
                    + partner * sin_signed[:, s * LANES:(s + 1) * LANES])
    return jnp.concatenate(outs, axis=1)


def _inproj_kernel(x_ref, g1_ref, w_ref, cos_ref, sin_ref, qg_ref, kg_ref, hd_ref, hdt_ref,
                   q_ref, k_ref, v_ref, u_ref, ga_ref, gs_ref):
    x = x_ref[...]
    xn = x * lax.rsqrt(jnp.mean(x * x, axis=-1, keepdims=True) + EPS) * g1_ref[...]
    xb = xn.astype(BF16)
    hd = hd_ref[...]
    hdt = hdt_ref[...]
    cos = cos_ref[...]
    sin = sin_ref[...]
    q = _dot(xb, w_ref[:, 0:ATTN_W])
    q_ref[...] = _head_norm_rope(q, qg_ref[...], cos, sin, hd, hdt)
    k = _dot(xb, w_ref[:, ATTN_W:2 * ATTN_W])
    k_ref[...] = _head_norm_rope(k, kg_ref[...], cos, sin, hd, hdt)
    v_ref[...] = _dot(xb, w_ref[:, 2 * ATTN_W:3 * ATTN_W])
    u_ref[...] = _dot(xb, w_ref[:, 3 * ATTN_W:3 * ATTN_W + SSM_W])
    ga_ref[...] = _dot(xb, w_ref[:, 2048:3072])
    gs_ref[...] = _dot(xb, w_ref[:, 3072:4096])


def _inproj(x_all, norm1_g, w_in_bf16, cos_tab, sin_tab, q_gain, k_gain, hd, hdt, n_prompt_tiles,
            tiles_per_seq):
    t_pad = x_all.shape[0]
    n_tiles = t_pad // ROW_TILE

    def pos_map(i):
        return (jnp.where(i < n_prompt_tiles, i % tiles_per_seq, tiles_per_seq), 0)

    row = lambda w: pl.BlockSpec((ROW_TILE, w), lambda i: (i, 0))
    const = lambda shape: pl.BlockSpec(shape, lambda i: (0, 0))
    out_shape = ([jax.ShapeDtypeStruct((t_pad, ATTN_W), F32)] * 4
                 + [jax.ShapeDtypeStruct((t_pad, D_MODEL), F32)] * 2)
    return pl.pallas_call(
        _inproj_kernel,
        grid=(n_tiles,),
        in_specs=[row(D_MODEL), const((1, D_MODEL)), const((D_MODEL, IN_W)),
                  pl.BlockSpec((ROW_TILE, ATTN_W), pos_map), pl.BlockSpec((ROW_TILE, ATTN_W), pos_map),
                  const((1, ATTN_W)), const((1, ATTN_W)), const((ATTN_W, LANES)), const((LANES, ATTN_W))],
        out_specs=[row(ATTN_W)] * 4 + [row(D_MODEL)] * 2,
        out_shape=out_shape,
        compiler_params=_cparams(("parallel",)),
        name="inproj",
    )(x_all, norm1_g, w_in_bf16, cos_tab, sin_tab, q_gain, k_gain, hd, hdt)


def _block_penalty_t(gate_t, n_past, own_blk):
    blk = lax.broadcasted_iota(jnp.int32, gate_t.shape, 0)
    valid = blk < n_past
    g = jnp.where(valid, gate_t, -jnp.inf)
    cnt = jnp.zeros(gate_t.shape, jnp.int32)
    for m in range(gate_t.shape[0]):
        gm = g[m:m + 1, :]
        beats = jnp.where(gm > g, 1, jnp.where((gm == g) & (blk > m), 1, 0))
        cnt = cnt + beats
    keep = (valid & (cnt < MOBA_TOP)) | (blk == own_blk)
    return jnp.where(keep, 0.0, NEG_BIG)


def _attn_prompt_kernel(q_ref, k_ref, v_ref, o_ref, kaug_ref, vb_ref, kmean_ref, m_ref, l_ref, acc_ref):
    j = pl.program_id(2)
    nblk = k_ref.shape[0] // MOBA_BLOCK

    @pl.when(j == 0)
    def _():
        k = k_ref[...]
        kaug_ref[:, 0:LANES] = k.astype(BF16)
        row_blk = lax.broadcasted_iota(jnp.int32, (k.shape[0], LANES), 0) // MOBA_BLOCK
        lane = lax.broadcasted_iota(jnp.int32, (k.shape[0], LANES), 1)
        kaug_ref[:, LANES:2 * LANES] = jnp.where(row_blk == lane, 1.0, 0.0).astype(BF16)
        vb_ref[...] = v_ref[...].astype(BF16)
        for n in range(nblk):
            kmean_ref[n:n + 1, :] = jnp.mean(k[n * MOBA_BLOCK:(n + 1) * MOBA_BLOCK], axis=0, keepdims=True)

    q = q_ref[...]
    lane = lax.broadcasted_iota(jnp.int32, q.shape, 1)
    km_hi, km_lo = _split_hi_lo(kmean_ref[...])
    zpad = jnp.zeros((LANES - nblk, MOBA_BLOCK), F32)
    qaug = []
    for h in range(2):
        qh = jnp.where(lane >= HEAD_DIM if h == 1 else lane < HEAD_DIM, q, 0.0)
        q_hi, q_lo = _split_hi_lo(qh)
        gate_t = _dot_nt(km_hi, q_hi) + _dot_nt(km_hi, q_lo) + _dot_nt(km_lo, q_hi)
        pen_t = _block_penalty_t(gate_t, j, j)
        pen = jnp.concatenate([pen_t, zpad], axis=0).T
        qaug.append(jnp.concatenate([(qh * (1.0 / math.sqrt(HEAD_DIM))).astype(BF16),
                                     pen.astype(BF16)], axis=1))
        m_ref[h] = jnp.full(m_ref.shape[1:], -jnp.inf, F32)
        l_ref[h] = jnp.zeros(l_ref.shape[1:], F32)
        acc_ref[h] = jnp.zeros(acc_ref.shape[1:], F32)

    row = lax.broadcasted_iota(jnp.int32, (MOBA_BLOCK, MOBA_BLOCK), 0)
    col = lax.broadcasted_iota(jnp.int32, (MOBA_BLOCK, MOBA_BLOCK), 1)

    def body(n, carry):
        start = pl.multiple_of(n * MOBA_BLOCK, MOBA_BLOCK)
        kb = kaug_ref[pl.ds(start, MOBA_BLOCK), :]
        vb = vb_ref[pl.ds(start, MOBA_BLOCK), :]
        future = (col - row) > jnp.where(n == j, 0, MOBA_BLOCK)
        for h in range(2):
            s = _dot_nt(qaug[h], kb)
            s = jnp.where(future, NEG_BIG, s)
            m_prev = m_ref[h]
            m_new = jnp.maximum(m_prev, jnp.max(s, axis=1, keepdims=True))
            p = jnp.exp(s - m_new)
            alpha = jnp.exp(m_prev - m_new)
            l_ref[h] = alpha * l_ref[h] + jnp.sum(p, axis=1, keepdims=True)
            acc_ref[h] = alpha * acc_ref[h] + _dot(p.astype(BF16), vb)
            m_ref[h] = m_new
        return carry

    lax.fori_loop(0, j + 1, body, 0)
    out0 = acc_ref[0] / l_ref[0]
    out1 = acc_ref[1] / l_ref[1]
    o_ref[...] = jnp.where(lane < HEAD_DIM, out0, out1)


def _attn_prompt(q, k, v, batch, seq):
    nq = seq // MOBA_BLOCK
    return pl.pallas_call(
        _attn_prompt_kernel,
        grid=(batch, ATTN_W // LANES, nq),
        in_specs=[pl.BlockSpec((MOBA_BLOCK, LANES), lambda b, hp, j: (b * nq + j, hp)),
                  pl.BlockSpec((seq, LANES), lambda b, hp, j: (b, hp)),
                  pl.BlockSpec((seq, LANES), lambda b, hp, j: (b, hp))],
        out_specs=pl.BlockSpec((MOBA_BLOCK, LANES), lambda b, hp, j: (b * nq + j, hp)),
        out_shape=jax.ShapeDtypeStruct((batch * seq, ATTN_W), F32),
        scratch_shapes=[pltpu.VMEM((seq, 2 * LANES), BF16), pltpu.VMEM((seq, LANES), BF16),
                        pltpu.VMEM((seq // MOBA_BLOCK, LANES), F32),
                        pltpu.VMEM((2, MOBA_BLOCK, 1), F32), pltpu.VMEM((2, MOBA_BLOCK, 1), F32),
                        pltpu.VMEM((2, MOBA_BLOCK, LANES), F32)],
        compiler_params=_cparams(("parallel", "parallel", "arbitrary")),
        name="attn_prompt",
    )(q, k, v)


def _attn_sample_kernel(n_pages, pt_ref, q_ref, kn_ref, vn_ref, hd_ref, hdt_ref, *rest):
    del pt_ref
    k_pages = rest[:n_pages]
    v_pages = rest[n_pages:2 * n_pages]
    o_ref = rest[2 * n_pages]
    b = pl.program_id(0)
    page = k_pages[0].shape[1]
    ppb = MOBA_BLOCK // page
    nblk = n_pages // ppb
    hd = hd_ref[...]
    hdt = hdt_ref[...]
    q = q_ref[pl.ds(b, 1), :]
    k_new = kn_ref[pl.ds(b, 1), :]
    v_new = vn_ref[pl.ds(b, 1), :]
    scale = 1.0 / math.sqrt(HEAD_DIM)

    kblocks = [jnp.concatenate([k_pages[n * ppb + i][0] for i in range(ppb)], axis=0) for n in range(nblk)]
    kmq = jnp.concatenate([jnp.mean(kb, axis=0, keepdims=True) * q for kb in kblocks], axis=0)
    gate = _dot_hilo(kmq, hd)
    pen = _block_penalty_t(gate, nblk, -1)

    s_blocks = []
    m = _row_dot_hilo(k_new * q, hd) * scale
    s_new = m
    for n in range(nblk):
        s = _dot((kblocks[n] * q).astype(BF16), hd) * scale + pen[n:n + 1, :]
        s_blocks.append(s)
        m = jnp.maximum(m, jnp.max(s, axis=0, keepdims=True))
    p_new = jnp.exp(s_new - m)
    l = p_new
    acc = _row_dot_hilo(p_new, hdt) * v_new
    for n in range(nblk):
        p = jnp.exp(s_blocks[n] - m)
        l = l + jnp.sum(p, axis=0, keepdims=True)
        vblock = jnp.concatenate([v_pages[n * ppb + i][0] for i in range(ppb)], axis=0)
        acc = acc + jnp.sum(_dot(p.astype(BF16), hdt) * vblock, axis=0, keepdims=True)
    o_ref[pl.ds(b, 1), :] = acc / _row_dot_hilo(l, hdt)


def _attn_sample(page_table, q_s, k_s, v_s, cache_k, cache_v, hd, hdt):
    dec_batch, n_pages = page_table.shape
    page = cache_k.shape[1]

    def page_spec(p):
        return pl.BlockSpec((1, page, ATTN_W), lambda b, pt: (pt[b, p], 0, 0))

    const2 = lambda shape: pl.BlockSpec(shape, lambda b, pt: (0, 0))
    grid_spec = pltpu.PrefetchScalarGridSpec(
        num_scalar_prefetch=1,
        grid=(dec_batch,),
        in_specs=[const2((dec_batch, ATTN_W))] * 3 + [const2((ATTN_W, LANES)), const2((LANES, ATTN_W))]
                 + [page_spec(p) for p in range(n_pages)] * 2,
        out_specs=const2((dec_batch, ATTN_W)),
    )
    return pl.pallas_call(
        functools.partial(_attn_sample_kernel, n_pages),
        grid_spec=grid_spec,
        out_shape=jax.ShapeDtypeStruct((dec_batch, ATTN_W), F32),
        compiler_params=_cparams(("arbitrary",)),
        name="attn_sample",
    )(page_table, q_s, k_s, v_s, hd, hdt, *([cache_k] * n_pages), *([cache_v] * n_pages))


def _s5_discretize(a_re, a_im, log_dt, b_re, b_im):
    lam_re = a_re.astype(F32)
    lam_im = a_im.astype(F32)
    dt = jnp.exp(log_dt.astype(F32))[:, None]
    mag = jnp.exp(lam_re * dt)
    ab_re = mag * jnp.cos(lam_im * dt)
    ab_im = mag * jnp.sin(lam_im * dt)
    den = lam_re * lam_re + lam_im * lam_im
    n_re, n_im = ab_re - 1.0, ab_im
    f_re = (n_re * lam_re + n_im * lam_im) / den
    f_im = (n_im * lam_re - n_re * lam_im) / den
    br, bi = b_re.astype(F32), b_im.astype(F32)
    bb_re = f_re[..., None] * br - f_im[..., None] * bi
    bb_im = f_re[..., None] * bi + f_im[..., None] * br
    return lam_re, lam_im, dt, ab_re, ab_im, bb_re, bb_im


def _s5_chunk_operators(lam_re, lam_im, dt, bb_re, bb_im, c_re, c_im):
    L = S5_CHUNK
    hp = lax.Precision.HIGHEST
    steps = jnp.arange(L + 1, dtype=F32)[:, None, None]
    pmag = jnp.exp(lam_re[None] * dt[None] * steps)
    pw_re = pmag * jnp.cos(lam_im[None] * dt[None] * steps)
    pw_im = pmag * jnp.sin(lam_im[None] * dt[None] * steps)
    cr, ci = c_re.astype(F32), c_im.astype(F32)
    cp_re = cr[None] * pw_re[:L, :, None, :] - ci[None] * pw_im[:L, :, None, :]
    cp_im = cr[None] * pw_im[:L, :, None, :] + ci[None] * pw_re[:L, :, None, :]
    kern = (jnp.einsum('jgcp,gpd->jgcd', cp_re, bb_re, precision=hp)
            - jnp.einsum('jgcp,gpd->jgcd', cp_im, bb_im, precision=hp))
    s_idx = jnp.arange(L)[:, None]
    t_idx = jnp.arange(L)[None, :]
    lag = jnp.clip(t_idx - s_idx, 0, L - 1)
    toep = jnp.where((t_idx >= s_idx)[:, :, None, None, None], kern[lag], 0.0)
    toep = toep.transpose(2, 0, 4, 1, 3).reshape(N_GROUPS, L * GROUP_CH, L * GROUP_CH)
    rev_re = pw_re[L - 1 - jnp.arange(L)]
    rev_im = pw_im[L - 1 - jnp.arange(L)]
    w1_re = rev_re[:, :, :, None] * bb_re[None] - rev_im[:, :, :, None] * bb_im[None]
    w1_im = rev_re[:, :, :, None] * bb_im[None] + rev_im[:, :, :, None] * bb_re[None]
    w1_re = w1_re.transpose(1, 0, 3, 2).reshape(N_GROUPS, L * GROUP_CH, STATE_DIM)
    w1_im = w1_im.transpose(1, 0, 3, 2).reshape(N_GROUPS, L * GROUP_CH, STATE_DIM)
    nx_re = pw_re[1:]
    nx_im = pw_im[1:]
    w2_re = cr[None] * nx_re[:, :, None, :] - ci[None] * nx_im[:, :, None, :]
    w2_im = -(cr[None] * nx_im[:, :, None, :] + ci[None] * nx_re[:, :, None, :])
    w2_re = w2_re.transpose(1, 3, 0, 2).reshape(N_GROUPS, STATE_DIM, L * GROUP_CH)
    w2_im = w2_im.transpose(1, 3, 0, 2).reshape(N_GROUPS, STATE_DIM, L * GROUP_CH)
    return (toep.astype(BF16), w1_re.astype(BF16), w1_im.astype(BF16), w2_re.astype(BF16),
            w2_im.astype(BF16), pw_re[L][:, None, :], pw_im[L][:, None, :])


def _s5_prompt_kernel(batch, u_ref, toep_ref, w1r_ref, w1i_ref, w2r_ref, w2i_ref, ar_ref, ai_ref, d_ref,
                      y_ref, hr_ref, hi_ref, dre_ref, dim_ref, pre_ref, pim_ref):
    u = u_ref[0]
    ub = u.astype(BF16)
    dre_ref[...] = _dot(ub, w1r_ref[0])
    dim_ref[...] = _dot(ub, w1i_ref[0])
    a_re = ar_ref[0]
    a_im = ai_ref[0]
    n_chunks = u.shape[0] // batch

    def body(c, carry):
        h_re, h_im = carry
        rows = pl.ds(pl.multiple_of(c * batch, batch), batch)
        pre_ref[rows, :] = h_re
        pim_ref[rows, :] = h_im
        return (a_re * h_re - a_im * h_im + dre_ref[rows, :],
                a_re * h_im + a_im * h_re + dim_ref[rows, :])

    zero = jnp.zeros((batch, STATE_DIM), F32)
    h_re, h_im = lax.fori_loop(0, n_chunks, body, (zero, zero))
    hr_ref[0] = h_re
    hi_ref[0] = h_im
    y_ref[0] = (_dot(ub, toep_ref[0]) + _dot(pre_ref[...].astype(BF16), w2r_ref[0])
                + _dot(pim_ref[...].astype(BF16), w2i_ref[0]) + d_ref[0] * u)


def _s5_prompt(u_t, toep, w1_re, w1_im, w2_re, w2_im, al_re, al_im, d_tiled, batch):
    g, rows, cols = u_t.shape
    blk = lambda s: pl.BlockSpec((1,) + s, lambda i: (i, 0, 0))
    return pl.pallas_call(
        functools.partial(_s5_prompt_kernel, batch),
        grid=(g,),
        in_specs=[blk((rows, cols)), blk((cols, cols)), blk((cols, STATE_DIM)), blk((cols, STATE_DIM)),
                  blk((STATE_DIM, cols)), blk((STATE_DIM, cols)), blk((1, STATE_DIM)), blk((1, STATE_DIM)),
                  blk((1, cols))],
        out_specs=[blk((rows, cols)), blk((batch, STATE_DIM)), blk((batch, STATE_DIM))],
        out_shape=[jax.ShapeDtypeStruct((g, rows, cols), F32),
                   jax.ShapeDtypeStruct((g, batch, STATE_DIM), F32),
                   jax.ShapeDtypeStruct((g, batch, STATE_DIM), F32)],
        scratch_shapes=[pltpu.VMEM((rows, STATE_DIM), F32)] * 4,
        compiler_params=_cparams(("parallel",)),
        name="s5_prompt",
    )(u_t, toep, w1_re, w1_im, w2_re, w2_im, al_re, al_im, d_tiled)


def _s5_sample_kernel(u_ref, h0r_ref, h0i_ref, bdr_ref, bdi_ref, cdr_ref, cdi_ref, ar_ref, ai_ref, d_ref,
                      y_ref, hr_ref, hi_ref):
    u = u_ref[...]
    ub = u.astype(BF16)
    a_re = ar_ref[...]
    a_im = ai_ref[...]
    h0r = h0r_ref[...]
    h0i = h0i_ref[...]
    h_re = a_re * h0r - a_im * h0i + _dot(ub, bdr_ref[...])
    h_im = a_re * h0i + a_im * h0r + _dot(ub, bdi_ref[...])
    hr_ref[...] = h_re
    hi_ref[...] = h_im
    y_ref[...] = (_dot(h_re.astype(BF16), cdr_ref[...]) - _dot(h_im.astype(BF16), cdi_ref[...])
                  + d_ref[...] * u)


def _s5_sample(u_s, h0_re, h0_im, bd_re, bd_im, cd_re, cd_im, ab_re, ab_im, d_skip):
    n = u_s.shape[0]
    width = N_GROUPS * STATE_DIM
    return pl.pallas_call(
        _s5_sample_kernel,
        out_shape=[jax.ShapeDtypeStruct((n, SSM_W), F32), jax.ShapeDtypeStruct((n, width), F32),
                   jax.ShapeDtypeStruct((n, width), F32)],
        compiler_params=pltpu.CompilerParams(vmem_limit_bytes=VMEM_LIMIT),
        name="s5_sample",
    )(u_s, h0_re, h0_im, bd_re, bd_im, cd_re, cd_im, ab_re, ab_im, d_skip)


def _block_diag(m):
    g, r, c = m.shape
    eye = jnp.eye(g, dtype=m.dtype)
    return (m[:, :, None, :] * eye[:, None, :, None]).reshape(g * r, g * c)


def _merge_kernel(x_ref, attn_ref, yssm_ref, ga_ref, gs_ref, wau_ref, wglu_ref, wout_ref, g2_ref,
                  wr_ref, br_ref, h_ref, xn_ref, comb_ref):
    branch_attn = _dot(attn_ref[...].astype(BF16), wau_ref[...])
    glu = _dot(jax.nn.gelu(yssm_ref[...]).astype(BF16), wglu_ref[...])
    branch_ssm = glu[:, :D_MODEL] * jax.nn.sigmoid(glu[:, D_MODEL:])
    merged = jax.nn.sigmoid(ga_ref[...]) * branch_attn + jax.nn.sigmoid(gs_ref[...]) * branch_ssm
    h = x_ref[...] + _dot(merged.astype(BF16), wout_ref[...])
    h_ref[...] = h
    xn = h * lax.rsqrt(jnp.mean(h * h, axis=-1, keepdims=True) + EPS) * g2_ref[...]
    xn_ref[...] = xn.astype(BF16)

    x_hi, x_lo = _split_hi_lo(xn)
    w_hi, w_lo = _split_hi_lo(wr_ref[...])
    logits = _dot(x_hi, w_hi) + _dot(x_lo, w_hi) + _dot(x_hi, w_lo)
    lt = logits.T[0:N_EXPERTS, :] + br_ref[...]
    e_iota = lax.broadcasted_iota(jnp.int32, lt.shape, 0).astype(F32)
    tops, idxs = [], []
    for _ in range(TOP_K):
        m = jnp.max(lt, axis=0, keepdims=True)
        idx = jnp.min(jnp.where(lt == m, e_iota, float(N_EXPERTS)), axis=0, keepdims=True)
        tops.append(m)
        idxs.append(idx)
        lt = jnp.where(e_iota == idx, -jnp.inf, lt)
    exps = [jnp.exp(t - tops[0]) for t in tops]
    denom = exps[0] + exps[1] + exps[2] + exps[3]
    comb_t = jnp.zeros(lt.shape, F32)
    for k in range(TOP_K):
        comb_t = comb_t + jnp.where(e_iota == idxs[k], exps[k] / denom, 0.0)
    comb_t = jnp.concatenate([comb_t, jnp.zeros((LANES - N_EXPERTS, comb_t.shape[1]), F32)], axis=0)
    comb_ref[...] = comb_t.T


def _merge(x_all, attn, y_ssm, g_attn, g_ssm, w_attn_up, w_ssm_glu, w_out, norm2_g, w_router_pad,
           b_router_col):
    t_pad = x_all.shape[0]
    row = lambda w: pl.BlockSpec((ROW_TILE, w), lambda i: (i, 0))
    const = lambda shape: pl.BlockSpec(shape, lambda i: (0, 0))
    return pl.pallas_call(
        _merge_kernel,
        grid=(t_pad // ROW_TILE,),
        in_specs=[row(D_MODEL), row(ATTN_W), row(SSM_W), row(D_MODEL), row(D_MODEL),
                  const((ATTN_W, D_MODEL)), const((SSM_W, 2 * D_MODEL)), const((D_MODEL, D_MODEL)),
                  const((1, D_MODEL)), const((D_MODEL, LANES)), const((N_EXPERTS, 1))],
        out_specs=[row(D_MODEL), row(D_MODEL), row(LANES)],
        out_shape=[jax.ShapeDtypeStruct((t_pad, D_MODEL), F32), jax.ShapeDtypeStruct((t_pad, D_MODEL), BF16),
                   jax.ShapeDtypeStruct((t_pad, LANES), F32)],
        compiler_params=_cparams(("parallel",)),
        name="merge_router",
    )(x_all, attn, y_ssm, g_attn, g_ssm, w_attn_up, w_ssm_glu, w_out, norm2_g, w_router_pad, b_router_col)


FF_CHUNK = 256


def _moe_dense_kernel(x_ref, h_ref, comb_ref, wg_ref, bg_ref, wu_ref, bu_ref, wd_ref, bd_ref, o_ref, y_ref):
    e = pl.program_id(1)
    x = x_ref[...]
    d_ff = wg_ref.shape[2]
    for c in range(d_ff // FF_CHUNK):
        cols = slice(c * FF_CHUNK, (c + 1) * FF_CHUNK)
        gate = _dot(x, wg_ref[0, :, cols].astype(BF16)) + bg_ref[0, :, cols]
        up = _dot(x, wu_ref[0, :, cols].astype(BF16)) + bu_ref[0, :, cols]
        gate = jnp.minimum(gate, SWIGLU_LIMIT)
        up = jnp.clip(up, -SWIGLU_LIMIT, SWIGLU_LIMIT)
        hid = (up + 1.0) * (gate * jax.nn.sigmoid(SWIGLU_ALPHA * gate))
        part = _dot(hid.astype(BF16), wd_ref[0, cols, :].astype(BF16))
        if c == 0:
            y_ref[...] = part + bd_ref[0]
        else:
            y_ref[...] += part
    comb = comb_ref[...]
    lane = lax.broadcasted_iota(jnp.int32, comb.shape, 1)
    cw = jnp.sum(jnp.where(lane == e, comb, 0.0), axis=1, keepdims=True)

    @pl.when(e == 0)
    def _():
        o_ref[...] = h_ref[...] + cw * y_ref[...]

    @pl.when(e != 0)
    def _():
        o_ref[...] += cw * y_ref[...]


def _moe_dense(xn, h, comb, w_gate, b_gate, w_up, b_up, w_down, b_down):
    t_pad = xn.shape[0]
    n_exp, d_model, d_ff = w_gate.shape
    row = lambda w: pl.BlockSpec((MOE_TILE, w), lambda i, e: (i, 0))
    wspec = lambda r, c: pl.BlockSpec((1, r, c), lambda i, e: (e, 0, 0))
    return pl.pallas_call(
        _moe_dense_kernel,
        grid=(t_pad // MOE_TILE, n_exp),
        in_specs=[row(d_model), row(d_model), row(LANES), wspec(d_model, d_ff), wspec(1, d_ff),
                  wspec(d_model, d_ff), wspec(1, d_ff), wspec(d_ff, d_model), wspec(1, d_model)],
        out_specs=row(d_model),
        out_shape=jax.ShapeDtypeStruct((t_pad, d_model), F32),
        scratch_shapes=[pltpu.VMEM((MOE_TILE, d_model), F32)],
        compiler_params=_cparams(("parallel", "arbitrary")),
        name="moe_dense",
    )(xn, h, comb, w_gate, b_gate.reshape(n_exp, 1, d_ff), w_up, b_up.reshape(n_exp, 1, d_ff),
      w_down, b_down.reshape(n_exp, 1, d_model))


def _rope_tables(seq, past_len):
    half = HEAD_DIM // 2
    inv_freq = ROPE_THETA ** (-jnp.arange(half, dtype=F32) / half)
    pos = jnp.concatenate([jnp.arange(seq, dtype=jnp.int32),
                           jnp.full((ROW_TILE,), past_len, jnp.int32)]).astype(F32)
    ang = pos[:, None] * inv_freq[None, :]
    cos = jnp.cos(ang)
    sin = jnp.sin(ang)
    cos_h = jnp.concatenate([cos, cos], axis=1)
    sin_h = jnp.concatenate([-sin, sin], axis=1)
    return jnp.tile(cos_h, (1, N_HEADS)), jnp.tile(sin_h, (1, N_HEADS))


def kernel(x_prompt, x_sample, cache_k, cache_v, state_ssm_re, state_ssm_im, page_table, norm1_g, w_in,
           q_norm_g, k_norm_g, ssm_a_re, ssm_a_im, ssm_log_dt, ssm_b_re, ssm_b_im, ssm_c_re, ssm_c_im,
           ssm_d, w_attn_up, w_ssm_glu, w_out, norm2_g, w_router, b_router, w_gate, b_gate, w_up, b_up,
           w_down, b_down):
    batch, seq, d_model = x_prompt.shape
    dec_batch = x_sample.shape[0]
    n_pages = page_table.shape[1]
    page = cache_k.shape[2]
    past_len = n_pages * page
    assert x_sample.shape[1] == 1 and w_in.shape[0] == 1 and seq % ROW_TILE == 0
    t_prompt = batch * seq
    t_all = t_prompt + dec_batch
    t_pad = -(-t_all // (2 * MOE_TILE)) * (2 * MOE_TILE)
    assert t_pad % ROW_TILE == 0 and t_pad % MOE_TILE == 0

    x_all = jnp.concatenate([x_prompt.reshape(t_prompt, d_model), x_sample.reshape(dec_batch, d_model),
                             jnp.zeros((t_pad - t_all, d_model), F32)], axis=0)
    hd, hdt = _head_indicator()
    cos_tab, sin_tab = _rope_tables(seq, past_len)
    q, k, v, u, g_attn, g_ssm = _inproj(
        x_all, norm1_g, w_in[0].astype(BF16), cos_tab, sin_tab,
        jnp.tile(q_norm_g, (1, N_HEADS)), jnp.tile(k_norm_g, (1, N_HEADS)), hd, hdt,
        t_prompt // ROW_TILE, seq // ROW_TILE)

    attn_p = _attn_prompt(q, k, v, batch, seq)
    sl = slice(t_prompt, t_all)
    attn_s = _attn_sample(page_table, q[sl], k[sl], v[sl],
                          cache_k[0].reshape(-1, page, ATTN_W), cache_v[0].reshape(-1, page, ATTN_W), hd, hdt)

    lam_re, lam_im, dt, ab_re, ab_im, bb_re, bb_im = _s5_discretize(
        ssm_a_re[0], ssm_a_im[0], ssm_log_dt[0], ssm_b_re[0], ssm_b_im[0])
    toep, w1_re, w1_im, w2_re, w2_im, al_re, al_im = _s5_chunk_operators(
        lam_re, lam_im, dt, bb_re, bb_im, ssm_c_re[0], ssm_c_im[0])
    L = S5_CHUNK
    n_chunks = seq // L
    u_t = (u[:t_prompt].reshape(batch, n_chunks, L, N_GROUPS, GROUP_CH)
           .transpose(3, 1, 0, 2, 4).reshape(N_GROUPS, n_chunks * batch, L * GROUP_CH))
    d_tiled = jnp.tile(ssm_d[0].reshape(N_GROUPS, 1, GROUP_CH), (1, L, 1)).reshape(N_GROUPS, 1, L * GROUP_CH)
    y_t, hp_re, hp_im = _s5_prompt(u_t, toep, w1_re, w1_im, w2_re, w2_im, al_re, al_im, d_tiled, batch)
    y_p = (y_t.reshape(N_GROUPS, n_chunks, batch, L, GROUP_CH)
           .transpose(2, 1, 3, 0, 4).reshape(t_prompt, SSM_W))
    width = N_GROUPS * STATE_DIM
    y_s, hs_re, hs_im = _s5_sample(
        u[sl], state_ssm_re[0].reshape(dec_batch, width), state_ssm_im[0].reshape(dec_batch, width),
        _block_diag(bb_re.transpose(0, 2, 1)).astype(BF16), _block_diag(bb_im.transpose(0, 2, 1)).astype(BF16),
        _block_diag(ssm_c_re[0].transpose(0, 2, 1)).astype(BF16),
        _block_diag(ssm_c_im[0].transpose(0, 2, 1)).astype(BF16),
        ab_re.reshape(1, width), ab_im.reshape(1, width), ssm_d)

    pad_rows = jnp.zeros((t_pad - t_all, ATTN_W), F32)
    attn = jnp.concatenate([attn_p, attn_s, pad_rows], axis=0)
    y_ssm = jnp.concatenate([y_p, y_s, pad_rows], axis=0)

    w_router_pad = jnp.concatenate([w_router[0], jnp.zeros((d_model, LANES - N_EXPERTS), F32)], axis=1)
    h, xn2, comb = _merge(x_all, attn, y_ssm, g_attn, g_ssm, w_attn_up[0].astype(BF16),
                          w_ssm_glu[0].astype(BF16), w_out[0].astype(BF16), norm2_g, w_router_pad,
                          b_router[0].reshape(N_EXPERTS, 1))
    y = _moe_dense(xn2, h, comb, w_gate[0], b_gate[0], w_up[0], b_up[0], w_down[0], b_down[0])

    heads = (N_HEADS, HEAD_DIM)
    return (y[:t_prompt].reshape(batch, seq, d_model),
            y[sl].reshape(dec_batch, 1, d_model),
            k[:t_prompt].reshape((1, batch, seq) + heads),
            v[:t_prompt].reshape((1, batch, seq) + heads),
            hp_re.transpose(1, 0, 2)[None],
            hp_im.transpose(1, 0, 2)[None],
            k[sl].reshape((1, dec_batch, 1) + heads),
            v[sl].reshape((1, dec_batch, 1) + heads),
            hs_re.reshape(1, dec_batch, N_GROUPS, STATE_DIM),
            hs_im.reshape(1, dec_batch, N_GROUPS, STATE_DIM))
```

```python
import functools
import math

import jax
import jax.numpy as jnp
from jax import lax
from jax.experimental import pallas as pl
from jax.experimental.pallas import tpu as pltpu

F32 = jnp.float32
BF16 = jnp.bfloat16
I32 = jnp.int32

D_MODEL = 1024
N_HEADS = 8
HEAD_DIM = 64
ATTN_W = N_HEADS * HEAD_DIM
MOBA_BLOCK = 256
MOBA_TOP = 3
ROPE_THETA = 10000.0
SSM_W = 512
GROUP_CH = 16
N_GROUPS = 32
STATE_DIM = 64
N_EXPERTS = 32
TOP_K = 4
SWIGLU_ALPHA = 1.702
SWIGLU_LIMIT = 7.0
EPS = 1e-6
IN_W = 4096
NEG_BIG = -1e9

LANES = 128
SUBLANES = 8
ROW_TILE = 512
S5_CHUNK = 32
MOE_TM = 256
MOE_TOK = 256
FF_CHUNK = 512
VMEM_LIMIT = 56 * 1024 * 1024


def _cparams(sem):
    return pltpu.CompilerParams(dimension_semantics=sem, vmem_limit_bytes=VMEM_LIMIT)


def _dot(a, b):
    return jnp.dot(a, b, preferred_element_type=F32)


def _split_hi_lo(a):
    hi = a.astype(BF16)
    lo = (a - hi.astype(F32)).astype(BF16)
    return hi, lo


def _dot_hilo(a, b_bf16):
    hi, lo = _split_hi_lo(a)
    return _dot(hi, b_bf16) + _dot(lo, b_bf16)


def _head_indicator():
    c = jnp.arange(ATTN_W)[:, None] // HEAD_DIM
    h = jnp.arange(LANES)[None, :]
    hd = (c == h).astype(BF16)
    return hd, hd.T


def _head_norm_rope(t, gain, cos, sin_signed, hd, hdt):
    ssq = _dot_hilo(t * t, hd)
    rstd = lax.rsqrt(ssq * (1.0 / HEAD_DIM) + EPS)
    t = t * _dot_hilo(rstd, hdt) * gain
    lane = lax.broadcasted_iota(I32, (t.shape[0], LANES), 1)
    first_half = (lane % HEAD_DIM) < (HEAD_DIM // 2)
    outs = []
    for s in range(ATTN_W // LANES):
        ts = t[:, s * LANES:(s + 1) * LANES]
        partner = jnp.where(first_half,
                            pltpu.roll(ts, LANES - HEAD_DIM // 2, axis=1),
                            pltpu.roll(ts, HEAD_DIM // 2, axis=1))
        outs.append(ts * cos[:, s * LANES:(s + 1) * LANES]
                    + partner * sin_signed[:, s * LANES:(s + 1) * LANES])
    return jnp.concatenate(outs, axis=1)


def _inproj_kernel(x_ref, g1_ref, w_ref, cos_ref, sin_ref, qg_ref, kg_ref, hd_ref, hdt_ref,
                   q_ref, k_ref, v_ref, u_ref, ga_ref, gs_ref):
    x = x_ref[...]
    xn = x * lax.rsqrt(jnp.mean(x * x, axis=-1, keepdims=True) + EPS) * g1_ref[...]
    xb = xn.astype(BF16)
    hd = hd_ref[...]
    hdt = hdt_ref[...]
    cos = cos_ref[...]
    sin = sin_ref[...]
    q = _dot(xb, w_ref[:, 0:ATTN_W])
    q_ref[...] = _head_norm_rope(q, qg_ref[...], cos, sin, hd, hdt)
    k = _dot(xb, w_ref[:, ATTN_W:2 * ATTN_W])
    k_ref[...] = _head_norm_rope(k, kg_ref[...], cos, sin, hd, hdt)
    v_ref[...] = _dot(xb, w_ref[:, 2 * ATTN_W:3 * ATTN_W])
    u_ref[...] = _dot(xb, w_ref[:, 3 * ATTN_W:3 * ATTN_W + SSM_W])
    ga_ref[...] = _dot(xb, w_ref[:, 2048:3072])
    gs_ref[...] = _dot(xb, w_ref[:, 3072:4096])


def _inproj(x, norm1_g, w_in_bf16, cos_tab, sin_tab, q_gain, k_gain, hd, hdt, tile):
    t = x.shape[0]
    period = cos_tab.shape[0] // tile
    row = lambda w: pl.BlockSpec((tile, w), lambda i: (i, 0))
    const = lambda shape: pl.BlockSpec(shape, lambda i: (0, 0))
    pos = pl.BlockSpec((tile, ATTN_W), lambda i: (i % period, 0))
    out_shape = ([jax.ShapeDtypeStruct((t, ATTN_W), F32)] * 4
                 + [jax.ShapeDtypeStruct((t, D_MODEL), F32)] * 2)
    return pl.pallas_call(
        _inproj_kernel,
        grid=(t // tile,),
        in_specs=[row(D_MODEL), const((1, D_MODEL)), const((D_MODEL, IN_W)), pos, pos,
                  const((1, ATTN_W)), const((1, ATTN_W)), const((ATTN_W, LANES)), const((LANES, ATTN_W))],
        out_specs=[row(ATTN_W)] * 4 + [row(D_MODEL)] * 2,
        out_shape=out_shape,
        compiler_params=_cparams(("parallel",)),
        name="inproj",
    )(x, norm1_g, w_in_bf16, cos_tab, sin_tab, q_gain, k_gain, hd, hdt)


def _block_penalty_t(gate_t, n_past, own_blk):
    blk = lax.broadcasted_iota(I32, gate_t.shape, 0)
    valid = blk < n_past
    g = jnp.where(valid, gate_t, -jnp.inf)
    cnt = jnp.zeros(gate_t.shape, I32)
    for m in range(gate_t.shape[0]):
        gm = g[m:m + 1, :]
        beats = jnp.where(gm > g, 1, jnp.where((gm == g) & (blk > m), 1, 0))
        cnt = cnt + beats
    keep = (valid & (cnt < MOBA_TOP)) | (blk == own_blk)
    return jnp.where(keep, 0.0, NEG_BIG)


def _attn_prompt_kernel(q_ref, k_ref, v_ref, o_ref, kaug_ref, vt_ref, kmean_ref):
    j = pl.program_id(2)
    nblk = kaug_ref.shape[0]
    blk = MOBA_BLOCK

    @pl.when(j == 0)
    def _():
        row_blk = lax.broadcasted_iota(I32, (blk, LANES), 1)
        for n in range(nblk):
            k = k_ref[n * blk:(n + 1) * blk, :]
            kaug_ref[n, :, 0:LANES] = k.astype(BF16)
            kaug_ref[n, :, LANES:2 * LANES] = jnp.where(row_blk == n, 1.0, 0.0).astype(BF16)
            vt_ref[n] = v_ref[n * blk:(n + 1) * blk, :].T.astype(BF16)
            kmean_ref[n:n + 1, :] = jnp.mean(k, axis=0, keepdims=True)

    q_t = q_ref[...].T
    feat = lax.broadcasted_iota(I32, q_t.shape, 0)
    km_hi, km_lo = _split_hi_lo(kmean_ref[...])
    zpad = jnp.zeros((LANES - nblk, blk), F32)
    qaug = []
    for h in range(2):
        qh = jnp.where(feat >= HEAD_DIM if h == 1 else feat < HEAD_DIM, q_t, 0.0)
        q_hi, q_lo = _split_hi_lo(qh)
        gate_t = _dot(km_hi, q_hi) + _dot(km_hi, q_lo) + _dot(km_lo, q_hi)
        pen_t = jnp.concatenate([_block_penalty_t(gate_t, j, j), zpad], axis=0)
        qaug.append(jnp.concatenate([(qh * (1.0 / math.sqrt(HEAD_DIM))).astype(BF16),
                                     pen_t.astype(BF16)], axis=0))

    key_i = lax.broadcasted_iota(I32, (blk, blk), 0)
    qry_i = lax.broadcasted_iota(I32, (blk, blk), 1)
    future = key_i > qry_i

    def block(n, carry, diagonal):
        kb = kaug_ref[n]
        vt = vt_ref[n]
        out = []
        for h in range(2):
            m_prev, l_prev, acc = carry[h]
            s = _dot(kb, qaug[h])
            if diagonal:
                s = jnp.where(future, NEG_BIG, s)
            m_new = jnp.maximum(m_prev, jnp.max(s, axis=0, keepdims=True))
            p = jnp.exp(s - m_new)
            alpha = jnp.exp(m_prev - m_new)
            l_new = alpha * l_prev + jnp.sum(p, axis=0, keepdims=True)
            pv = _dot(vt[h * HEAD_DIM:(h + 1) * HEAD_DIM, :], p.astype(BF16))
            out.append((m_new, l_new, alpha * acc + pv))
        return tuple(out)

    init = tuple((jnp.full((1, blk), -jnp.inf, F32), jnp.zeros((1, blk), F32),
                  jnp.zeros((HEAD_DIM, blk), F32)) for _ in range(2))
    carry = lax.fori_loop(0, j, lambda n, c: block(n, c, False), init)
    carry = block(j, carry, True)
    out_t = jnp.concatenate([carry[0][2] / carry[0][1], carry[1][2] / carry[1][1]], axis=0)
    o_ref[...] = out_t.T


def _attn_prompt(q, k, v, batch, seq):
    nq = seq // MOBA_BLOCK
    return pl.pallas_call(
        _attn_prompt_kernel,
        grid=(batch, ATTN_W // LANES, nq),
        in_specs=[pl.BlockSpec((MOBA_BLOCK, LANES), lambda b, hp, j: (b * nq + j, hp)),
                  pl.BlockSpec((seq, LANES), lambda b, hp, j: (b, hp)),
                  pl.BlockSpec((seq, LANES), lambda b, hp, j: (b, hp))],
        out_specs=pl.BlockSpec((MOBA_BLOCK, LANES), lambda b, hp, j: (b * nq + j, hp)),
        out_shape=jax.ShapeDtypeStruct((batch * seq, ATTN_W), F32),
        scratch_shapes=[pltpu.VMEM((nq, MOBA_BLOCK, 2 * LANES), BF16),
                        pltpu.VMEM((nq, LANES, MOBA_BLOCK), BF16),
                        pltpu.VMEM((nq, LANES), F32)],
        compiler_params=_cparams(("parallel", "parallel", "arbitrary")),
        name="attn_prompt",
    )(q, k, v)


def _attn_sample_kernel(n_pages, pt_ref, q_ref, kn_ref, vn_ref, ones_ref, *rest):
    del pt_ref
    k_pages = rest[:n_pages]
    v_pages = rest[n_pages:2 * n_pages]
    o_ref = rest[2 * n_pages]
    b = pl.program_id(0)
    page = k_pages[0].shape[1]
    ppb = MOBA_BLOCK // page
    nblk = n_pages // ppb
    scale = 1.0 / math.sqrt(HEAD_DIM)
    q = q_ref[b]
    k_new = kn_ref[b]
    v_new = vn_ref[b]

    def gather(pages, n):
        return jnp.concatenate([pages[n * ppb + i][0] for i in range(ppb)], axis=0)

    gates = [jnp.sum(jnp.mean(gather(k_pages, n), axis=0) * q, axis=-1, keepdims=True) for n in range(nblk)]
    pens = []
    for n in range(nblk):
        cnt = jnp.zeros(gates[n].shape, I32)
        for m in range(nblk):
            if m == n:
                continue
            beats = (gates[m] >= gates[n]) if m < n else (gates[m] > gates[n])
            cnt = cnt + jnp.where(beats, 1, 0)
        pens.append(jnp.where(cnt < MOBA_TOP, 0.0, NEG_BIG))

    m_run = jnp.broadcast_to(jnp.sum(k_new * q, axis=-1, keepdims=True) * scale, q.shape)
    l_run = jnp.ones(q.shape, F32)
    acc = v_new
    for n in range(nblk):
        kb = gather(k_pages, n)
        prod = (kb * q[None]).reshape(MOBA_BLOCK * N_HEADS, HEAD_DIM).astype(BF16)
        s = _dot(prod, ones_ref[...]).reshape(MOBA_BLOCK, N_HEADS, HEAD_DIM)
        s = s * scale + pens[n][None]
        m_new = jnp.maximum(m_run, jnp.max(s, axis=0))
        p = jnp.exp(s - m_new[None])
        alpha = jnp.exp(m_run - m_new)
        l_run = alpha * l_run + jnp.sum(p, axis=0)
        acc = alpha * acc + jnp.sum(p * gather(v_pages, n), axis=0)
        m_run = m_new
    o_ref[b] = acc / l_run


def _attn_sample(page_table, q_s, k_s, v_s, cache_k, cache_v):
    dec_batch, n_pages = page_table.shape
    page = cache_k.shape[1]

    def page_spec(p):
        return pl.BlockSpec((1, page, N_HEADS, HEAD_DIM), lambda b, pt: (pt[b, p], 0, 0, 0))

    whole = pl.BlockSpec((dec_batch, N_HEADS, HEAD_DIM), lambda b, pt: (0, 0, 0))
    grid_spec = pltpu.PrefetchScalarGridSpec(
        num_scalar_prefetch=1,
        grid=(dec_batch,),
        in_specs=[whole] * 3 + [pl.BlockSpec((HEAD_DIM, HEAD_DIM), lambda b, pt: (0, 0))]
                 + [page_spec(p) for p in range(n_pages)] * 2,
        out_specs=whole,
    )
    return pl.pallas_call(
        functools.partial(_attn_sample_kernel, n_pages),
        grid_spec=grid_spec,
        out_shape=jax.ShapeDtypeStruct((dec_batch, N_HEADS, HEAD_DIM), F32),
        compiler_params=_cparams(("arbitrary",)),
        name="attn_sample",
    )(page_table, q_s, k_s, v_s, jnp.ones((HEAD_DIM, HEAD_DIM), BF16),
      *([cache_k] * n_pages), *([cache_v] * n_pages))


def _s5_discretize(a_re, a_im, log_dt, b_re, b_im):
    lam_re = a_re.astype(F32)
    lam_im = a_im.astype(F32)
    dt = jnp.exp(log_dt.astype(F32))[:, None]
    mag = jnp.exp(lam_re * dt)
    ab_re = mag * jnp.cos(lam_im * dt)
    ab_im = mag * jnp.sin(lam_im * dt)
    den = lam_re * lam_re + lam_im * lam_im
    n_re, n_im = ab_re - 1.0, ab_im
    f_re = (n_re * lam_re + n_im * lam_im) / den
    f_im = (n_im * lam_re - n_re * lam_im) / den
    br, bi = b_re.astype(F32), b_im.astype(F32)
    bb_re = f_re[..., None] * br - f_im[..., None] * bi
    bb_im = f_re[..., None] * bi + f_im[..., None] * br
    return lam_re, lam_im, dt, ab_re, ab_im, bb_re, bb_im


def _s5_chunk_operators(lam_re, lam_im, dt, bb_re, bb_im, c_re, c_im):
    L = S5_CHUNK
    hp = lax.Precision.HIGHEST
    steps = jnp.arange(L + 1, dtype=F32)[:, None, None]
    pmag = jnp.exp(lam_re[None] * dt[None] * steps)
    pw_re = pmag * jnp.cos(lam_im[None] * dt[None] * steps)
    pw_im = pmag * jnp.sin(lam_im[None] * dt[None] * steps)
    cr, ci = c_re.astype(F32), c_im.astype(F32)
    cp_re = cr[None] * pw_re[:L, :, None, :] - ci[None] * pw_im[:L, :, None, :]
    cp_im = cr[None] * pw_im[:L, :, None, :] + ci[None] * pw_re[:L, :, None, :]
    kern = (jnp.einsum('jgcp,gpd->jgcd', cp_re, bb_re, precision=hp)
            - jnp.einsum('jgcp,gpd->jgcd', cp_im, bb_im, precision=hp))
    s_idx = jnp.arange(L)[:, None]
    t_idx = jnp.arange(L)[None, :]
    lag = jnp.clip(t_idx - s_idx, 0, L - 1)
    toep = jnp.where((t_idx >= s_idx)[:, :, None, None, None], kern[lag], 0.0)
    toep = toep.transpose(2, 0, 4, 1, 3).reshape(N_GROUPS, L * GROUP_CH, L * GROUP_CH)
    rev_re = pw_re[L - 1 - jnp.arange(L)]
    rev_im = pw_im[L - 1 - jnp.arange(L)]
    w1_re = rev_re[:, :, :, None] * bb_re[None] - rev_im[:, :, :, None] * bb_im[None]
    w1_im = rev_re[:, :, :, None] * bb_im[None] + rev_im[:, :, :, None] * bb_re[None]
    w1_re = w1_re.transpose(1, 0, 3, 2).reshape(N_GROUPS, L * GROUP_CH, STATE_DIM)
    w1_im = w1_im.transpose(1, 0, 3, 2).reshape(N_GROUPS, L * GROUP_CH, STATE_DIM)
    nx_re = pw_re[1:]
    nx_im = pw_im[1:]
    w2_re = cr[None] * nx_re[:, :, None, :] - ci[None] * nx_im[:, :, None, :]
    w2_im = -(cr[None] * nx_im[:, :, None, :] + ci[None] * nx_re[:, :, None, :])
    w2_re = w2_re.transpose(1, 3, 0, 2).reshape(N_GROUPS, STATE_DIM, L * GROUP_CH)
    w2_im = w2_im.transpose(1, 3, 0, 2).reshape(N_GROUPS, STATE_DIM, L * GROUP_CH)
    return (toep.astype(BF16), w1_re.astype(BF16), w1_im.astype(BF16), w2_re.astype(BF16),
            w2_im.astype(BF16), pw_re[L][:, None, :], pw_im[L][:, None, :])


def _s5_prompt_kernel(batch, u_ref, toep_ref, w1r_ref, w1i_ref, w2r_ref, w2i_ref, ar_ref, ai_ref, d_ref,
                      y_ref, hr_ref, hi_ref, dre_ref, dim_ref, pre_ref, pim_ref):
    u = u_ref[0]
    ub = u.astype(BF16)
    dre_ref[...] = _dot(ub, w1r_ref[0])
    dim_ref[...] = _dot(ub, w1i_ref[0])
    a_re = ar_ref[0]
    a_im = ai_ref[0]
    n_chunks = u.shape[0] // batch

    def body(c, carry):
        h_re, h_im = carry
        rows = pl.ds(pl.multiple_of(c * batch, batch), batch)
        pre_ref[rows, :] = h_re
        pim_ref[rows, :] = h_im
        return (a_re * h_re - a_im * h_im + dre_ref[rows, :],
                a_re * h_im + a_im * h_re + dim_ref[rows, :])

    zero = jnp.zeros((batch, STATE_DIM), F32)
    h_re, h_im = lax.fori_loop(0, n_chunks, body, (zero, zero))
    hr_ref[0] = h_re
    hi_ref[0] = h_im
    y_ref[0] = (_dot(ub, toep_ref[0]) + _dot(pre_ref[...].astype(BF16), w2r_ref[0])
                + _dot(pim_ref[...].astype(BF16), w2i_ref[0]) + d_ref[0] * u)


def _s5_prompt(u_t, toep, w1_re, w1_im, w2_re, w2_im, al_re, al_im, d_tiled, batch):
    g, rows, cols = u_t.shape
    blk = lambda s: pl.BlockSpec((1,) + s, lambda i: (i, 0, 0))
    return pl.pallas_call(
        functools.partial(_s5_prompt_kernel, batch),
        grid=(g,),
        in_specs=[blk((rows, cols)), blk((cols, cols)), blk((cols, STATE_DIM)), blk((cols, STATE_DIM)),
                  blk((STATE_DIM, cols)), blk((STATE_DIM, cols)), blk((1, STATE_DIM)), blk((1, STATE_DIM)),
                  blk((1, cols))],
        out_specs=[blk((rows, cols)), blk((batch, STATE_DIM)), blk((batch, STATE_DIM))],
        out_shape=[jax.ShapeDtypeStruct((g, rows, cols), F32),
                   jax.ShapeDtypeStruct((g, batch, STATE_DIM), F32),
                   jax.ShapeDtypeStruct((g, batch, STATE_DIM), F32)],
        scratch_shapes=[pltpu.VMEM((rows, STATE_DIM), F32)] * 4,
        compiler_params=_cparams(("parallel",)),
        name="s5_prompt",
    )(u_t, toep, w1_re, w1_im, w2_re, w2_im, al_re, al_im, d_tiled)


def _s5_sample_kernel(u_ref, h0r_ref, h0i_ref, bdr_ref, bdi_ref, cdr_ref, cdi_ref, ar_ref, ai_ref, d_ref,
                      y_ref, hr_ref, hi_ref):
    u = u_ref[...]
    ub = u.astype(BF16)
    a_re = ar_ref[...]
    a_im = ai_ref[...]
    h0r = h0r_ref[...]
    h0i = h0i_ref[...]
    h_re = a_re * h0r - a_im * h0i + _dot(ub, bdr_ref[...])
    h_im = a_re * h0i + a_im * h0r + _dot(ub, bdi_ref[...])
    hr_ref[...] = h_re
    hi_ref[...] = h_im
    y_ref[...] = (_dot(h_re.astype(BF16), cdr_ref[...]) - _dot(h_im.astype(BF16), cdi_ref[...])
                  + d_ref[...] * u)


def _s5_sample(u_s, h0_re, h0_im, bd_re, bd_im, cd_re, cd_im, ab_re, ab_im, d_skip):
    n = u_s.shape[0]
    width = N_GROUPS * STATE_DIM
    return pl.pallas_call(
        _s5_sample_kernel,
        out_shape=[jax.ShapeDtypeStruct((n, SSM_W), F32), jax.ShapeDtypeStruct((n, width), F32),
                   jax.ShapeDtypeStruct((n, width), F32)],
        compiler_params=pltpu.CompilerParams(vmem_limit_bytes=VMEM_LIMIT),
        name="s5_sample",
    )(u_s, h0_re, h0_im, bd_re, bd_im, cd_re, cd_im, ab_re, ab_im, d_skip)


def _block_diag(m):
    g, r, c = m.shape
    eye = jnp.eye(g, dtype=m.dtype)
    return (m[:, :, None, :] * eye[:, None, :, None]).reshape(g * r, g * c)


def _merge_kernel(x_ref, attn_ref, yssm_ref, ga_ref, gs_ref, wau_ref, wglu_ref, wout_ref, g2_ref,
                  wr_ref, br_ref, su_ref, cnt_in_ref,
                  h_ref, xn_ref, idx_ref, rank_ref, w_ref, cnt_ref, run_ref):
    @pl.when(pl.program_id(0) == 0)
    def _():
        run_ref[...] = cnt_in_ref[...]

    branch_attn = _dot(attn_ref[...].astype(BF16), wau_ref[...])
    glu = _dot(jax.nn.gelu(yssm_ref[...]).astype(BF16), wglu_ref[...])
    branch_ssm = glu[:, :D_MODEL] * jax.nn.sigmoid(glu[:, D_MODEL:])
    merged = jax.nn.sigmoid(ga_ref[...]) * branch_attn + jax.nn.sigmoid(gs_ref[...]) * branch_ssm
    h = x_ref[...] + _dot(merged.astype(BF16), wout_ref[...])
    h_ref[...] = h
    xn = h * lax.rsqrt(jnp.mean(h * h, axis=-1, keepdims=True) + EPS) * g2_ref[...]
    xn_ref[...] = xn

    x_hi, x_lo = _split_hi_lo(xn)
    w_hi, w_lo = _split_hi_lo(wr_ref[...])
    logits = _dot(x_hi, w_hi) + _dot(x_lo, w_hi) + _dot(x_hi, w_lo)
    lt = logits.T[0:N_EXPERTS, :] + br_ref[...]
    rows = lt.shape[1]
    e_iota = lax.broadcasted_iota(I32, lt.shape, 0).astype(F32)
    tops, idxs, hots = [], [], []
    for _ in range(TOP_K):
        m = jnp.max(lt, axis=0, keepdims=True)
        idx = jnp.min(jnp.where(lt == m, e_iota, float(N_EXPERTS)), axis=0, keepdims=True)
        hot = e_iota == idx
        tops.append(m)
        idxs.append(idx)
        hots.append(hot)
        lt = jnp.where(hot, -jnp.inf, lt)
    exps = [jnp.exp(t - tops[0]) for t in tops]
    denom = exps[0] + exps[1] + exps[2] + exps[3]

    member = jnp.zeros(lt.shape, F32)
    for hot in hots:
        member = member + jnp.where(hot, 1.0, 0.0)
    before = run_ref[:, 0:1] + _dot(member.astype(BF16), su_ref[...])
    ranks = [jnp.sum(jnp.where(hot, before, 0.0), axis=0, keepdims=True) for hot in hots]
    run_ref[...] = run_ref[...] + jnp.sum(member, axis=1, keepdims=True)
    cnt_ref[...] = run_ref[...]

    zrows = jnp.zeros((SUBLANES - TOP_K, rows), F32)
    idx_ref[...] = jnp.concatenate(idxs + [zrows], axis=0).astype(I32)
    rank_ref[...] = jnp.concatenate(ranks + [zrows], axis=0).astype(I32)
    w_t = jnp.concatenate([e / denom for e in exps] + [jnp.zeros((LANES - TOP_K, rows), F32)], axis=0)
    w_ref[...] = w_t.T


def _merge(x, attn, y_ssm, g_attn, g_ssm, w_attn_up, w_ssm_glu, w_out, norm2_g, w_router_pad,
           b_router_col, cnt_in, tile):
    t = x.shape[0]
    row = lambda w: pl.BlockSpec((tile, w), lambda i: (i, 0))
    col = pl.BlockSpec((SUBLANES, tile), lambda i: (0, i))
    const = lambda shape: pl.BlockSpec(shape, lambda i: (0, 0))
    strict_upper = jnp.triu(jnp.ones((tile, tile), F32), 1).astype(BF16)
    return pl.pallas_call(
        _merge_kernel,
        grid=(t // tile,),
        in_specs=[row(D_MODEL), row(ATTN_W), row(SSM_W), row(D_MODEL), row(D_MODEL),
                  const((ATTN_W, D_MODEL)), const((SSM_W, 2 * D_MODEL)), const((D_MODEL, D_MODEL)),
                  const((1, D_MODEL)), const((D_MODEL, LANES)), const((N_EXPERTS, 1)),
                  const((tile, tile)), const((N_EXPERTS, LANES))],
        out_specs=[row(D_MODEL), row(D_MODEL), col, col, row(LANES), const((N_EXPERTS, LANES))],
        out_shape=[jax.ShapeDtypeStruct((t, D_MODEL), F32), jax.ShapeDtypeStruct((t, D_MODEL), F32),
                   jax.ShapeDtypeStruct((SUBLANES, t), I32), jax.ShapeDtypeStruct((SUBLANES, t), I32),
                   jax.ShapeDtypeStruct((t, LANES), F32), jax.ShapeDtypeStruct((N_EXPERTS, LANES), F32)],
        scratch_shapes=[pltpu.VMEM((N_EXPERTS, LANES), F32)],
        compiler_params=_cparams(("arbitrary",)),
        name="merge_router",
    )(x, attn, y_ssm, g_attn, g_ssm, w_attn_up, w_ssm_glu, w_out, norm2_g, w_router_pad, b_router_col,
      strict_upper, cnt_in)


def _dispatch_kernel(tok, pos_ref, x_hbm, xs_in, xs_out, sem):
    del xs_in
    i = pl.program_id(0)
    n = pl.num_programs(0)
    slot = i % 2

    def issue(r, c):
        for k in range(TOP_K):
            pltpu.make_async_copy(x_hbm.at[pl.ds(i * tok + r, 1)], xs_out.at[pl.ds(pos_ref[k, r], 1)],
                                  sem.at[slot]).start()
        return c

    lax.fori_loop(0, tok, issue, 0, unroll=8)

    def drain(s):
        for _ in range(TOP_K):
            pltpu.make_async_copy(x_hbm.at[pl.ds(0, tok)], xs_out.at[pl.ds(0, tok)], sem.at[s]).wait()

    @pl.when(i > 0)
    def _():
        drain(1 - slot)

    @pl.when(i == n - 1)
    def _():
        drain(slot)


def _dispatch(pos, xn, xs, tok):
    t = xn.shape[0]
    return pl.pallas_call(
        functools.partial(_dispatch_kernel, tok),
        grid=(t // tok,),
        in_specs=[pl.BlockSpec((SUBLANES, tok), lambda i: (0, i), memory_space=pltpu.SMEM),
                  pl.BlockSpec(memory_space=pl.ANY), pl.BlockSpec(memory_space=pl.ANY)],
        out_specs=pl.BlockSpec(memory_space=pl.ANY),
        out_shape=jax.ShapeDtypeStruct(xs.shape, xs.dtype),
        scratch_shapes=[pltpu.SemaphoreType.DMA((2,))],
        input_output_aliases={2: 0},
        compiler_params=_cparams(("arbitrary",)),
        name="moe_dispatch",
    )(pos, xn, xs)


def _moe_group_kernel(te_ref, nu_ref, x_ref, wg_ref, bg_ref, wu_ref, bu_ref, wd_ref, bd_ref, y_ref,
                      wgb_ref, wub_ref, wdb_ref):
    i = pl.program_id(0)

    @pl.when(i < nu_ref[0])
    def _():
        @pl.when((i == 0) | (te_ref[i] != te_ref[jnp.maximum(i - 1, 0)]))
        def _():
            wgb_ref[...] = wg_ref[0].astype(BF16)
            wub_ref[...] = wu_ref[0].astype(BF16)
            wdb_ref[...] = wd_ref[0].astype(BF16)

        x = x_ref[...].astype(BF16)
        d_ff = wgb_ref.shape[1]
        for c in range(d_ff // FF_CHUNK):
            cols = slice(c * FF_CHUNK, (c + 1) * FF_CHUNK)
            gate = _dot(x, wgb_ref[:, cols]) + bg_ref[0, :, cols]
            up = _dot(x, wub_ref[:, cols]) + bu_ref[0, :, cols]
            gate = jnp.minimum(gate, SWIGLU_LIMIT)
            up = jnp.clip(up, -SWIGLU_LIMIT, SWIGLU_LIMIT)
            hid = (up + 1.0) * (gate * jax.nn.sigmoid(SWIGLU_ALPHA * gate))
            part = _dot(hid.astype(BF16), wdb_ref[cols, :])
            if c == 0:
                y_ref[...] = part + bd_ref[0]
            else:
                y_ref[...] += part

    @pl.when(i >= nu_ref[0])
    def _():
        y_ref[...] = jnp.zeros(y_ref.shape, F32)


def _moe_grouped(tile_expert, n_used, xs, w_gate, b_gate, w_up, b_up, w_down, b_down):
    n_exp, d_model, d_ff = w_gate.shape
    n_tiles = xs.shape[0] // MOE_TM
    wspec = lambda r, c: pl.BlockSpec((1, r, c), lambda i, te, nu: (te[i], 0, 0))
    grid_spec = pltpu.PrefetchScalarGridSpec(
        num_scalar_prefetch=2,
        grid=(n_tiles,),
        in_specs=[pl.BlockSpec((MOE_TM, d_model), lambda i, te, nu: (jnp.minimum(i, nu[0] - 1), 0)),
                  wspec(d_model, d_ff), wspec(1, d_ff), wspec(d_model, d_ff), wspec(1, d_ff),
                  wspec(d_ff, d_model), wspec(1, d_model)],
        out_specs=pl.BlockSpec((MOE_TM, d_model), lambda i, te, nu: (i, 0)),
        scratch_shapes=[pltpu.VMEM((d_model, d_ff), BF16), pltpu.VMEM((d_model, d_ff), BF16),
                        pltpu.VMEM((d_ff, d_model), BF16)],
    )
    return pl.pallas_call(
        _moe_group_kernel,
        grid_spec=grid_spec,
        out_shape=jax.ShapeDtypeStruct((xs.shape[0], d_model), F32),
        compiler_params=_cparams(("arbitrary",)),
        name="moe_grouped",
    )(tile_expert, n_used, xs, w_gate, b_gate.reshape(n_exp, 1, d_ff), w_up, b_up.reshape(n_exp, 1, d_ff),
      w_down, b_down.reshape(n_exp, 1, d_model))


def _combine_kernel(tok, pos_ref, posn_ref, h_ref, w_ref, ys_hbm, o_ref, buf, sem):
    i = pl.program_id(0)
    n = pl.num_programs(0)
    slot = i % 2

    def gather(p_ref, s):
        def issue(r, c):
            for k in range(TOP_K):
                pltpu.make_async_copy(ys_hbm.at[pl.ds(p_ref[k, r], 1)], buf.at[s, k, pl.ds(r, 1)],
                                      sem.at[s]).start()
            return c
        lax.fori_loop(0, tok, issue, 0, unroll=8)

    @pl.when(i == 0)
    def _():
        gather(pos_ref, 0)

    @pl.when(i + 1 < n)
    def _():
        gather(posn_ref, 1 - slot)

    for k in range(TOP_K):
        pltpu.make_async_copy(ys_hbm.at[pl.ds(0, tok)], buf.at[slot, k], sem.at[slot]).wait()
    w = w_ref[...]
    out = h_ref[...]
    for k in range(TOP_K):
        out = out + w[:, k:k + 1] * buf[slot, k]
    o_ref[...] = out


def _combine(pos, h, w_rows, ys, tok):
    t, d_model = h.shape
    n = t // tok
    smem = lambda imap: pl.BlockSpec((SUBLANES, tok), imap, memory_space=pltpu.SMEM)
    row = lambda w: pl.BlockSpec((tok, w), lambda i: (i, 0))
    return pl.pallas_call(
        functools.partial(_combine_kernel, tok),
        grid=(n,),
        in_specs=[smem(lambda i: (0, i)), smem(lambda i: (0, jnp.minimum(i + 1, n - 1))),
                  row(d_model), row(LANES), pl.BlockSpec(memory_space=pl.ANY)],
        out_specs=row(d_model),
        out_shape=jax.ShapeDtypeStruct((t, d_model), F32),
        scratch_shapes=[pltpu.VMEM((2, TOP_K, tok, d_model), F32), pltpu.SemaphoreType.DMA((2,))],
        compiler_params=_cparams(("arbitrary",)),
        name="moe_combine",
    )(pos, pos, h, w_rows, ys)


def _routing_tables(counts, n_tiles):
    padded = (counts + MOE_TM - 1) // MOE_TM * MOE_TM
    ends = jnp.cumsum(padded)
    starts = ends - padded
    tile_row = jnp.arange(n_tiles, dtype=I32) * MOE_TM
    tile_expert = jnp.minimum(jnp.sum((tile_row[:, None] >= ends[None, :]).astype(I32), axis=1), N_EXPERTS - 1)
    n_used = jnp.maximum(ends[-1:] // MOE_TM, 1)
    return starts.astype(I32), tile_expert.astype(I32), n_used.astype(I32)


def _rope_tables(positions):
    half = HEAD_DIM // 2
    inv_freq = ROPE_THETA ** (-jnp.arange(half, dtype=F32) / half)
    ang = positions.astype(F32)[:, None] * inv_freq[None, :]
    cos = jnp.cos(ang)
    sin = jnp.sin(ang)
    cos_h = jnp.concatenate([cos, cos], axis=1)
    sin_h = jnp.concatenate([-sin, sin], axis=1)
    return jnp.tile(cos_h, (1, N_HEADS)), jnp.tile(sin_h, (1, N_HEADS))


def kernel(x_prompt, x_sample, cache_k, cache_v, state_ssm_re, state_ssm_im, page_table, norm1_g, w_in,
           q_norm_g, k_norm_g, ssm_a_re, ssm_a_im, ssm_log_dt, ssm_b_re, ssm_b_im, ssm_c_re, ssm_c_im,
           ssm_d, w_attn_up, w_ssm_glu, w_out, norm2_g, w_router, b_router, w_gate, b_gate, w_up, b_up,
           w_down, b_down):
    batch, seq, d_model = x_prompt.shape
    dec_batch = x_sample.shape[0]
    n_pages = page_table.shape[1]
    page = cache_k.shape[2]
    past_len = n_pages * page
    assert x_sample.shape[1] == 1 and w_in.shape[0] == 1
    assert seq % ROW_TILE == 0 and seq % MOBA_BLOCK == 0 and past_len % MOBA_BLOCK == 0
    t_prompt = batch * seq
    heads = (N_HEADS, HEAD_DIM)

    xp = x_prompt.reshape(t_prompt, d_model)
    xs = x_sample.reshape(dec_batch, d_model)
    hd, hdt = _head_indicator()
    w_in_b = w_in[0].astype(BF16)
    q_gain = jnp.tile(q_norm_g, (1, N_HEADS))
    k_gain = jnp.tile(k_norm_g, (1, N_HEADS))
    cos_p, sin_p = _rope_tables(jnp.arange(seq, dtype=I32))
    cos_s, sin_s = _rope_tables(jnp.full((dec_batch,), past_len, I32))
    q_p, k_p, v_p, u_p, ga_p, gs_p = _inproj(xp, norm1_g, w_in_b, cos_p, sin_p, q_gain, k_gain, hd, hdt,
                                             ROW_TILE)
    q_s, k_s, v_s, u_s, ga_s, gs_s = _inproj(xs, norm1_g, w_in_b, cos_s, sin_s, q_gain, k_gain, hd, hdt,
                                             dec_batch)

    attn_p = _attn_prompt(q_p, k_p, v_p, batch, seq)
    per_head = (dec_batch,) + heads
    attn_s = _attn_sample(page_table, q_s.reshape(per_head), k_s.reshape(per_head), v_s.reshape(per_head),
                          cache_k[0], cache_v[0]).reshape(dec_batch, ATTN_W)

    lam_re, lam_im, dt, ab_re, ab_im, bb_re, bb_im = _s5_discretize(
        ssm_a_re[0], ssm_a_im[0], ssm_log_dt[0], ssm_b_re[0], ssm_b_im[0])
    toep, w1_re, w1_im, w2_re, w2_im, al_re, al_im = _s5_chunk_operators(
        lam_re, lam_im, dt, bb_re, bb_im, ssm_c_re[0], ssm_c_im[0])
    L = S5_CHUNK
    n_chunks = seq // L
    u_t = (u_p.reshape(batch, n_chunks, L, N_GROUPS, GROUP_CH)
           .transpose(3, 1, 0, 2, 4).reshape(N_GROUPS, n_chunks * batch, L * GROUP_CH))
    d_tiled = jnp.tile(ssm_d[0].reshape(N_GROUPS, 1, GROUP_CH), (1, L, 1)).reshape(N_GROUPS, 1, L * GROUP_CH)
    y_t, hp_re, hp_im = _s5_prompt(u_t, toep, w1_re, w1_im, w2_re, w2_im, al_re, al_im, d_tiled, batch)
    y_p = (y_t.reshape(N_GROUPS, n_chunks, batch, L, GROUP_CH)
           .transpose(2, 1, 3, 0, 4).reshape(t_prompt, SSM_W))
    width = N_GROUPS * STATE_DIM
    y_s, hs_re, hs_im = _s5_sample(
        u_s, state_ssm_re[0].reshape(dec_batch, width), state_ssm_im[0].reshape(dec_batch, width),
        _block_diag(bb_re.transpose(0, 2, 1)).astype(BF16), _block_diag(bb_im.transpose(0, 2, 1)).astype(BF16),
        _block_diag(ssm_c_re[0].transpose(0, 2, 1)).astype(BF16),
        _block_diag(ssm_c_im[0].transpose(0, 2, 1)).astype(BF16),
        ab_re.reshape(1, width), ab_im.reshape(1, width), ssm_d)

    w_router_pad = jnp.concatenate([w_router[0], jnp.zeros((d_model, LANES - N_EXPERTS), F32)], axis=1)
    merge_w = (w_attn_up[0].astype(BF16), w_ssm_glu[0].astype(BF16), w_out[0].astype(BF16), norm2_g,
               w_router_pad, b_router[0].reshape(N_EXPERTS, 1))
    h_p, xn_p, idx_p, rank_p, wr_p, cnt_p = _merge(xp, attn_p, y_p, ga_p, gs_p, *merge_w,
                                                   jnp.zeros((N_EXPERTS, LANES), F32), ROW_TILE)
    h_s, xn_s, idx_s, rank_s, wr_s, cnt_all = _merge(xs, attn_s, y_s, ga_s, gs_s, *merge_w, cnt_p, dec_batch)

    n_pairs = (t_prompt + dec_batch) * TOP_K
    n_tiles = n_pairs // MOE_TM + N_EXPERTS
    starts, tile_expert, n_used = _routing_tables(cnt_all[:, 0].astype(I32), n_tiles)
    pos_p = rank_p + jnp.take(starts, idx_p)
    pos_s = rank_s + jnp.take(starts, idx_s)
    x_sorted = jnp.zeros((n_tiles * MOE_TM, d_model), F32)
    x_sorted = _dispatch(pos_p, xn_p, x_sorted, MOE_TOK)
    x_sorted = _dispatch(pos_s, xn_s, x_sorted, dec_batch)
    y_sorted = _moe_grouped(tile_expert, n_used, x_sorted, w_gate[0], b_gate[0], w_up[0], b_up[0],
                            w_down[0], b_down[0])
    out_p = _combine(pos_p, h_p, wr_p, y_sorted, MOE_TOK)
    out_s = _combine(pos_s, h_s, wr_s, y_sorted, dec_batch)

    return (out_p.reshape(batch, seq, d_model),
            out_s.reshape(dec_batch, 1, d_model),
            k_p.reshape((1, batch, seq) + heads),
            v_p.reshape((1, batch, seq) + heads),
            hp_re.transpose(1, 0, 2)[None],
            hp_im.transpose(1, 0, 2)[None],
            k_s.reshape((1, dec_batch, 1) + heads),
            v_s.reshape((1, dec_batch, 1) + heads),
            hs_re.reshape(1, dec_batch, N_GROUPS, STATE_DIM),
            hs_im.reshape(1, dec_batch, N_GROUPS, STATE_DIM))
```

```python
import functools
import math

import jax
import jax.numpy as jnp
from jax import lax
from jax.experimental import pallas as pl
from jax.experimental.pallas import tpu as pltpu

F32 = jnp.float32
BF16 = jnp.bfloat16
I32 = jnp.int32

D_MODEL = 1024
N_HEADS = 8
HEAD_DIM = 64
ATTN_W = N_HEADS * HEAD_DIM
MOBA_BLOCK = 256
MOBA_TOP = 3
ROPE_THETA = 10000.0
SSM_W = 512
GROUP_CH = 16
N_GROUPS = 32
STATE_DIM = 64
N_EXPERTS = 32
TOP_K = 4
SWIGLU_ALPHA = 1.702
SWIGLU_LIMIT = 7.0
EPS = 1e-6
IN_W = 4096
NEG_BIG = -1e9

LANES = 128
SUBLANES = 8
ROW_TILE = 512
S5_CHUNK = 32
MOE_TM = 256
MOE_TOK = 256
FF_CHUNK = 512
VMEM_LIMIT = 56 * 1024 * 1024


def _cparams(sem):
    return pltpu.CompilerParams(dimension_semantics=sem, vmem_limit_bytes=VMEM_LIMIT)


def _dot(a, b):
    return jnp.dot(a, b, preferred_element_type=F32)


def _split_hi_lo(a):
    hi = a.astype(BF16)
    lo = (a - hi.astype(F32)).astype(BF16)
    return hi, lo


def _dot_hilo(a, b_bf16):
    hi, lo = _split_hi_lo(a)
    return _dot(hi, b_bf16) + _dot(lo, b_bf16)


def _head_indicator():
    c = jnp.arange(ATTN_W)[:, None] // HEAD_DIM
    h = jnp.arange(LANES)[None, :]
    hd = (c == h).astype(BF16)
    return hd, hd.T


def _head_norm_rope(t, gain, cos, sin_signed, hd, hdt):
    ssq = _dot_hilo(t * t, hd)
    rstd = lax.rsqrt(ssq * (1.0 / HEAD_DIM) + EPS)
    t = t * _dot_hilo(rstd, hdt) * gain
    lane = lax.broadcasted_iota(I32, (t.shape[0], LANES), 1)
    first_half = (lane % HEAD_DIM) < (HEAD_DIM // 2)
    outs = []
    for s in range(ATTN_W // LANES):
        ts = t[:, s * LANES:(s + 1) * LANES]
        partner = jnp.where(first_half,
                            pltpu.roll(ts, LANES - HEAD_DIM // 2, axis=1),
                            pltpu.roll(ts, HEAD_DIM // 2, axis=1))
        outs.append(ts * cos[:, s * LANES:(s + 1) * LANES]
                    + partner * sin_signed[:, s * LANES:(s + 1) * LANES])
    return jnp.concatenate(outs, axis=1)


def _inproj_kernel(x_ref, g1_ref, w_ref, cos_ref, sin_ref, qg_ref, kg_ref, hd_ref, hdt_ref,
                   q_ref, k_ref, v_ref, u_ref, ga_ref, gs_ref):
    x = x_ref[...]
    xn = x * lax.rsqrt(jnp.mean(x * x, axis=-1, keepdims=True) + EPS) * g1_ref[...]
    xb = xn.astype(BF16)
    hd = hd_ref[...]
    hdt = hdt_ref[...]
    cos = cos_ref[...]
    sin = sin_ref[...]
    q = _dot(xb, w_ref[:, 0:ATTN_W])
    q_ref[...] = _head_norm_rope(q, qg_ref[...], cos, sin, hd, hdt)
    k = _dot(xb, w_ref[:, ATTN_W:2 * ATTN_W])
    k_ref[...] = _head_norm_rope(k, kg_ref[...], cos, sin, hd, hdt)
    v_ref[...] = _dot(xb, w_ref[:, 2 * ATTN_W:3 * ATTN_W])
    u_ref[...] = _dot(xb, w_ref[:, 3 * ATTN_W:3 * ATTN_W + SSM_W])
    ga_ref[...] = _dot(xb, w_ref[:, 2048:3072])
    gs_ref[...] = _dot(xb, w_ref[:, 3072:4096])


def _inproj(x, norm1_g, w_in_bf16, cos_tab, sin_tab, q_gain, k_gain, hd, hdt, tile):
    t = x.shape[0]
    period = cos_tab.shape[0] // tile
    row = lambda w: pl.BlockSpec((tile, w), lambda i: (i, 0))
    const = lambda shape: pl.BlockSpec(shape, lambda i: (0, 0))
    pos = pl.BlockSpec((tile, ATTN_W), lambda i: (i % period, 0))
    out_shape = ([jax.ShapeDtypeStruct((t, ATTN_W), F32)] * 4
                 + [jax.ShapeDtypeStruct((t, D_MODEL), F32)] * 2)
    return pl.pallas_call(
        _inproj_kernel,
        grid=(t // tile,),
        in_specs=[row(D_MODEL), const((1, D_MODEL)), const((D_MODEL, IN_W)), pos, pos,
                  const((1, ATTN_W)), const((1, ATTN_W)), const((ATTN_W, LANES)), const((LANES, ATTN_W))],
        out_specs=[row(ATTN_W)] * 4 + [row(D_MODEL)] * 2,
        out_shape=out_shape,
        compiler_params=_cparams(("parallel",)),
        name="inproj",
    )(x, norm1_g, w_in_bf16, cos_tab, sin_tab, q_gain, k_gain, hd, hdt)


def _block_penalty_t(gate_t, n_past, own_blk):
    blk = lax.broadcasted_iota(I32, gate_t.shape, 0)
    valid = blk < n_past
    g = jnp.where(valid, gate_t, -jnp.inf)
    cnt = jnp.zeros(gate_t.shape, I32)
    for m in range(gate_t.shape[0]):
        gm = g[m:m + 1, :]
        beats = jnp.where(gm > g, 1, jnp.where((gm == g) & (blk > m), 1, 0))
        cnt = cnt + beats
    keep = (valid & (cnt < MOBA_TOP)) | (blk == own_blk)
    return jnp.where(keep, 0.0, NEG_BIG)


def _attn_prompt_kernel(q_ref, k_ref, v_ref, o_ref, kaug_ref, vt_ref, kmean_ref):
    j = pl.program_id(2)
    nblk = kaug_ref.shape[0]
    blk = MOBA_BLOCK

    @pl.when(j == 0)
    def _():
        row_blk = lax.broadcasted_iota(I32, (blk, LANES), 1)
        for n in range(nblk):
            k = k_ref[n * blk:(n + 1) * blk, :]
            kaug_ref[n, :, 0:LANES] = k.astype(BF16)
            kaug_ref[n, :, LANES:2 * LANES] = jnp.where(row_blk == n, 1.0, 0.0).astype(BF16)
            vt_ref[n] = v_ref[n * blk:(n + 1) * blk, :].T.astype(BF16)
            kmean_ref[n:n + 1, :] = jnp.mean(k, axis=0, keepdims=True)

    q_t = q_ref[...].T
    feat = lax.broadcasted_iota(I32, q_t.shape, 0)
    km_hi, km_lo = _split_hi_lo(kmean_ref[...])
    zpad = jnp.zeros((LANES - nblk, blk), F32)
    qaug = []
    for h in range(2):
        qh = jnp.where(feat >= HEAD_DIM if h == 1 else feat < HEAD_DIM, q_t, 0.0)
        q_hi, q_lo = _split_hi_lo(qh)
        gate_t = _dot(km_hi, q_hi) + _dot(km_hi, q_lo) + _dot(km_lo, q_hi)
        pen_t = jnp.concatenate([_block_penalty_t(gate_t, j, j), zpad], axis=0)
        qaug.append(jnp.concatenate([(qh * (1.0 / math.sqrt(HEAD_DIM))).astype(BF16),
                                     pen_t.astype(BF16)], axis=0))

    key_i = lax.broadcasted_iota(I32, (blk, blk), 0)
    qry_i = lax.broadcasted_iota(I32, (blk, blk), 1)
    future = key_i > qry_i

    def block(n, carry, diagonal):
        kb = kaug_ref[n]
        vt = vt_ref[n]
        out = []
        for h in range(2):
            m_prev, l_prev, acc = carry[h]
            s = _dot(kb, qaug[h])
            if diagonal:
                s = jnp.where(future, NEG_BIG, s)
            m_new = jnp.maximum(m_prev, jnp.max(s, axis=0, keepdims=True))
            p = jnp.exp(s - m_new)
            alpha = jnp.exp(m_prev - m_new)
            l_new = alpha * l_prev + jnp.sum(p, axis=0, keepdims=True)
            pv = _dot(vt[h * HEAD_DIM:(h + 1) * HEAD_DIM, :], p.astype(BF16))
            out.append((m_new, l_new, alpha * acc + pv))
        return tuple(out)

    init = tuple((jnp.full((1, blk), -jnp.inf, F32), jnp.zeros((1, blk), F32),
                  jnp.zeros((HEAD_DIM, blk), F32)) for _ in range(2))
    carry = lax.fori_loop(0, j, lambda n, c: block(n, c, False), init)
    carry = block(j, carry, True)
    out_t = jnp.concatenate([carry[0][2] / carry[0][1], carry[1][2] / carry[1][1]], axis=0)
    o_ref[...] = out_t.T


def _attn_prompt(q, k, v, batch, seq):
    nq = seq // MOBA_BLOCK
    return pl.pallas_call(
        _attn_prompt_kernel,
        grid=(batch, ATTN_W // LANES, nq),
        in_specs=[pl.BlockSpec((MOBA_BLOCK, LANES), lambda b, hp, j: (b * nq + j, hp)),
                  pl.BlockSpec((seq, LANES), lambda b, hp, j: (b, hp)),
                  pl.BlockSpec((seq, LANES), lambda b, hp, j: (b, hp))],
        out_specs=pl.BlockSpec((MOBA_BLOCK, LANES), lambda b, hp, j: (b * nq + j, hp)),
        out_shape=jax.ShapeDtypeStruct((batch * seq, ATTN_W), F32),
        scratch_shapes=[pltpu.VMEM((nq, MOBA_BLOCK, 2 * LANES), BF16),
                        pltpu.VMEM((nq, LANES, MOBA_BLOCK), BF16),
                        pltpu.VMEM((nq, LANES), F32)],
        compiler_params=_cparams(("parallel", "parallel", "arbitrary")),
        name="attn_prompt",
    )(q, k, v)


def _attn_sample_kernel(n_pages, pt_ref, qb_ref, kn_ref, vn_ref, *rest):
    del pt_ref
    k_pages = rest[:n_pages]
    v_pages = rest[n_pages:2 * n_pages]
    o_ref = rest[2 * n_pages]
    b = pl.program_id(0)
    page = k_pages[0].shape[2]
    ppb = MOBA_BLOCK // page
    nblk = n_pages // ppb
    qb = qb_ref[0] * (1.0 / math.sqrt(HEAD_DIM))

    def head_sums(x):
        return jnp.sum(x.reshape(N_HEADS, HEAD_DIM, x.shape[1]), axis=1)

    def per_feature(p):
        return jnp.broadcast_to(p[:, None, :], (N_HEADS, HEAD_DIM, p.shape[1])).reshape(ATTN_W, p.shape[1])

    s_pages = [head_sums(k_pages[p][0] * qb) for p in range(n_pages)]
    lane = lax.broadcasted_iota(I32, s_pages[0].shape, 1)
    s_new = jnp.where(lane == 0, head_sums(kn_ref[0] * qb), NEG_BIG)

    gates = []
    for n in range(nblk):
        blk_sum = s_pages[n * ppb]
        for i in range(1, ppb):
            blk_sum = blk_sum + s_pages[n * ppb + i]
        gates.append(jnp.sum(blk_sum, axis=1, keepdims=True))
    pens = []
    for n in range(nblk):
        cnt = jnp.zeros(gates[n].shape, I32)
        for m in range(nblk):
            if m == n:
                continue
            beats = (gates[m] >= gates[n]) if m < n else (gates[m] > gates[n])
            cnt = cnt + jnp.where(beats, 1, 0)
        pens.append(jnp.where(cnt < MOBA_TOP, 0.0, NEG_BIG))

    s_adj = [s_pages[p] + pens[p // ppb] for p in range(n_pages)]
    m = s_new
    for s in s_adj:
        m = jnp.maximum(m, s)
    m = jnp.max(m, axis=1, keepdims=True)
    e_new = jnp.exp(s_new - m)
    es = [jnp.exp(s - m) for s in s_adj]
    total = e_new
    for e in es:
        total = total + e
    inv = 1.0 / jnp.sum(total, axis=1, keepdims=True)
    acc = vn_ref[0] * per_feature(e_new * inv)
    for p in range(n_pages):
        acc = acc + v_pages[p][0] * per_feature(es[p] * inv)
    a_hi, a_lo = _split_hi_lo(acc)
    ones = jnp.ones((SUBLANES, page), BF16)
    nt = (((1,), (1,)), ((), ()))
    row = (lax.dot_general(ones, a_hi, nt, preferred_element_type=F32)
           + lax.dot_general(ones, a_lo, nt, preferred_element_type=F32))
    o_ref[pl.ds(b, 1), :] = row[0:1, :]


def _attn_sample(page_table, q_rep, k_new, v_new, cache_kt, cache_vt):
    dec_batch, n_pages = page_table.shape
    page = cache_kt.shape[2]

    def page_spec(p):
        return pl.BlockSpec((1, ATTN_W, page), lambda b, pt: (pt[b, p], 0, 0))

    own = pl.BlockSpec((1, ATTN_W, page), lambda b, pt: (b, 0, 0))
    grid_spec = pltpu.PrefetchScalarGridSpec(
        num_scalar_prefetch=1,
        grid=(dec_batch,),
        in_specs=[own] * 3 + [page_spec(p) for p in range(n_pages)] * 2,
        out_specs=pl.BlockSpec((dec_batch, ATTN_W), lambda b, pt: (0, 0)),
    )
    return pl.pallas_call(
        functools.partial(_attn_sample_kernel, n_pages),
        grid_spec=grid_spec,
        out_shape=jax.ShapeDtypeStruct((dec_batch, ATTN_W), F32),
        compiler_params=_cparams(("arbitrary",)),
        name="attn_sample",
    )(page_table, q_rep, k_new, v_new, *([cache_kt] * n_pages), *([cache_vt] * n_pages))


def _s5_discretize(a_re, a_im, log_dt, b_re, b_im):
    lam_re = a_re.astype(F32)
    lam_im = a_im.astype(F32)
    dt = jnp.exp(log_dt.astype(F32))[:, None]
    mag = jnp.exp(lam_re * dt)
    ab_re = mag * jnp.cos(lam_im * dt)
    ab_im = mag * jnp.sin(lam_im * dt)
    den = lam_re * lam_re + lam_im * lam_im
    n_re, n_im = ab_re - 1.0, ab_im
    f_re = (n_re * lam_re + n_im * lam_im) / den
    f_im = (n_im * lam_re - n_re * lam_im) / den
    br, bi = b_re.astype(F32), b_im.astype(F32)
    bb_re = f_re[..., None] * br - f_im[..., None] * bi
    bb_im = f_re[..., None] * bi + f_im[..., None] * br
    return lam_re, lam_im, dt, ab_re, ab_im, bb_re, bb_im


def _s5_chunk_operators(lam_re, lam_im, dt, bb_re, bb_im, c_re, c_im):
    L = S5_CHUNK
    hp = lax.Precision.HIGHEST
    steps = jnp.arange(L + 1, dtype=F32)[:, None, None]
    pmag = jnp.exp(lam_re[None] * dt[None] * steps)
    pw_re = pmag * jnp.cos(lam_im[None] * dt[None] * steps)
    pw_im = pmag * jnp.sin(lam_im[None] * dt[None] * steps)
    cr, ci = c_re.astype(F32), c_im.astype(F32)
    cp_re = cr[None] * pw_re[:L, :, None, :] - ci[None] * pw_im[:L, :, None, :]
    cp_im = cr[None] * pw_im[:L, :, None, :] + ci[None] * pw_re[:L, :, None, :]
    kern = (jnp.einsum('jgcp,gpd->jgcd', cp_re, bb_re, precision=hp)
            - jnp.einsum('jgcp,gpd->jgcd', cp_im, bb_im, precision=hp))
    s_idx = jnp.arange(L)[:, None]
    t_idx = jnp.arange(L)[None, :]
    lag = jnp.clip(t_idx - s_idx, 0, L - 1)
    toep = jnp.where((t_idx >= s_idx)[:, :, None, None, None], kern[lag], 0.0)
    toep = toep.transpose(2, 0, 4, 1, 3).reshape(N_GROUPS, L * GROUP_CH, L * GROUP_CH)
    rev_re = pw_re[L - 1 - jnp.arange(L)]
    rev_im = pw_im[L - 1 - jnp.arange(L)]
    w1_re = rev_re[:, :, :, None] * bb_re[None] - rev_im[:, :, :, None] * bb_im[None]
    w1_im = rev_re[:, :, :, None] * bb_im[None] + rev_im[:, :, :, None] * bb_re[None]
    w1_re = w1_re.transpose(1, 0, 3, 2).reshape(N_GROUPS, L * GROUP_CH, STATE_DIM)
    w1_im = w1_im.transpose(1, 0, 3, 2).reshape(N_GROUPS, L * GROUP_CH, STATE_DIM)
    nx_re = pw_re[1:]
    nx_im = pw_im[1:]
    w2_re = cr[None] * nx_re[:, :, None, :] - ci[None] * nx_im[:, :, None, :]
    w2_im = -(cr[None] * nx_im[:, :, None, :] + ci[None] * nx_re[:, :, None, :])
    w2_re = w2_re.transpose(1, 3, 0, 2).reshape(N_GROUPS, STATE_DIM, L * GROUP_CH)
    w2_im = w2_im.transpose(1, 3, 0, 2).reshape(N_GROUPS, STATE_DIM, L * GROUP_CH)
    return (toep.astype(BF16), w1_re.astype(BF16), w1_im.astype(BF16), w2_re.astype(BF16),
            w2_im.astype(BF16), pw_re[L][:, None, :], pw_im[L][:, None, :])


def _s5_prompt_kernel(batch, u_ref, toep_ref, w1r_ref, w1i_ref, w2r_ref, w2i_ref, ar_ref, ai_ref, d_ref,
                      y_ref, hr_ref, hi_ref, dre_ref, dim_ref, pre_ref, pim_ref):
    u = u_ref[0]
    ub = u.astype(BF16)
    dre_ref[...] = _dot(ub, w1r_ref[0])
    dim_ref[...] = _dot(ub, w1i_ref[0])
    a_re = ar_ref[0]
    a_im = ai_ref[0]
    n_chunks = u.shape[0] // batch

    def body(c, carry):
        h_re, h_im = carry
        rows = pl.ds(pl.multiple_of(c * batch, batch), batch)
        pre_ref[rows, :] = h_re
        pim_ref[rows, :] = h_im
        return (a_re * h_re - a_im * h_im + dre_ref[rows, :],
                a_re * h_im + a_im * h_re + dim_ref[rows, :])

    zero = jnp.zeros((batch, STATE_DIM), F32)
    h_re, h_im = lax.fori_loop(0, n_chunks, body, (zero, zero))
    hr_ref[0] = h_re
    hi_ref[0] = h_im
    y_ref[0] = (_dot(ub, toep_ref[0]) + _dot(pre_ref[...].astype(BF16), w2r_ref[0])
                + _dot(pim_ref[...].astype(BF16), w2i_ref[0]) + d_ref[0] * u)


def _s5_prompt(u_t, toep, w1_re, w1_im, w2_re, w2_im, al_re, al_im, d_tiled, batch):
    g, rows, cols = u_t.shape
    blk = lambda s: pl.BlockSpec((1,) + s, lambda i: (i, 0, 0))
    return pl.pallas_call(
        functools.partial(_s5_prompt_kernel, batch),
        grid=(g,),
        in_specs=[blk((rows, cols)), blk((cols, cols)), blk((cols, STATE_DIM)), blk((cols, STATE_DIM)),
                  blk((STATE_DIM, cols)), blk((STATE_DIM, cols)), blk((1, STATE_DIM)), blk((1, STATE_DIM)),
                  blk((1, cols))],
        out_specs=[blk((rows, cols)), blk((batch, STATE_DIM)), blk((batch, STATE_DIM))],
        out_shape=[jax.ShapeDtypeStruct((g, rows, cols), F32),
                   jax.ShapeDtypeStruct((g, batch, STATE_DIM), F32),
                   jax.ShapeDtypeStruct((g, batch, STATE_DIM), F32)],
        scratch_shapes=[pltpu.VMEM((rows, STATE_DIM), F32)] * 4,
        compiler_params=_cparams(("parallel",)),
        name="s5_prompt",
    )(u_t, toep, w1_re, w1_im, w2_re, w2_im, al_re, al_im, d_tiled)


def _s5_sample_kernel(u_ref, h0r_ref, h0i_ref, bdr_ref, bdi_ref, cdr_ref, cdi_ref, ar_ref, ai_ref, d_ref,
                      y_ref, hr_ref, hi_ref):
    u = u_ref[...]
    ub = u.astype(BF16)
    a_re = ar_ref[...]
    a_im = ai_ref[...]
    h0r = h0r_ref[...]
    h0i = h0i_ref[...]
    h_re = a_re * h0r - a_im * h0i + _dot(ub, bdr_ref[...])
    h_im = a_re * h0i + a_im * h0r + _dot(ub, bdi_ref[...])
    hr_ref[...] = h_re
    hi_ref[...] = h_im
    y_ref[...] = (_dot(h_re.astype(BF16), cdr_ref[...]) - _dot(h_im.astype(BF16), cdi_ref[...])
                  + d_ref[...] * u)


def _s5_sample(u_s, h0_re, h0_im, bd_re, bd_im, cd_re, cd_im, ab_re, ab_im, d_skip):
    n = u_s.shape[0]
    width = N_GROUPS * STATE_DIM
    return pl.pallas_call(
        _s5_sample_kernel,
        out_shape=[jax.ShapeDtypeStruct((n, SSM_W), F32), jax.ShapeDtypeStruct((n, width), F32),
                   jax.ShapeDtypeStruct((n, width), F32)],
        compiler_params=pltpu.CompilerParams(vmem_limit_bytes=VMEM_LIMIT),
        name="s5_sample",
    )(u_s, h0_re, h0_im, bd_re, bd_im, cd_re, cd_im, ab_re, ab_im, d_skip)


def _block_diag(m):
    g, r, c = m.shape
    eye = jnp.eye(g, dtype=m.dtype)
    return (m[:, :, None, :] * eye[:, None, :, None]).reshape(g * r, g * c)


def _merge_kernel(x_ref, attn_ref, yssm_ref, ga_ref, gs_ref, wau_ref, wglu_ref, wout_ref, g2_ref,
                  wr_ref, br_ref, su_ref, cnt_in_ref,
                  h_ref, xn_ref, idx_ref, rank_ref, w_ref, cnt_ref, run_ref):
    @pl.when(pl.program_id(0) == 0)
    def _():
        run_ref[...] = cnt_in_ref[...]

    branch_attn = _dot(attn_ref[...].astype(BF16), wau_ref[...])
    glu = _dot(jax.nn.gelu(yssm_ref[...]).astype(BF16), wglu_ref[...])
    branch_ssm = glu[:, :D_MODEL] * jax.nn.sigmoid(glu[:, D_MODEL:])
    merged = jax.nn.sigmoid(ga_ref[...]) * branch_attn + jax.nn.sigmoid(gs_ref[...]) * branch_ssm
    h = x_ref[...] + _dot(merged.astype(BF16), wout_ref[...])
    h_ref[...] = h
    xn = h * lax.rsqrt(jnp.mean(h * h, axis=-1, keepdims=True) + EPS) * g2_ref[...]
    xn_ref[...] = xn

    x_hi, x_lo = _split_hi_lo(xn)
    w_hi, w_lo = _split_hi_lo(wr_ref[...])
    logits = _dot(x_hi, w_hi) + _dot(x_lo, w_hi) + _dot(x_hi, w_lo)
    lt = logits.T[0:N_EXPERTS, :] + br_ref[...]
    rows = lt.shape[1]
    e_iota = lax.broadcasted_iota(I32, lt.shape, 0).astype(F32)
    tops, idxs, hots = [], [], []
    for _ in range(TOP_K):
        m = jnp.max(lt, axis=0, keepdims=True)
        idx = jnp.min(jnp.where(lt == m, e_iota, float(N_EXPERTS)), axis=0, keepdims=True)
        hot = e_iota == idx
        tops.append(m)
        idxs.append(idx)
        hots.append(hot)
        lt = jnp.where(hot, -jnp.inf, lt)
    exps = [jnp.exp(t - tops[0]) for t in tops]
    denom = exps[0] + exps[1] + exps[2] + exps[3]

    member = jnp.zeros(lt.shape, F32)
    for hot in hots:
        member = member + jnp.where(hot, 1.0, 0.0)
    before = run_ref[:, 0:1] + _dot(member.astype(BF16), su_ref[...])
    ranks = [jnp.sum(jnp.where(hot, before, 0.0), axis=0, keepdims=True) for hot in hots]
    run_ref[...] = run_ref[...] + jnp.sum(member, axis=1, keepdims=True)
    cnt_ref[...] = run_ref[...]

    zrows = jnp.zeros((SUBLANES - TOP_K, rows), F32)
    idx_ref[...] = jnp.concatenate(idxs + [zrows], axis=0).astype(I32)
    rank_ref[...] = jnp.concatenate(ranks + [zrows], axis=0).astype(I32)
    w_t = jnp.concatenate([e / denom for e in exps] + [jnp.zeros((LANES - TOP_K, rows), F32)], axis=0)
    w_ref[...] = w_t.T


def _merge(x, attn, y_ssm, g_attn, g_ssm, w_attn_up, w_ssm_glu, w_out, norm2_g, w_router_pad,
           b_router_col, cnt_in, tile):
    t = x.shape[0]
    row = lambda w: pl.BlockSpec((tile, w), lambda i: (i, 0))
    col = pl.BlockSpec((SUBLANES, tile), lambda i: (0, i))
    const = lambda shape: pl.BlockSpec(shape, lambda i: (0, 0))
    strict_upper = jnp.triu(jnp.ones((tile, tile), F32), 1).astype(BF16)
    return pl.pallas_call(
        _merge_kernel,
        grid=(t // tile,),
        in_specs=[row(D_MODEL), row(ATTN_W), row(SSM_W), row(D_MODEL), row(D_MODEL),
                  const((ATTN_W, D_MODEL)), const((SSM_W, 2 * D_MODEL)), const((D_MODEL, D_MODEL)),
                  const((1, D_MODEL)), const((D_MODEL, LANES)), const((N_EXPERTS, 1)),
                  const((tile, tile)), const((N_EXPERTS, LANES))],
        out_specs=[row(D_MODEL), row(D_MODEL), col, col, row(LANES), const((N_EXPERTS, LANES))],
        out_shape=[jax.ShapeDtypeStruct((t, D_MODEL), F32), jax.ShapeDtypeStruct((t, D_MODEL), F32),
                   jax.ShapeDtypeStruct((SUBLANES, t), I32), jax.ShapeDtypeStruct((SUBLANES, t), I32),
                   jax.ShapeDtypeStruct((t, LANES), F32), jax.ShapeDtypeStruct((N_EXPERTS, LANES), F32)],
        scratch_shapes=[pltpu.VMEM((N_EXPERTS, LANES), F32)],
        compiler_params=_cparams(("arbitrary",)),
        name="merge_router",
    )(x, attn, y_ssm, g_attn, g_ssm, w_attn_up, w_ssm_glu, w_out, norm2_g, w_router_pad, b_router_col,
      strict_upper, cnt_in)


def _dispatch_kernel(tok, pos_ref, x_ref, xs_in, xs_out, sem):
    del xs_in

    def issue(r, c):
        for k in range(TOP_K):
            pltpu.make_async_copy(x_ref.at[pl.ds(r, 1)], xs_out.at[pl.ds(pos_ref[k, r], 1)], sem).start()
        return c

    lax.fori_loop(0, tok, issue, 0, unroll=8)
    for _ in range(TOP_K):
        pltpu.make_async_copy(x_ref, xs_out.at[pl.ds(0, tok)], sem).wait()


def _dispatch(pos, xn, xs, tok):
    t, d_model = xn.shape
    return pl.pallas_call(
        functools.partial(_dispatch_kernel, tok),
        grid=(t // tok,),
        in_specs=[pl.BlockSpec((SUBLANES, tok), lambda i: (0, i), memory_space=pltpu.SMEM),
                  pl.BlockSpec((tok, d_model), lambda i: (i, 0)), pl.BlockSpec(memory_space=pl.ANY)],
        out_specs=pl.BlockSpec(memory_space=pl.ANY),
        out_shape=jax.ShapeDtypeStruct(xs.shape, xs.dtype),
        scratch_shapes=[pltpu.SemaphoreType.DMA(())],
        input_output_aliases={2: 0},
        compiler_params=_cparams(("arbitrary",)),
        name="moe_dispatch",
    )(pos, xn, xs)


def _moe_group_kernel(te_ref, nu_ref, x_ref, wg_ref, bg_ref, wu_ref, bu_ref, wd_ref, bd_ref, y_ref,
                      wgb_ref, wub_ref, wdb_ref):
    i = pl.program_id(0)

    @pl.when(i < nu_ref[0])
    def _():
        @pl.when((i == 0) | (te_ref[i] != te_ref[jnp.maximum(i - 1, 0)]))
        def _():
            wgb_ref[...] = wg_ref[0].astype(BF16)
            wub_ref[...] = wu_ref[0].astype(BF16)
            wdb_ref[...] = wd_ref[0].astype(BF16)

        x = x_ref[...].astype(BF16)
        d_ff = wgb_ref.shape[1]
        for c in range(d_ff // FF_CHUNK):
            cols = slice(c * FF_CHUNK, (c + 1) * FF_CHUNK)
            gate = _dot(x, wgb_ref[:, cols]) + bg_ref[0, :, cols]
            up = _dot(x, wub_ref[:, cols]) + bu_ref[0, :, cols]
            gate = jnp.minimum(gate, SWIGLU_LIMIT)
            up = jnp.clip(up, -SWIGLU_LIMIT, SWIGLU_LIMIT)
            hid = (up + 1.0) * (gate * jax.nn.sigmoid(SWIGLU_ALPHA * gate))
            part = _dot(hid.astype(BF16), wdb_ref[cols, :])
            if c == 0:
                y_ref[...] = part + bd_ref[0]
            else:
                y_ref[...] += part

    @pl.when(i >= nu_ref[0])
    def _():
        y_ref[...] = jnp.zeros(y_ref.shape, F32)


def _moe_grouped(tile_expert, n_used, xs, w_gate, b_gate, w_up, b_up, w_down, b_down):
    n_exp, d_model, d_ff = w_gate.shape
    n_tiles = xs.shape[0] // MOE_TM
    wspec = lambda r, c: pl.BlockSpec((1, r, c), lambda i, te, nu: (te[i], 0, 0))
    grid_spec = pltpu.PrefetchScalarGridSpec(
        num_scalar_prefetch=2,
        grid=(n_tiles,),
        in_specs=[pl.BlockSpec((MOE_TM, d_model), lambda i, te, nu: (jnp.minimum(i, nu[0] - 1), 0)),
                  wspec(d_model, d_ff), wspec(1, d_ff), wspec(d_model, d_ff), wspec(1, d_ff),
                  wspec(d_ff, d_model), wspec(1, d_model)],
        out_specs=pl.BlockSpec((MOE_TM, d_model), lambda i, te, nu: (i, 0)),
        scratch_shapes=[pltpu.VMEM((d_model, d_ff), BF16), pltpu.VMEM((d_model, d_ff), BF16),
                        pltpu.VMEM((d_ff, d_model), BF16)],
    )
    return pl.pallas_call(
        _moe_group_kernel,
        grid_spec=grid_spec,
        out_shape=jax.ShapeDtypeStruct((xs.shape[0], d_model), F32),
        compiler_params=_cparams(("arbitrary",)),
        name="moe_grouped",
    )(tile_expert, n_used, xs, w_gate, b_gate.reshape(n_exp, 1, d_ff), w_up, b_up.reshape(n_exp, 1, d_ff),
      w_down, b_down.reshape(n_exp, 1, d_model))


def _combine_kernel(tok, pos_ref, posn_ref, h_ref, w_ref, ys_hbm, o_ref, buf, sem):
    i = pl.program_id(0)
    n = pl.num_programs(0)
    slot = i % 2

    def gather(p_ref, s):
        def issue(r, c):
            for k in range(TOP_K):
                pltpu.make_async_copy(ys_hbm.at[pl.ds(p_ref[k, r], 1)], buf.at[s, k, pl.ds(r, 1)],
                                      sem.at[s]).start()
            return c
        lax.fori_loop(0, tok, issue, 0, unroll=8)

    @pl.when(i == 0)
    def _():
        gather(pos_ref, 0)

    @pl.when(i + 1 < n)
    def _():
        gather(posn_ref, 1 - slot)

    for k in range(TOP_K):
        pltpu.make_async_copy(ys_hbm.at[pl.ds(0, tok)], buf.at[slot, k], sem.at[slot]).wait()
    w = w_ref[...]
    out = h_ref[...]
    for k in range(TOP_K):
        out = out + w[:, k:k + 1] * buf[slot, k]
    o_ref[...] = out


def _combine(pos, h, w_rows, ys, tok):
    t, d_model = h.shape
    n = t // tok
    smem = lambda imap: pl.BlockSpec((SUBLANES, tok), imap, memory_space=pltpu.SMEM)
    row = lambda w: pl.BlockSpec((tok, w), lambda i: (i, 0))
    return pl.pallas_call(
        functools.partial(_combine_kernel, tok),
        grid=(n,),
        in_specs=[smem(lambda i: (0, i)), smem(lambda i: (0, jnp.minimum(i + 1, n - 1))),
                  row(d_model), row(LANES), pl.BlockSpec(memory_space=pl.ANY)],
        out_specs=row(d_model),
        out_shape=jax.ShapeDtypeStruct((t, d_model), F32),
        scratch_shapes=[pltpu.VMEM((2, TOP_K, tok, d_model), F32), pltpu.SemaphoreType.DMA((2,))],
        compiler_params=_cparams(("arbitrary",)),
        name="moe_combine",
    )(pos, pos, h, w_rows, ys)


def _routing_tables(counts, n_tiles):
    padded = (counts + MOE_TM - 1) // MOE_TM * MOE_TM
    ends = jnp.cumsum(padded)
    starts = ends - padded
    tile_row = jnp.arange(n_tiles, dtype=I32) * MOE_TM
    tile_expert = jnp.minimum(jnp.sum((tile_row[:, None] >= ends[None, :]).astype(I32), axis=1), N_EXPERTS - 1)
    n_used = jnp.maximum(ends[-1:] // MOE_TM, 1)
    return starts.astype(I32), tile_expert.astype(I32), n_used.astype(I32)


def _rope_tables(positions):
    half = HEAD_DIM // 2
    inv_freq = ROPE_THETA ** (-jnp.arange(half, dtype=F32) / half)
    ang = positions.astype(F32)[:, None] * inv_freq[None, :]
    cos = jnp.cos(ang)
    sin = jnp.sin(ang)
    cos_h = jnp.concatenate([cos, cos], axis=1)
    sin_h = jnp.concatenate([-sin, sin], axis=1)
    return jnp.tile(cos_h, (1, N_HEADS)), jnp.tile(sin_h, (1, N_HEADS))


def kernel(x_prompt, x_sample, cache_k, cache_v, state_ssm_re, state_ssm_im, page_table, norm1_g, w_in,
           q_norm_g, k_norm_g, ssm_a_re, ssm_a_im, ssm_log_dt, ssm_b_re, ssm_b_im, ssm_c_re, ssm_c_im,
           ssm_d, w_attn_up, w_ssm_glu, w_out, norm2_g, w_router, b_router, w_gate, b_gate, w_up, b_up,
           w_down, b_down):
    batch, seq, d_model = x_prompt.shape
    dec_batch = x_sample.shape[0]
    n_pages = page_table.shape[1]
    page = cache_k.shape[2]
    past_len = n_pages * page
    assert x_sample.shape[1] == 1 and w_in.shape[0] == 1
    assert seq % ROW_TILE == 0 and seq % MOBA_BLOCK == 0 and past_len % MOBA_BLOCK == 0
    assert page == LANES, "the sample attention keeps one page per lane tile"
    t_prompt = batch * seq
    heads = (N_HEADS, HEAD_DIM)

    xp = x_prompt.reshape(t_prompt, d_model)
    xs = x_sample.reshape(dec_batch, d_model)
    hd, hdt = _head_indicator()
    w_in_b = w_in[0].astype(BF16)
    q_gain = jnp.tile(q_norm_g, (1, N_HEADS))
    k_gain = jnp.tile(k_norm_g, (1, N_HEADS))
    cos_p, sin_p = _rope_tables(jnp.arange(seq, dtype=I32))
    cos_s, sin_s = _rope_tables(jnp.full((dec_batch,), past_len, I32))
    q_p, k_p, v_p, u_p, ga_p, gs_p = _inproj(xp, norm1_g, w_in_b, cos_p, sin_p, q_gain, k_gain, hd, hdt,
                                             ROW_TILE)
    q_s, k_s, v_s, u_s, ga_s, gs_s = _inproj(xs, norm1_g, w_in_b, cos_s, sin_s, q_gain, k_gain, hd, hdt,
                                             dec_batch)

    attn_p = _attn_prompt(q_p, k_p, v_p, batch, seq)
    feature_major = lambda c: c[0].transpose(0, 2, 3, 1).reshape(c.shape[1], ATTN_W, page)
    lane0 = lambda a: jnp.pad(a[:, :, None], ((0, 0), (0, 0), (0, page - 1)))
    attn_s = _attn_sample(page_table, jnp.broadcast_to(q_s[:, :, None], (dec_batch, ATTN_W, page)),
                          lane0(k_s), lane0(v_s), feature_major(cache_k), feature_major(cache_v))

    lam_re, lam_im, dt, ab_re, ab_im, bb_re, bb_im = _s5_discretize(
        ssm_a_re[0], ssm_a_im[0], ssm_log_dt[0], ssm_b_re[0], ssm_b_im[0])
    toep, w1_re, w1_im, w2_re, w2_im, al_re, al_im = _s5_chunk_operators(
        lam_re, lam_im, dt, bb_re, bb_im, ssm_c_re[0], ssm_c_im[0])
    L = S5_CHUNK
    n_chunks = seq // L
    u_t = (u_p.reshape(batch, n_chunks, L, N_GROUPS, GROUP_CH)
           .transpose(3, 1, 0, 2, 4).reshape(N_GROUPS, n_chunks * batch, L * GROUP_CH))
    d_tiled = jnp.tile(ssm_d[0].reshape(N_GROUPS, 1, GROUP_CH), (1, L, 1)).reshape(N_GROUPS, 1, L * GROUP_CH)
    y_t, hp_re, hp_im = _s5_prompt(u_t, toep, w1_re, w1_im, w2_re, w2_im, al_re, al_im, d_tiled, batch)
    y_p = (y_t.reshape(N_GROUPS, n_chunks, batch, L, GROUP_CH)
           .transpose(2, 1, 3, 0, 4).reshape(t_prompt, SSM_W))
    width = N_GROUPS * STATE_DIM
    y_s, hs_re, hs_im = _s5_sample(
        u_s, state_ssm_re[0].reshape(dec_batch, width), state_ssm_im[0].reshape(dec_batch, width),
        _block_diag(bb_re.transpose(0, 2, 1)).astype(BF16), _block_diag(bb_im.transpose(0, 2, 1)).astype(BF16),
        _block_diag(ssm_c_re[0].transpose(0, 2, 1)).astype(BF16),
        _block_diag(ssm_c_im[0].transpose(0, 2, 1)).astype(BF16),
        ab_re.reshape(1, width), ab_im.reshape(1, width), ssm_d)

    w_router_pad = jnp.concatenate([w_router[0], jnp.zeros((d_model, LANES - N_EXPERTS), F32)], axis=1)
    merge_w = (w_attn_up[0].astype(BF16), w_ssm_glu[0].astype(BF16), w_out[0].astype(BF16), norm2_g,
               w_router_pad, b_router[0].reshape(N_EXPERTS, 1))
    h_p, xn_p, idx_p, rank_p, wr_p, cnt_p = _merge(xp, attn_p, y_p, ga_p, gs_p, *merge_w,
                                                   jnp.zeros((N_EXPERTS, LANES), F32), ROW_TILE)
    h_s, xn_s, idx_s, rank_s, wr_s, cnt_all = _merge(xs, attn_s, y_s, ga_s, gs_s, *merge_w, cnt_p, dec_batch)

    n_pairs = (t_prompt + dec_batch) * TOP_K
    n_tiles = n_pairs // MOE_TM + N_EXPERTS
    starts, tile_expert, n_used = _routing_tables(cnt_all[:, 0].astype(I32), n_tiles)
    pos_p, pos_s = rank_p, rank_s
    for e in range(N_EXPERTS):
        pos_p = pos_p + jnp.where(idx_p == e, starts[e], 0)
        pos_s = pos_s + jnp.where(idx_s == e, starts[e], 0)
    x_sorted = jnp.zeros((n_tiles * MOE_TM, d_model), F32)
    x_sorted = _dispatch(pos_p, xn_p, x_sorted, MOE_TOK)
    x_sorted = _dispatch(pos_s, xn_s, x_sorted, dec_batch)
    y_sorted = _moe_grouped(tile_expert, n_used, x_sorted, w_gate[0], b_gate[0], w_up[0], b_up[0],
                            w_down[0], b_down[0])
    out_p = _combine(pos_p, h_p, wr_p, y_sorted, MOE_TOK)
    out_s = _combine(pos_s, h_s, wr_s, y_sorted, dec_batch)

    return (out_p.reshape(batch, seq, d_model),
            out_s.reshape(dec_batch, 1, d_model),
            k_p.reshape((1, batch, seq) + heads),
            v_p.reshape((1, batch, seq) + heads),
            hp_re.transpose(1, 0, 2)[None],
            hp_im.transpose(1, 0, 2)[None],
            k_s.reshape((1, dec_batch, 1) + heads),
            v_s.reshape((1, dec_batch, 1) + heads),
            hs_re.reshape(1, dec_batch, N_GROUPS, STATE_DIM),
            hs_im.reshape(1, dec_batch, N_GROUPS, STATE_DIM))
```

```python
import functools
import math

import jax
import jax.numpy as jnp
from jax import lax
from jax.experimental import pallas as pl
from jax.experimental.pallas import tpu as pltpu

F32 = jnp.float32
BF16 = jnp.bfloat16
I32 = jnp.int32

D_MODEL = 1024
N_HEADS = 8
HEAD_DIM = 64
ATTN_W = N_HEADS * HEAD_DIM
MOBA_BLOCK = 256
MOBA_TOP = 3
ROPE_THETA = 10000.0
SSM_W = 512
GROUP_CH = 16
N_GROUPS = 32
STATE_DIM = 64
N_EXPERTS = 32
TOP_K = 4
SWIGLU_ALPHA = 1.702
SWIGLU_LIMIT = 7.0
EPS = 1e-6
IN_W = 4096
NEG_BIG = -1e9

LANES = 128
SUBLANES = 8
ROW_TILE = 512
S5_CHUNK = 32
MOE_TM = 256
MOE_TOK = 256
FF_CHUNK = 512
VMEM_LIMIT = 56 * 1024 * 1024


def _cparams(sem):
    return pltpu.CompilerParams(dimension_semantics=sem, vmem_limit_bytes=VMEM_LIMIT)


def _dot(a, b):
    return jnp.dot(a, b, preferred_element_type=F32)


def _split_hi_lo(a):
    hi = a.astype(BF16)
    lo = (a - hi.astype(F32)).astype(BF16)
    return hi, lo


def _dot_hilo(a, b_bf16):
    hi, lo = _split_hi_lo(a)
    return _dot(hi, b_bf16) + _dot(lo, b_bf16)


def _head_indicator():
    c = jnp.arange(ATTN_W)[:, None] // HEAD_DIM
    h = jnp.arange(LANES)[None, :]
    hd = (c == h).astype(BF16)
    return hd, hd.T


def _head_norm_rope(t, gain, cos, sin_signed, hd, hdt):
    ssq = _dot_hilo(t * t, hd)
    rstd = lax.rsqrt(ssq * (1.0 / HEAD_DIM) + EPS)
    t = t * _dot_hilo(rstd, hdt) * gain
    lane = lax.broadcasted_iota(I32, (t.shape[0], LANES), 1)
    first_half = (lane % HEAD_DIM) < (HEAD_DIM // 2)
    outs = []
    for s in range(ATTN_W // LANES):
        ts = t[:, s * LANES:(s + 1) * LANES]
        partner = jnp.where(first_half,
                            pltpu.roll(ts, LANES - HEAD_DIM // 2, axis=1),
                            pltpu.roll(ts, HEAD_DIM // 2, axis=1))
        outs.append(ts * cos[:, s * LANES:(s + 1) * LANES]
                    + partner * sin_signed[:, s * LANES:(s + 1) * LANES])
    return jnp.concatenate(outs, axis=1)


def _inproj_kernel(x_ref, g1_ref, w_ref, cos_ref, sin_ref, qg_ref, kg_ref, hd_ref, hdt_ref,
                   q_ref, k_ref, v_ref, u_ref, ga_ref, gs_ref, kt_ref, vt_ref):
    x = x_ref[...]
    xn = x * lax.rsqrt(jnp.mean(x * x, axis=-1, keepdims=True) + EPS) * g1_ref[...]
    xb = xn.astype(BF16)
    hd = hd_ref[...]
    hdt = hdt_ref[...]
    cos = cos_ref[...]
    sin = sin_ref[...]
    q = _dot(xb, w_ref[:, 0:ATTN_W])
    q_ref[...] = _head_norm_rope(q, qg_ref[...], cos, sin, hd, hdt)
    k = _head_norm_rope(_dot(xb, w_ref[:, ATTN_W:2 * ATTN_W]), kg_ref[...], cos, sin, hd, hdt)
    k_ref[...] = k
    kt_ref[0] = k.T
    v = _dot(xb, w_ref[:, 2 * ATTN_W:3 * ATTN_W])
    v_ref[...] = v
    vt_ref[0] = v.T
    u_ref[...] = _dot(xb, w_ref[:, 3 * ATTN_W:3 * ATTN_W + SSM_W])
    ga_ref[...] = _dot(xb, w_ref[:, 2048:3072])
    gs_ref[...] = _dot(xb, w_ref[:, 3072:4096])


def _inproj(x, norm1_g, w_in_bf16, cos_tab, sin_tab, q_gain, k_gain, hd, hdt, tile):
    t = x.shape[0]
    period = cos_tab.shape[0] // tile
    row = lambda w: pl.BlockSpec((tile, w), lambda i: (i, 0))
    const = lambda shape: pl.BlockSpec(shape, lambda i: (0, 0))
    pos = pl.BlockSpec((tile, ATTN_W), lambda i: (i % period, 0))
    feature_major = pl.BlockSpec((1, ATTN_W, tile), lambda i: (i // period, 0, i % period))
    out_shape = ([jax.ShapeDtypeStruct((t, ATTN_W), F32)] * 4
                 + [jax.ShapeDtypeStruct((t, D_MODEL), F32)] * 2
                 + [jax.ShapeDtypeStruct((t // (period * tile), ATTN_W, period * tile), F32)] * 2)
    return pl.pallas_call(
        _inproj_kernel,
        grid=(t // tile,),
        in_specs=[row(D_MODEL), const((1, D_MODEL)), const((D_MODEL, IN_W)), pos, pos,
                  const((1, ATTN_W)), const((1, ATTN_W)), const((ATTN_W, LANES)), const((LANES, ATTN_W))],
        out_specs=[row(ATTN_W)] * 4 + [row(D_MODEL)] * 2 + [feature_major] * 2,
        out_shape=out_shape,
        compiler_params=_cparams(("parallel",)),
        name="inproj",
    )(x, norm1_g, w_in_bf16, cos_tab, sin_tab, q_gain, k_gain, hd, hdt)


def _block_penalty_t(gate_t, n_past, own_blk):
    blk = lax.broadcasted_iota(I32, gate_t.shape, 0)
    valid = blk < n_past
    g = jnp.where(valid, gate_t, -jnp.inf)
    cnt = jnp.zeros(gate_t.shape, I32)
    for m in range(gate_t.shape[0]):
        gm = g[m:m + 1, :]
        beats = jnp.where(gm > g, 1, jnp.where((gm == g) & (blk > m), 1, 0))
        cnt = cnt + beats
    keep = (valid & (cnt < MOBA_TOP)) | (blk == own_blk)
    return jnp.where(keep, 0.0, NEG_BIG)


def _attn_prompt_kernel(q_ref, k_ref, vt_ref, o_ref, kaug_ref, vtb_ref, kmean_ref):
    j = pl.program_id(2)
    blk = MOBA_BLOCK
    nblk = k_ref.shape[0] // blk

    @pl.when(j == 0)
    def _():
        lane = lax.broadcasted_iota(I32, (blk, LANES), 1)
        for n in range(nblk):
            k = k_ref[n * blk:(n + 1) * blk, :]
            kaug_ref[n * blk:(n + 1) * blk, 0:LANES] = k.astype(BF16)
            kaug_ref[n * blk:(n + 1) * blk, LANES:2 * LANES] = jnp.where(lane == n, 1.0, 0.0).astype(BF16)
            kmean_ref[n:n + 1, :] = jnp.mean(k, axis=0, keepdims=True)
        vtb_ref[...] = vt_ref[0].astype(BF16)

    q_t = q_ref[...].T
    feat = lax.broadcasted_iota(I32, q_t.shape, 0)
    km_hi, km_lo = _split_hi_lo(kmean_ref[...])
    zpad = jnp.zeros((LANES - nblk, blk), F32)
    qaug = []
    for h in range(2):
        qh = jnp.where(feat >= HEAD_DIM if h == 1 else feat < HEAD_DIM, q_t, 0.0)
        q_hi, q_lo = _split_hi_lo(qh)
        gate_t = _dot(km_hi, q_hi) + _dot(km_hi, q_lo) + _dot(km_lo, q_hi)
        pen_t = jnp.concatenate([_block_penalty_t(gate_t, j, j), zpad], axis=0)
        qaug.append(jnp.concatenate([(qh * (1.0 / math.sqrt(HEAD_DIM))).astype(BF16),
                                     pen_t.astype(BF16)], axis=0))

    key_i = lax.broadcasted_iota(I32, (blk, blk), 0)
    qry_i = lax.broadcasted_iota(I32, (blk, blk), 1)
    future = key_i > qry_i

    def attend(jq):
        past = jq * blk
        outs = []
        for h in range(2):
            s_diag = jnp.where(future, NEG_BIG, _dot(kaug_ref[past:past + blk, :], qaug[h]))
            m = jnp.max(s_diag, axis=0, keepdims=True)
            if jq > 0:
                s_past = _dot(kaug_ref[0:past, :], qaug[h])
                m = jnp.maximum(m, jnp.max(s_past, axis=0, keepdims=True))
                p_past = jnp.exp(s_past - m)
            p_diag = jnp.exp(s_diag - m)
            l = jnp.sum(p_diag, axis=0, keepdims=True)
            feats = slice(h * HEAD_DIM, (h + 1) * HEAD_DIM)
            pv = _dot(vtb_ref[feats, past:past + blk], p_diag.astype(BF16))
            if jq > 0:
                l = l + jnp.sum(p_past, axis=0, keepdims=True)
                pv = pv + _dot(vtb_ref[feats, 0:past], p_past.astype(BF16))
            outs.append(pv / l)
        o_ref[...] = jnp.concatenate(outs, axis=0).T

    for jq in range(nblk):
        pl.when(j == jq)(functools.partial(attend, jq))


def _attn_prompt(q, k, vt, batch, seq):
    nq = seq // MOBA_BLOCK
    return pl.pallas_call(
        _attn_prompt_kernel,
        grid=(batch, ATTN_W // LANES, nq),
        in_specs=[pl.BlockSpec((MOBA_BLOCK, LANES), lambda b, hp, j: (b * nq + j, hp)),
                  pl.BlockSpec((seq, LANES), lambda b, hp, j: (b, hp)),
                  pl.BlockSpec((1, LANES, seq), lambda b, hp, j: (b, hp, 0))],
        out_specs=pl.BlockSpec((MOBA_BLOCK, LANES), lambda b, hp, j: (b * nq + j, hp)),
        out_shape=jax.ShapeDtypeStruct((batch * seq, ATTN_W), F32),
        scratch_shapes=[pltpu.VMEM((seq, 2 * LANES), BF16), pltpu.VMEM((LANES, seq), BF16),
                        pltpu.VMEM((nq, LANES), F32)],
        compiler_params=_cparams(("parallel", "parallel", "arbitrary")),
        name="attn_prompt",
    )(q, k, vt)


def _attn_sample_kernel(n_pages, pt_ref, qb_ref, kn_ref, vn_ref, *rest):
    del pt_ref
    k_pages = rest[:n_pages]
    v_pages = rest[n_pages:2 * n_pages]
    o_ref = rest[2 * n_pages]
    b = pl.program_id(0)
    page = k_pages[0].shape[2]
    ppb = MOBA_BLOCK // page
    nblk = n_pages // ppb
    qb = qb_ref[0] * (1.0 / math.sqrt(HEAD_DIM))

    def head_sums(x):
        return jnp.sum(x.reshape(N_HEADS, HEAD_DIM, x.shape[1]), axis=1)

    def per_feature(p):
        return jnp.broadcast_to(p[:, None, :], (N_HEADS, HEAD_DIM, p.shape[1])).reshape(ATTN_W, p.shape[1])

    s_pages = [head_sums(k_pages[p][0] * qb) for p in range(n_pages)]
    lane = lax.broadcasted_iota(I32, s_pages[0].shape, 1)
    s_new = jnp.where(lane == 0, head_sums(kn_ref[0] * qb), NEG_BIG)

    gates = []
    for n in range(nblk):
        blk_sum = s_pages[n * ppb]
        for i in range(1, ppb):
            blk_sum = blk_sum + s_pages[n * ppb + i]
        gates.append(jnp.sum(blk_sum, axis=1, keepdims=True))
    pens = []
    for n in range(nblk):
        cnt = jnp.zeros(gates[n].shape, I32)
        for m in range(nblk):
            if m == n:
                continue
            beats = (gates[m] >= gates[n]) if m < n else (gates[m] > gates[n])
            cnt = cnt + jnp.where(beats, 1, 0)
        pens.append(jnp.where(cnt < MOBA_TOP, 0.0, NEG_BIG))

    s_adj = [s_pages[p] + pens[p // ppb] for p in range(n_pages)]
    m = s_new
    for s in s_adj:
        m = jnp.maximum(m, s)
    m = jnp.max(m, axis=1, keepdims=True)
    e_new = jnp.exp(s_new - m)
    es = [jnp.exp(s - m) for s in s_adj]
    total = e_new
    for e in es:
        total = total + e
    inv = 1.0 / jnp.sum(total, axis=1, keepdims=True)
    acc = vn_ref[0] * per_feature(e_new * inv)
    for p in range(n_pages):
        acc = acc + v_pages[p][0] * per_feature(es[p] * inv)
    a_hi, a_lo = _split_hi_lo(acc)
    ones = jnp.ones((SUBLANES, page), BF16)
    nt = (((1,), (1,)), ((), ()))
    row = (lax.dot_general(ones, a_hi, nt, preferred_element_type=F32)
           + lax.dot_general(ones, a_lo, nt, preferred_element_type=F32))
    o_ref[pl.ds(b, 1), :] = row[0:1, :]


def _attn_sample(page_table, q_rep, k_new, v_new, cache_kt, cache_vt):
    dec_batch, n_pages = page_table.shape
    page = cache_kt.shape[2]

    def page_spec(p):
        return pl.BlockSpec((1, ATTN_W, page), lambda b, pt: (pt[b, p], 0, 0))

    own = pl.BlockSpec((1, ATTN_W, page), lambda b, pt: (b, 0, 0))
    grid_spec = pltpu.PrefetchScalarGridSpec(
        num_scalar_prefetch=1,
        grid=(dec_batch,),
        in_specs=[own] * 3 + [page_spec(p) for p in range(n_pages)] * 2,
        out_specs=pl.BlockSpec((dec_batch, ATTN_W), lambda b, pt: (0, 0)),
    )
    return pl.pallas_call(
        functools.partial(_attn_sample_kernel, n_pages),
        grid_spec=grid_spec,
        out_shape=jax.ShapeDtypeStruct((dec_batch, ATTN_W), F32),
        compiler_params=_cparams(("arbitrary",)),
        name="attn_sample",
    )(page_table, q_rep, k_new, v_new, *([cache_kt] * n_pages), *([cache_vt] * n_pages))


def _s5_discretize(a_re, a_im, log_dt, b_re, b_im):
    lam_re = a_re.astype(F32)
    lam_im = a_im.astype(F32)
    dt = jnp.exp(log_dt.astype(F32))[:, None]
    mag = jnp.exp(lam_re * dt)
    ab_re = mag * jnp.cos(lam_im * dt)
    ab_im = mag * jnp.sin(lam_im * dt)
    den = lam_re * lam_re + lam_im * lam_im
    n_re, n_im = ab_re - 1.0, ab_im
    f_re = (n_re * lam_re + n_im * lam_im) / den
    f_im = (n_im * lam_re - n_re * lam_im) / den
    br, bi = b_re.astype(F32), b_im.astype(F32)
    bb_re = f_re[..., None] * br - f_im[..., None] * bi
    bb_im = f_re[..., None] * bi + f_im[..., None] * br
    return lam_re, lam_im, dt, ab_re, ab_im, bb_re, bb_im


def _s5_chunk_operators(lam_re, lam_im, dt, bb_re, bb_im, c_re, c_im):
    L = S5_CHUNK
    hp = lax.Precision.HIGHEST
    steps = jnp.arange(L + 1, dtype=F32)[:, None, None]
    pmag = jnp.exp(lam_re[None] * dt[None] * steps)
    pw_re = pmag * jnp.cos(lam_im[None] * dt[None] * steps)
    pw_im = pmag * jnp.sin(lam_im[None] * dt[None] * steps)
    cr, ci = c_re.astype(F32), c_im.astype(F32)
    cp_re = cr[None] * pw_re[:L, :, None, :] - ci[None] * pw_im[:L, :, None, :]
    cp_im = cr[None] * pw_im[:L, :, None, :] + ci[None] * pw_re[:L, :, None, :]
    kern = (jnp.einsum('jgcp,gpd->jgcd', cp_re, bb_re, precision=hp)
            - jnp.einsum('jgcp,gpd->jgcd', cp_im, bb_im, precision=hp))
    s_idx = jnp.arange(L)[:, None]
    t_idx = jnp.arange(L)[None, :]
    lag = jnp.clip(t_idx - s_idx, 0, L - 1)
    toep = jnp.where((t_idx >= s_idx)[:, :, None, None, None], kern[lag], 0.0)
    toep = toep.transpose(2, 0, 4, 1, 3).reshape(N_GROUPS, L * GROUP_CH, L * GROUP_CH)
    rev_re = pw_re[L - 1 - jnp.arange(L)]
    rev_im = pw_im[L - 1 - jnp.arange(L)]
    w1_re = rev_re[:, :, :, None] * bb_re[None] - rev_im[:, :, :, None] * bb_im[None]
    w1_im = rev_re[:, :, :, None] * bb_im[None] + rev_im[:, :, :, None] * bb_re[None]
    w1_re = w1_re.transpose(1, 0, 3, 2).reshape(N_GROUPS, L * GROUP_CH, STATE_DIM)
    w1_im = w1_im.transpose(1, 0, 3, 2).reshape(N_GROUPS, L * GROUP_CH, STATE_DIM)
    nx_re = pw_re[1:]
    nx_im = pw_im[1:]
    w2_re = cr[None] * nx_re[:, :, None, :] - ci[None] * nx_im[:, :, None, :]
    w2_im = -(cr[None] * nx_im[:, :, None, :] + ci[None] * nx_re[:, :, None, :])
    w2_re = w2_re.transpose(1, 3, 0, 2).reshape(N_GROUPS, STATE_DIM, L * GROUP_CH)
    w2_im = w2_im.transpose(1, 3, 0, 2).reshape(N_GROUPS, STATE_DIM, L * GROUP_CH)
    return (toep.astype(BF16), w1_re.astype(BF16), w1_im.astype(BF16), w2_re.astype(BF16),
            w2_im.astype(BF16), pw_re[L][:, None, :], pw_im[L][:, None, :])


def _s5_prompt_kernel(batch, u_ref, toep_ref, w1r_ref, w1i_ref, w2r_ref, w2i_ref, ar_ref, ai_ref, d_ref,
                      y_ref, hr_ref, hi_ref, dre_ref, dim_ref, pre_ref, pim_ref):
    u = u_ref[0]
    ub = u.astype(BF16)
    dre_ref[...] = _dot(ub, w1r_ref[0])
    dim_ref[...] = _dot(ub, w1i_ref[0])
    a_re = ar_ref[0]
    a_im = ai_ref[0]
    n_chunks = u.shape[0] // batch

    def body(c, carry):
        h_re, h_im = carry
        rows = pl.ds(pl.multiple_of(c * batch, batch), batch)
        pre_ref[rows, :] = h_re
        pim_ref[rows, :] = h_im
        return (a_re * h_re - a_im * h_im + dre_ref[rows, :],
                a_re * h_im + a_im * h_re + dim_ref[rows, :])

    zero = jnp.zeros((batch, STATE_DIM), F32)
    h_re, h_im = lax.fori_loop(0, n_chunks, body, (zero, zero))
    hr_ref[0] = h_re
    hi_ref[0] = h_im
    y_ref[0] = (_dot(ub, toep_ref[0]) + _dot(pre_ref[...].astype(BF16), w2r_ref[0])
                + _dot(pim_ref[...].astype(BF16), w2i_ref[0]) + d_ref[0] * u)


def _s5_prompt(u_t, toep, w1_re, w1_im, w2_re, w2_im, al_re, al_im, d_tiled, batch):
    g, rows, cols = u_t.shape
    blk = lambda s: pl.BlockSpec((1,) + s, lambda i: (i, 0, 0))
    return pl.pallas_call(
        functools.partial(_s5_prompt_kernel, batch),
        grid=(g,),
        in_specs=[blk((rows, cols)), blk((cols, cols)), blk((cols, STATE_DIM)), blk((cols, STATE_DIM)),
                  blk((STATE_DIM, cols)), blk((STATE_DIM, cols)), blk((1, STATE_DIM)), blk((1, STATE_DIM)),
                  blk((1, cols))],
        out_specs=[blk((rows, cols)), blk((batch, STATE_DIM)), blk((batch, STATE_DIM))],
        out_shape=[jax.ShapeDtypeStruct((g, rows, cols), F32),
                   jax.ShapeDtypeStruct((g, batch, STATE_DIM), F32),
                   jax.ShapeDtypeStruct((g, batch, STATE_DIM), F32)],
        scratch_shapes=[pltpu.VMEM((rows, STATE_DIM), F32)] * 4,
        compiler_params=_cparams(("parallel",)),
        name="s5_prompt",
    )(u_t, toep, w1_re, w1_im, w2_re, w2_im, al_re, al_im, d_tiled)


def _s5_sample_kernel(u_ref, h0r_ref, h0i_ref, bdr_ref, bdi_ref, cdr_ref, cdi_ref, ar_ref, ai_ref, d_ref,
                      y_ref, hr_ref, hi_ref):
    u = u_ref[...]
    ub = u.astype(BF16)
    a_re = ar_ref[...]
    a_im = ai_ref[...]
    h0r = h0r_ref[...]
    h0i = h0i_ref[...]
    h_re = a_re * h0r - a_im * h0i + _dot(ub, bdr_ref[...])
    h_im = a_re * h0i + a_im * h0r + _dot(ub, bdi_ref[...])
    hr_ref[...] = h_re
    hi_ref[...] = h_im
    y_ref[...] = (_dot(h_re.astype(BF16), cdr_ref[...]) - _dot(h_im.astype(BF16), cdi_ref[...])
                  + d_ref[...] * u)


def _s5_sample(u_s, h0_re, h0_im, bd_re, bd_im, cd_re, cd_im, ab_re, ab_im, d_skip):
    n = u_s.shape[0]
    width = N_GROUPS * STATE_DIM
    return pl.pallas_call(
        _s5_sample_kernel,
        out_shape=[jax.ShapeDtypeStruct((n, SSM_W), F32), jax.ShapeDtypeStruct((n, width), F32),
                   jax.ShapeDtypeStruct((n, width), F32)],
        compiler_params=pltpu.CompilerParams(vmem_limit_bytes=VMEM_LIMIT),
        name="s5_sample",
    )(u_s, h0_re, h0_im, bd_re, bd_im, cd_re, cd_im, ab_re, ab_im, d_skip)


def _block_diag(m):
    g, r, c = m.shape
    eye = jnp.eye(g, dtype=m.dtype)
    return (m[:, :, None, :] * eye[:, None, :, None]).reshape(g * r, g * c)


def _merge_kernel(x_ref, attn_ref, yssm_ref, ga_ref, gs_ref, wau_ref, wglu_ref, wout_ref, g2_ref,
                  wr_ref, br_ref, su_ref, cnt_in_ref,
                  h_ref, xn_ref, idx_ref, rank_ref, w_ref, cnt_ref, run_ref):
    @pl.when(pl.program_id(0) == 0)
    def _():
        run_ref[...] = cnt_in_ref[...]

    branch_attn = _dot(attn_ref[...].astype(BF16), wau_ref[...])
    glu = _dot(jax.nn.gelu(yssm_ref[...]).astype(BF16), wglu_ref[...])
    branch_ssm = glu[:, :D_MODEL] * jax.nn.sigmoid(glu[:, D_MODEL:])
    merged = jax.nn.sigmoid(ga_ref[...]) * branch_attn + jax.nn.sigmoid(gs_ref[...]) * branch_ssm
    h = x_ref[...] + _dot(merged.astype(BF16), wout_ref[...])
    h_ref[...] = h
    xn = h * lax.rsqrt(jnp.mean(h * h, axis=-1, keepdims=True) + EPS) * g2_ref[...]
    xn_ref[...] = xn

    x_hi, x_lo = _split_hi_lo(xn)
    w_hi, w_lo = _split_hi_lo(wr_ref[...])
    logits = _dot(x_hi, w_hi) + _dot(x_lo, w_hi) + _dot(x_hi, w_lo)
    lt = logits.T[0:N_EXPERTS, :] + br_ref[...]
    rows = lt.shape[1]
    e_iota = lax.broadcasted_iota(I32, lt.shape, 0).astype(F32)
    tops, idxs, hots = [], [], []
    for _ in range(TOP_K):
        m = jnp.max(lt, axis=0, keepdims=True)
        idx = jnp.min(jnp.where(lt == m, e_iota, float(N_EXPERTS)), axis=0, keepdims=True)
        hot = e_iota == idx
        tops.append(m)
        idxs.append(idx)
        hots.append(hot)
        lt = jnp.where(hot, -jnp.inf, lt)
    exps = [jnp.exp(t - tops[0]) for t in tops]
    denom = exps[0] + exps[1] + exps[2] + exps[3]

    member = jnp.zeros(lt.shape, F32)
    for hot in hots:
        member = member + jnp.where(hot, 1.0, 0.0)
    before = run_ref[:, 0:1] + _dot(member.astype(BF16), su_ref[...])
    ranks = [jnp.sum(jnp.where(hot, before, 0.0), axis=0, keepdims=True) for hot in hots]
    run_ref[...] = run_ref[...] + jnp.sum(member, axis=1, keepdims=True)
    cnt_ref[...] = run_ref[...]

    zrows = jnp.zeros((SUBLANES - TOP_K, rows), F32)
    idx_ref[...] = jnp.concatenate(idxs + [zrows], axis=0).astype(I32)
    rank_ref[...] = jnp.concatenate(ranks + [zrows], axis=0).astype(I32)
    w_t = jnp.concatenate([e / denom for e in exps] + [jnp.zeros((LANES - TOP_K, rows), F32)], axis=0)
    w_ref[...] = w_t.T


def _merge(x, attn, y_ssm, g_attn, g_ssm, w_attn_up, w_ssm_glu, w_out, norm2_g, w_router_pad,
           b_router_col, cnt_in, tile):
    t = x.shape[0]
    row = lambda w: pl.BlockSpec((tile, w), lambda i: (i, 0))
    col = pl.BlockSpec((SUBLANES, tile), lambda i: (0, i))
    const = lambda shape: pl.BlockSpec(shape, lambda i: (0, 0))
    strict_upper = jnp.triu(jnp.ones((tile, tile), F32), 1).astype(BF16)
    return pl.pallas_call(
        _merge_kernel,
        grid=(t // tile,),
        in_specs=[row(D_MODEL), row(ATTN_W), row(SSM_W), row(D_MODEL), row(D_MODEL),
                  const((ATTN_W, D_MODEL)), const((SSM_W, 2 * D_MODEL)), const((D_MODEL, D_MODEL)),
                  const((1, D_MODEL)), const((D_MODEL, LANES)), const((N_EXPERTS, 1)),
                  const((tile, tile)), const((N_EXPERTS, LANES))],
        out_specs=[row(D_MODEL), row(D_MODEL), col, col, row(LANES), const((N_EXPERTS, LANES))],
        out_shape=[jax.ShapeDtypeStruct((t, D_MODEL), F32), jax.ShapeDtypeStruct((t, D_MODEL), F32),
                   jax.ShapeDtypeStruct((SUBLANES, t), I32), jax.ShapeDtypeStruct((SUBLANES, t), I32),
                   jax.ShapeDtypeStruct((t, LANES), F32), jax.ShapeDtypeStruct((N_EXPERTS, LANES), F32)],
        scratch_shapes=[pltpu.VMEM((N_EXPERTS, LANES), F32)],
        compiler_params=_cparams(("arbitrary",)),
        name="merge_router",
    )(x, attn, y_ssm, g_attn, g_ssm, w_attn_up, w_ssm_glu, w_out, norm2_g, w_router_pad, b_router_col,
      strict_upper, cnt_in)


def _dispatch_kernel(tok, pos_ref, x_ref, xs_in, xs_out, sem):
    del xs_in

    def issue(r, c):
        for k in range(TOP_K):
            pltpu.make_async_copy(x_ref.at[pl.ds(r, 1)], xs_out.at[pl.ds(pos_ref[k, r], 1)], sem).start()
        return c

    lax.fori_loop(0, tok, issue, 0, unroll=8)
    for _ in range(TOP_K):
        pltpu.make_async_copy(x_ref, xs_out.at[pl.ds(0, tok)], sem).wait()


def _dispatch(pos, xn, xs, tok):
    t, d_model = xn.shape
    return pl.pallas_call(
        functools.partial(_dispatch_kernel, tok),
        grid=(t // tok,),
        in_specs=[pl.BlockSpec((SUBLANES, tok), lambda i: (0, i), memory_space=pltpu.SMEM),
                  pl.BlockSpec((tok, d_model), lambda i: (i, 0)), pl.BlockSpec(memory_space=pl.ANY)],
        out_specs=pl.BlockSpec(memory_space=pl.ANY),
        out_shape=jax.ShapeDtypeStruct(xs.shape, xs.dtype),
        scratch_shapes=[pltpu.SemaphoreType.DMA(())],
        input_output_aliases={2: 0},
        compiler_params=_cparams(("arbitrary",)),
        name="moe_dispatch",
    )(pos, xn, xs)


def _moe_group_kernel(te_ref, nu_ref, x_ref, wg_ref, bg_ref, wu_ref, bu_ref, wd_ref, bd_ref, y_ref,
                      wgb_ref, wub_ref, wdb_ref):
    i = pl.program_id(0)

    @pl.when(i < nu_ref[0])
    def _():
        @pl.when((i == 0) | (te_ref[i] != te_ref[jnp.maximum(i - 1, 0)]))
        def _():
            wgb_ref[...] = wg_ref[0].astype(BF16)
            wub_ref[...] = wu_ref[0].astype(BF16)
            wdb_ref[...] = wd_ref[0].astype(BF16)

        x = x_ref[...].astype(BF16)
        d_ff = wgb_ref.shape[1]
        for c in range(d_ff // FF_CHUNK):
            cols = slice(c * FF_CHUNK, (c + 1) * FF_CHUNK)
            gate = _dot(x, wgb_ref[:, cols]) + bg_ref[0, :, cols]
            up = _dot(x, wub_ref[:, cols]) + bu_ref[0, :, cols]
            gate = jnp.minimum(gate, SWIGLU_LIMIT)
            up = jnp.clip(up, -SWIGLU_LIMIT, SWIGLU_LIMIT)
            hid = (up + 1.0) * (gate * jax.nn.sigmoid(SWIGLU_ALPHA * gate))
            part = _dot(hid.astype(BF16), wdb_ref[cols, :])
            if c == 0:
                y_ref[...] = part + bd_ref[0]
            else:
                y_ref[...] += part

    @pl.when(i >= nu_ref[0])
    def _():
        y_ref[...] = jnp.zeros(y_ref.shape, F32)


def _moe_grouped(tile_expert, n_used, xs, w_gate, b_gate, w_up, b_up, w_down, b_down):
    n_exp, d_model, d_ff = w_gate.shape
    n_tiles = xs.shape[0] // MOE_TM
    wspec = lambda r, c: pl.BlockSpec((1, r, c), lambda i, te, nu: (te[i], 0, 0))
    grid_spec = pltpu.PrefetchScalarGridSpec(
        num_scalar_prefetch=2,
        grid=(n_tiles,),
        in_specs=[pl.BlockSpec((MOE_TM, d_model), lambda i, te, nu: (jnp.minimum(i, nu[0] - 1), 0)),
                  wspec(d_model, d_ff), wspec(1, d_ff), wspec(d_model, d_ff), wspec(1, d_ff),
                  wspec(d_ff, d_model), wspec(1, d_model)],
        out_specs=pl.BlockSpec((MOE_TM, d_model), lambda i, te, nu: (i, 0)),
        scratch_shapes=[pltpu.VMEM((d_model, d_ff), BF16), pltpu.VMEM((d_model, d_ff), BF16),
                        pltpu.VMEM((d_ff, d_model), BF16)],
    )
    return pl.pallas_call(
        _moe_group_kernel,
        grid_spec=grid_spec,
        out_shape=jax.ShapeDtypeStruct((xs.shape[0], d_model), F32),
        compiler_params=_cparams(("arbitrary",)),
        name="moe_grouped",
    )(tile_expert, n_used, xs, w_gate, b_gate.reshape(n_exp, 1, d_ff), w_up, b_up.reshape(n_exp, 1, d_ff),
      w_down, b_down.reshape(n_exp, 1, d_model))


def _combine_kernel(tok, pos_ref, posn_ref, h_ref, w_ref, ys_hbm, o_ref, buf, sem):
    i = pl.program_id(0)
    n = pl.num_programs(0)
    slot = i % 2

    def gather(p_ref, s):
        def issue(r, c):
            for k in range(TOP_K):
                pltpu.make_async_copy(ys_hbm.at[pl.ds(p_ref[k, r], 1)], buf.at[s, k, pl.ds(r, 1)],
                                      sem.at[s]).start()
            return c
        lax.fori_loop(0, tok, issue, 0, unroll=8)

    @pl.when(i == 0)
    def _():
        gather(pos_ref, 0)

    @pl.when(i + 1 < n)
    def _():
        gather(posn_ref, 1 - slot)

    for k in range(TOP_K):
        pltpu.make_async_copy(ys_hbm.at[pl.ds(0, tok)], buf.at[slot, k], sem.at[slot]).wait()
    w = w_ref[...]
    out = h_ref[...]
    for k in range(TOP_K):
        out = out + w[:, k:k + 1] * buf[slot, k]
    o_ref[...] = out


def _combine(pos, h, w_rows, ys, tok):
    t, d_model = h.shape
    n = t // tok
    smem = lambda imap: pl.BlockSpec((SUBLANES, tok), imap, memory_space=pltpu.SMEM)
    row = lambda w: pl.BlockSpec((tok, w), lambda i: (i, 0))
    return pl.pallas_call(
        functools.partial(_combine_kernel, tok),
        grid=(n,),
        in_specs=[smem(lambda i: (0, i)), smem(lambda i: (0, jnp.minimum(i + 1, n - 1))),
                  row(d_model), row(LANES), pl.BlockSpec(memory_space=pl.ANY)],
        out_specs=row(d_model),
        out_shape=jax.ShapeDtypeStruct((t, d_model), F32),
        scratch_shapes=[pltpu.VMEM((2, TOP_K, tok, d_model), F32), pltpu.SemaphoreType.DMA((2,))],
        compiler_params=_cparams(("arbitrary",)),
        name="moe_combine",
    )(pos, pos, h, w_rows, ys)


def _routing_tables(counts, n_tiles):
    padded = (counts + MOE_TM - 1) // MOE_TM * MOE_TM
    ends = jnp.cumsum(padded)
    starts = ends - padded
    tile_row = jnp.arange(n_tiles, dtype=I32) * MOE_TM
    tile_expert = jnp.minimum(jnp.sum((tile_row[:, None] >= ends[None, :]).astype(I32), axis=1), N_EXPERTS - 1)
    n_used = jnp.maximum(ends[-1:] // MOE_TM, 1)
    return starts.astype(I32), tile_expert.astype(I32), n_used.astype(I32)


def _rope_tables(positions):
    half = HEAD_DIM // 2
    inv_freq = ROPE_THETA ** (-jnp.arange(half, dtype=F32) / half)
    ang = positions.astype(F32)[:, None] * inv_freq[None, :]
    cos = jnp.cos(ang)
    sin = jnp.sin(ang)
    cos_h = jnp.concatenate([cos, cos], axis=1)
    sin_h = jnp.concatenate([-sin, sin], axis=1)
    return jnp.tile(cos_h, (1, N_HEADS)), jnp.tile(sin_h, (1, N_HEADS))


def kernel(x_prompt, x_sample, cache_k, cache_v, state_ssm_re, state_ssm_im, page_table, norm1_g, w_in,
           q_norm_g, k_norm_g, ssm_a_re, ssm_a_im, ssm_log_dt, ssm_b_re, ssm_b_im, ssm_c_re, ssm_c_im,
           ssm_d, w_attn_up, w_ssm_glu, w_out, norm2_g, w_router, b_router, w_gate, b_gate, w_up, b_up,
           w_down, b_down):
    batch, seq, d_model = x_prompt.shape
    dec_batch = x_sample.shape[0]
    n_pages = page_table.shape[1]
    page = cache_k.shape[2]
    past_len = n_pages * page
    assert x_sample.shape[1] == 1 and w_in.shape[0] == 1
    assert seq % ROW_TILE == 0 and seq % MOBA_BLOCK == 0 and past_len % MOBA_BLOCK == 0
    assert page == LANES, "the sample attention keeps one page per lane tile"
    t_prompt = batch * seq
    heads = (N_HEADS, HEAD_DIM)

    xp = x_prompt.reshape(t_prompt, d_model)
    xs = x_sample.reshape(dec_batch, d_model)
    hd, hdt = _head_indicator()
    w_in_b = w_in[0].astype(BF16)
    q_gain = jnp.tile(q_norm_g, (1, N_HEADS))
    k_gain = jnp.tile(k_norm_g, (1, N_HEADS))
    cos_p, sin_p = _rope_tables(jnp.arange(seq, dtype=I32))
    cos_s, sin_s = _rope_tables(jnp.full((dec_batch,), past_len, I32))
    q_p, k_p, _, u_p, ga_p, gs_p, kt_p, vt_p = _inproj(xp, norm1_g, w_in_b, cos_p, sin_p, q_gain, k_gain,
                                                       hd, hdt, ROW_TILE)
    q_s, k_s, v_s, u_s, ga_s, gs_s, kt_s, vt_s = _inproj(xs, norm1_g, w_in_b, cos_s, sin_s, q_gain, k_gain,
                                                         hd, hdt, dec_batch)

    attn_p = _attn_prompt(q_p, k_p, vt_p, batch, seq)
    feature_major = lambda c: c[0].transpose(0, 2, 3, 1).reshape(c.shape[1], ATTN_W, page)
    lane0 = lambda a: jnp.pad(a[:, :, None], ((0, 0), (0, 0), (0, page - 1)))
    attn_s = _attn_sample(page_table, jnp.broadcast_to(q_s[:, :, None], (dec_batch, ATTN_W, page)),
                          lane0(k_s), lane0(v_s), feature_major(cache_k), feature_major(cache_v))

    lam_re, lam_im, dt, ab_re, ab_im, bb_re, bb_im = _s5_discretize(
        ssm_a_re[0], ssm_a_im[0], ssm_log_dt[0], ssm_b_re[0], ssm_b_im[0])
    toep, w1_re, w1_im, w2_re, w2_im, al_re, al_im = _s5_chunk_operators(
        lam_re, lam_im, dt, bb_re, bb_im, ssm_c_re[0], ssm_c_im[0])
    L = S5_CHUNK
    n_chunks = seq // L
    u_t = (u_p.reshape(batch, n_chunks, L, N_GROUPS, GROUP_CH)
           .transpose(3, 1, 0, 2, 4).reshape(N_GROUPS, n_chunks * batch, L * GROUP_CH))
    d_tiled = jnp.tile(ssm_d[0].reshape(N_GROUPS, 1, GROUP_CH), (1, L, 1)).reshape(N_GROUPS, 1, L * GROUP_CH)
    y_t, hp_re, hp_im = _s5_prompt(u_t, toep, w1_re, w1_im, w2_re, w2_im, al_re, al_im, d_tiled, batch)
    y_p = (y_t.reshape(N_GROUPS, n_chunks, batch, L, GROUP_CH)
           .transpose(2, 1, 3, 0, 4).reshape(t_prompt, SSM_W))
    width = N_GROUPS * STATE_DIM
    y_s, hs_re, hs_im = _s5_sample(
        u_s, state_ssm_re[0].reshape(dec_batch, width), state_ssm_im[0].reshape(dec_batch, width),
        _block_diag(bb_re.transpose(0, 2, 1)).astype(BF16), _block_diag(bb_im.transpose(0, 2, 1)).astype(BF16),
        _block_diag(ssm_c_re[0].transpose(0, 2, 1)).astype(BF16),
        _block_diag(ssm_c_im[0].transpose(0, 2, 1)).astype(BF16),
        ab_re.reshape(1, width), ab_im.reshape(1, width), ssm_d)

    w_router_pad = jnp.concatenate([w_router[0], jnp.zeros((d_model, LANES - N_EXPERTS), F32)], axis=1)
    merge_w = (w_attn_up[0].astype(BF16), w_ssm_glu[0].astype(BF16), w_out[0].astype(BF16), norm2_g,
               w_router_pad, b_router[0].reshape(N_EXPERTS, 1))
    h_p, xn_p, idx_p, rank_p, wr_p, cnt_p = _merge(xp, attn_p, y_p, ga_p, gs_p, *merge_w,
                                                   jnp.zeros((N_EXPERTS, LANES), F32), ROW_TILE)
    h_s, xn_s, idx_s, rank_s, wr_s, cnt_all = _merge(xs, attn_s, y_s, ga_s, gs_s, *merge_w, cnt_p, dec_batch)

    n_pairs = (t_prompt + dec_batch) * TOP_K
    n_tiles = n_pairs // MOE_TM + N_EXPERTS
    starts, tile_expert, n_used = _routing_tables(cnt_all[:, 0].astype(I32), n_tiles)
    pos_p, pos_s = rank_p, rank_s
    for e in range(N_EXPERTS):
        pos_p = pos_p + jnp.where(idx_p == e, starts[e], 0)
        pos_s = pos_s + jnp.where(idx_s == e, starts[e], 0)
    x_sorted = jnp.zeros((n_tiles * MOE_TM, d_model), F32)
    x_sorted = _dispatch(pos_p, xn_p, x_sorted, MOE_TOK)
    x_sorted = _dispatch(pos_s, xn_s, x_sorted, dec_batch)
    y_sorted = _moe_grouped(tile_expert, n_used, x_sorted, w_gate[0], b_gate[0], w_up[0], b_up[0],
                            w_down[0], b_down[0])
    out_p = _combine(pos_p, h_p, wr_p, y_sorted, MOE_TOK)
    out_s = _combine(pos_s, h_s, wr_s, y_sorted, dec_batch)

    per_seq = lambda a, n, s: a.reshape((1, n) + heads + (s,)).transpose(0, 1, 4, 2, 3)
    new_rows = lambda a: per_seq(a, 1, dec_batch).transpose(0, 2, 1, 3, 4)
    return (out_p.reshape(batch, seq, d_model),
            out_s.reshape(dec_batch, 1, d_model),
            per_seq(kt_p, batch, seq),
            per_seq(vt_p, batch, seq),
            hp_re.transpose(1, 0, 2)[None],
            hp_im.transpose(1, 0, 2)[None],
            new_rows(kt_s),
            new_rows(vt_s),
            hs_re.reshape(1, dec_batch, N_GROUPS, STATE_DIM),
            hs_im.reshape(1, dec_batch, N_GROUPS, STATE_DIM))
```

```python
import functools
import math

import jax
import jax.numpy as jnp
from jax import lax
from jax.experimental import pallas as pl
from jax.experimental.pallas import tpu as pltpu

F32 = jnp.float32
BF16 = jnp.bfloat16
I32 = jnp.int32

D_MODEL = 1024
N_HEADS = 8
HEAD_DIM = 64
ATTN_W = N_HEADS * HEAD_DIM
MOBA_BLOCK = 256
MOBA_TOP = 3
ROPE_THETA = 10000.0
SSM_W = 512
GROUP_CH = 16
N_GROUPS = 32
STATE_DIM = 64
N_EXPERTS = 32
TOP_K = 4
SWIGLU_ALPHA = 1.702
SWIGLU_LIMIT = 7.0
EPS = 1e-6
IN_W = 4096
NEG_BIG = -1e9

LANES = 128
SUBLANES = 8
ROW_TILE = 512
S5_CHUNK = LANES
MOE_TM = 512
MOE_TOK = 256
FF_CHUNK = 512
VMEM_LIMIT = 56 * 1024 * 1024


def _cparams(sem):
    return pltpu.CompilerParams(dimension_semantics=sem, vmem_limit_bytes=VMEM_LIMIT)


def _dot(a, b):
    return jnp.dot(a, b, preferred_element_type=F32)


def _split_hi_lo(a):
    hi = a.astype(BF16)
    lo = (a - hi.astype(F32)).astype(BF16)
    return hi, lo


def _dot_hilo(a, b_bf16):
    hi, lo = _split_hi_lo(a)
    return _dot(hi, b_bf16) + _dot(lo, b_bf16)


def _head_indicator():
    c = jnp.arange(ATTN_W)[:, None] // HEAD_DIM
    h = jnp.arange(LANES)[None, :]
    hd = (c == h).astype(BF16)
    return hd, hd.T


def _head_norm_rope(t, gain, cos, sin_signed, hd, hdt):
    ssq = _dot_hilo(t * t, hd)
    rstd = lax.rsqrt(ssq * (1.0 / HEAD_DIM) + EPS)
    t = t * _dot_hilo(rstd, hdt) * gain
    lane = lax.broadcasted_iota(I32, (t.shape[0], LANES), 1)
    first_half = (lane % HEAD_DIM) < (HEAD_DIM // 2)
    outs = []
    for s in range(ATTN_W // LANES):
        ts = t[:, s * LANES:(s + 1) * LANES]
        partner = jnp.where(first_half,
                            pltpu.roll(ts, LANES - HEAD_DIM // 2, axis=1),
                            pltpu.roll(ts, HEAD_DIM // 2, axis=1))
        outs.append(ts * cos[:, s * LANES:(s + 1) * LANES]
                    + partner * sin_signed[:, s * LANES:(s + 1) * LANES])
    return jnp.concatenate(outs, axis=1)


def _inproj_kernel(x_ref, g1_ref, w_ref, cos_ref, sin_ref, qg_ref, kg_ref, hd_ref, hdt_ref,
                   k_ref, u_ref, ga_ref, gs_ref, qt_ref, kt_ref, vt_ref, ut_ref):
    x = x_ref[...]
    xn = x * lax.rsqrt(jnp.mean(x * x, axis=-1, keepdims=True) + EPS) * g1_ref[...]
    xb = xn.astype(BF16)
    hd = hd_ref[...]
    hdt = hdt_ref[...]
    cos = cos_ref[...]
    sin = sin_ref[...]
    q = _dot(xb, w_ref[:, 0:ATTN_W])
    qt_ref[0] = _head_norm_rope(q, qg_ref[...], cos, sin, hd, hdt).T
    k = _head_norm_rope(_dot(xb, w_ref[:, ATTN_W:2 * ATTN_W]), kg_ref[...], cos, sin, hd, hdt)
    k_ref[...] = k
    kt_ref[0] = k.T
    vt_ref[0] = _dot(xb, w_ref[:, 2 * ATTN_W:3 * ATTN_W]).T
    u = _dot(xb, w_ref[:, 3 * ATTN_W:3 * ATTN_W + SSM_W])
    u_ref[...] = u
    ut_ref[0] = u.T
    ga_ref[...] = _dot(xb, w_ref[:, 2048:3072])
    gs_ref[...] = _dot(xb, w_ref[:, 3072:4096])


def _inproj(x, norm1_g, w_in_bf16, cos_tab, sin_tab, q_gain, k_gain, hd, hdt, tile):
    t = x.shape[0]
    period = cos_tab.shape[0] // tile
    row = lambda w: pl.BlockSpec((tile, w), lambda i: (i, 0))
    const = lambda shape: pl.BlockSpec(shape, lambda i: (0, 0))
    pos = pl.BlockSpec((tile, ATTN_W), lambda i: (i % period, 0))
    feature_major = pl.BlockSpec((1, ATTN_W, tile), lambda i: (i // period, 0, i % period))
    out_shape = ([jax.ShapeDtypeStruct((t, ATTN_W), F32)] * 2
                 + [jax.ShapeDtypeStruct((t, D_MODEL), F32)] * 2
                 + [jax.ShapeDtypeStruct((t // (period * tile), ATTN_W, period * tile), F32)] * 4)
    return pl.pallas_call(
        _inproj_kernel,
        grid=(t // tile,),
        in_specs=[row(D_MODEL), const((1, D_MODEL)), const((D_MODEL, IN_W)), pos, pos,
                  const((1, ATTN_W)), const((1, ATTN_W)), const((ATTN_W, LANES)), const((LANES, ATTN_W))],
        out_specs=[row(ATTN_W)] * 2 + [row(D_MODEL)] * 2 + [feature_major] * 4,
        out_shape=out_shape,
        compiler_params=_cparams(("parallel",)),
        name="inproj",
    )(x, norm1_g, w_in_bf16, cos_tab, sin_tab, q_gain, k_gain, hd, hdt)


def _block_penalty_t(gate_t, n_past, own_blk):
    blk = lax.broadcasted_iota(I32, gate_t.shape, 0)
    valid = blk < n_past
    g = jnp.where(valid, gate_t, -jnp.inf)
    cnt = jnp.zeros(gate_t.shape, I32)
    for m in range(gate_t.shape[0]):
        gm = g[m:m + 1, :]
        beats = jnp.where(gm > g, 1, jnp.where((gm == g) & (blk > m), 1, 0))
        cnt = cnt + beats
    keep = (valid & (cnt < MOBA_TOP)) | (blk == own_blk)
    return jnp.where(keep, 0.0, NEG_BIG)


def _attn_prompt_kernel(qt_ref, k_ref, vt_ref, o_ref, kaug_ref, vtb_ref, kmean_ref):
    j = pl.program_id(2)
    blk = MOBA_BLOCK
    nblk = k_ref.shape[0] // blk

    @pl.when(j == 0)
    def _():
        lane = lax.broadcasted_iota(I32, (blk, LANES), 1)
        for n in range(nblk):
            k = k_ref[n * blk:(n + 1) * blk, :]
            kaug_ref[n * blk:(n + 1) * blk, 0:LANES] = k.astype(BF16)
            kaug_ref[n * blk:(n + 1) * blk, LANES:2 * LANES] = jnp.where(lane == n, 1.0, 0.0).astype(BF16)
            kmean_ref[n:n + 1, :] = jnp.mean(k, axis=0, keepdims=True)
        vtb_ref[...] = vt_ref[0].astype(BF16)

    q_t = qt_ref[0]
    feat = lax.broadcasted_iota(I32, q_t.shape, 0)
    km_hi, km_lo = _split_hi_lo(kmean_ref[...])
    zpad = jnp.zeros((LANES - nblk, blk), F32)
    qaug = []
    for h in range(2):
        qh = jnp.where(feat >= HEAD_DIM if h == 1 else feat < HEAD_DIM, q_t, 0.0)
        q_hi, q_lo = _split_hi_lo(qh)
        gate_t = _dot(km_hi, q_hi) + _dot(km_hi, q_lo) + _dot(km_lo, q_hi)
        pen_t = jnp.concatenate([_block_penalty_t(gate_t, j, j), zpad], axis=0)
        qaug.append(jnp.concatenate([(qh * (1.0 / math.sqrt(HEAD_DIM))).astype(BF16),
                                     pen_t.astype(BF16)], axis=0))

    key_i = lax.broadcasted_iota(I32, (blk, blk), 0)
    qry_i = lax.broadcasted_iota(I32, (blk, blk), 1)
    future = key_i > qry_i

    def attend(jq):
        past = jq * blk
        outs = []
        for h in range(2):
            s_diag = jnp.where(future, NEG_BIG, _dot(kaug_ref[past:past + blk, :], qaug[h]))
            m = jnp.max(s_diag, axis=0, keepdims=True)
            if jq > 0:
                s_past = _dot(kaug_ref[0:past, :], qaug[h])
                m = jnp.maximum(m, jnp.max(s_past, axis=0, keepdims=True))
                p_past = jnp.exp(s_past - m)
            p_diag = jnp.exp(s_diag - m)
            l = jnp.sum(p_diag, axis=0, keepdims=True)
            feats = slice(h * HEAD_DIM, (h + 1) * HEAD_DIM)
            pv = _dot(vtb_ref[feats, past:past + blk], p_diag.astype(BF16))
            if jq > 0:
                l = l + jnp.sum(p_past, axis=0, keepdims=True)
                pv = pv + _dot(vtb_ref[feats, 0:past], p_past.astype(BF16))
            outs.append(pv / l)
        o_ref[...] = jnp.concatenate(outs, axis=0).T

    for jq in range(nblk):
        pl.when(j == jq)(functools.partial(attend, jq))


def _attn_prompt(qt, k, vt, batch, seq):
    nq = seq // MOBA_BLOCK
    return pl.pallas_call(
        _attn_prompt_kernel,
        grid=(batch, ATTN_W // LANES, nq),
        in_specs=[pl.BlockSpec((1, LANES, MOBA_BLOCK), lambda b, hp, j: (b, hp, j)),
                  pl.BlockSpec((seq, LANES), lambda b, hp, j: (b, hp)),
                  pl.BlockSpec((1, LANES, seq), lambda b, hp, j: (b, hp, 0))],
        out_specs=pl.BlockSpec((MOBA_BLOCK, LANES), lambda b, hp, j: (b * nq + j, hp)),
        out_shape=jax.ShapeDtypeStruct((batch * seq, ATTN_W), F32),
        scratch_shapes=[pltpu.VMEM((seq, 2 * LANES), BF16), pltpu.VMEM((LANES, seq), BF16),
                        pltpu.VMEM((nq, LANES), F32)],
        compiler_params=_cparams(("parallel", "parallel", "arbitrary")),
        name="attn_prompt",
    )(qt, k, vt)


def _attn_sample_kernel(n_pages, pt_ref, qt_ref, kt_ref, vt_ref, *rest):
    del pt_ref
    k_pages = rest[:n_pages]
    v_pages = rest[n_pages:2 * n_pages]
    o_ref = rest[2 * n_pages]
    b = pl.program_id(0)
    page = k_pages[0].shape[2]
    ppb = MOBA_BLOCK // page
    nblk = n_pages // ppb
    seq_i = lax.broadcasted_iota(I32, (qt_ref.shape[1], page), 0)
    lane_i = lax.broadcasted_iota(I32, (qt_ref.shape[1], page), 1)
    pick_all = jnp.where(seq_i == b, 1.0, 0.0).astype(BF16)
    pick_lane0 = jnp.where((seq_i == b) & (lane_i == 0), 1.0, 0.0).astype(BF16)
    qb = _dot_hilo(qt_ref[...], pick_all) * (1.0 / math.sqrt(HEAD_DIM))
    k_new = _dot_hilo(kt_ref[...], pick_lane0)
    v_new = _dot_hilo(vt_ref[...], pick_lane0)

    def head_sums(x):
        return jnp.sum(x.reshape(N_HEADS, HEAD_DIM, x.shape[1]), axis=1)

    def per_feature(p):
        return jnp.broadcast_to(p[:, None, :], (N_HEADS, HEAD_DIM, p.shape[1])).reshape(ATTN_W, p.shape[1])

    s_pages = [head_sums(k_pages[p][0] * qb) for p in range(n_pages)]
    lane = lax.broadcasted_iota(I32, s_pages[0].shape, 1)
    s_new = jnp.where(lane == 0, head_sums(k_new * qb), NEG_BIG)

    gates = []
    for n in range(nblk):
        blk_sum = s_pages[n * ppb]
        for i in range(1, ppb):
            blk_sum = blk_sum + s_pages[n * ppb + i]
        gates.append(jnp.sum(blk_sum, axis=1, keepdims=True))
    pens = []
    for n in range(nblk):
        cnt = jnp.zeros(gates[n].shape, I32)
        for m in range(nblk):
            if m == n:
                continue
            beats = (gates[m] >= gates[n]) if m < n else (gates[m] > gates[n])
            cnt = cnt + jnp.where(beats, 1, 0)
        pens.append(jnp.where(cnt < MOBA_TOP, 0.0, NEG_BIG))

    s_adj = [s_pages[p] + pens[p // ppb] for p in range(n_pages)]
    m = s_new
    for s in s_adj:
        m = jnp.maximum(m, s)
    m = jnp.max(m, axis=1, keepdims=True)
    e_new = jnp.exp(s_new - m)
    es = [jnp.exp(s - m) for s in s_adj]
    total = e_new
    for e in es:
        total = total + e
    inv = 1.0 / jnp.sum(total, axis=1, keepdims=True)
    acc = v_new * per_feature(e_new * inv)
    for p in range(n_pages):
        acc = acc + v_pages[p][0] * per_feature(es[p] * inv)
    a_hi, a_lo = _split_hi_lo(acc)
    ones = jnp.ones((SUBLANES, page), BF16)
    nt = (((1,), (1,)), ((), ()))
    row = (lax.dot_general(ones, a_hi, nt, preferred_element_type=F32)
           + lax.dot_general(ones, a_lo, nt, preferred_element_type=F32))
    o_ref[pl.ds(b, 1), :] = row[0:1, :]


def _attn_sample(page_table, qt, kt, vt, cache_kt, cache_vt):
    dec_batch, n_pages = page_table.shape
    page = cache_kt.shape[2]

    def page_spec(p):
        return pl.BlockSpec((1, ATTN_W, page), lambda b, pt: (pt[b, p], 0, 0))

    own = pl.BlockSpec((ATTN_W, dec_batch), lambda b, pt: (0, 0))
    grid_spec = pltpu.PrefetchScalarGridSpec(
        num_scalar_prefetch=1,
        grid=(dec_batch,),
        in_specs=[own] * 3 + [page_spec(p) for p in range(n_pages)] * 2,
        out_specs=pl.BlockSpec((dec_batch, ATTN_W), lambda b, pt: (0, 0)),
    )
    return pl.pallas_call(
        functools.partial(_attn_sample_kernel, n_pages),
        grid_spec=grid_spec,
        out_shape=jax.ShapeDtypeStruct((dec_batch, ATTN_W), F32),
        compiler_params=_cparams(("arbitrary",)),
        name="attn_sample",
    )(page_table, qt, kt, vt, *([cache_kt] * n_pages), *([cache_vt] * n_pages))


def _s5_discretize(a_re, a_im, log_dt, b_re, b_im):
    lam_re = a_re.astype(F32)
    lam_im = a_im.astype(F32)
    dt = jnp.exp(log_dt.astype(F32))[:, None]
    mag = jnp.exp(lam_re * dt)
    ab_re = mag * jnp.cos(lam_im * dt)
    ab_im = mag * jnp.sin(lam_im * dt)
    den = lam_re * lam_re + lam_im * lam_im
    n_re, n_im = ab_re - 1.0, ab_im
    f_re = (n_re * lam_re + n_im * lam_im) / den
    f_im = (n_im * lam_re - n_re * lam_im) / den
    br, bi = b_re.astype(F32), b_im.astype(F32)
    bb_re = f_re[..., None] * br - f_im[..., None] * bi
    bb_im = f_re[..., None] * bi + f_im[..., None] * br
    return lam_re, lam_im, dt, ab_re, ab_im, bb_re, bb_im


def _s5_chunk_operators(lam_re, lam_im, dt, bb_re, bb_im, c_re, c_im):
    L = S5_CHUNK
    hp = lax.Precision.HIGHEST
    steps = jnp.arange(L + 1, dtype=F32)[None, None, :]
    rate = dt[:, :, None] * steps
    pmag = jnp.exp(lam_re[:, :, None] * rate)
    pw_re = pmag * jnp.cos(lam_im[:, :, None] * rate)
    pw_im = pmag * jnp.sin(lam_im[:, :, None] * rate)
    cr, ci = c_re.astype(F32), c_im.astype(F32)
    btr, bti = bb_re.transpose(0, 2, 1), bb_im.transpose(0, 2, 1)
    bc_re = btr[:, :, None, :] * cr[:, None, :, :] - bti[:, :, None, :] * ci[:, None, :, :]
    bc_im = btr[:, :, None, :] * ci[:, None, :, :] + bti[:, :, None, :] * cr[:, None, :, :]
    pairs = GROUP_CH * GROUP_CH
    lags = (jnp.einsum('gxp,gpj->gxj', bc_re.reshape(N_GROUPS, pairs, STATE_DIM), pw_re[:, :, :L], precision=hp)
            - jnp.einsum('gxp,gpj->gxj', bc_im.reshape(N_GROUPS, pairs, STATE_DIM), pw_im[:, :, :L],
                         precision=hp))
    rev_re = pw_re[:, :, L - 1::-1].transpose(0, 2, 1)
    rev_im = pw_im[:, :, L - 1::-1].transpose(0, 2, 1)
    w1_re = rev_re[:, None, :, :] * btr[:, :, None, :] - rev_im[:, None, :, :] * bti[:, :, None, :]
    w1_im = rev_re[:, None, :, :] * bti[:, :, None, :] + rev_im[:, None, :, :] * btr[:, :, None, :]
    w1_re = w1_re.reshape(N_GROUPS, GROUP_CH * L, STATE_DIM)
    w1_im = w1_im.reshape(N_GROUPS, GROUP_CH * L, STATE_DIM)
    ctr, cti = cr.transpose(0, 2, 1), ci.transpose(0, 2, 1)
    nx_re, nx_im = pw_re[:, :, 1:], pw_im[:, :, 1:]
    w2_re = ctr[:, :, :, None] * nx_re[:, :, None, :] - cti[:, :, :, None] * nx_im[:, :, None, :]
    w2_im = -(ctr[:, :, :, None] * nx_im[:, :, None, :] + cti[:, :, :, None] * nx_re[:, :, None, :])
    w2_re = w2_re.reshape(N_GROUPS, STATE_DIM, GROUP_CH * L)
    w2_im = w2_im.reshape(N_GROUPS, STATE_DIM, GROUP_CH * L)
    return (lags, w1_re.astype(BF16), w1_im.astype(BF16), w2_re.astype(BF16), w2_im.astype(BF16),
            pw_re[:, None, :, L], pw_im[:, None, :, L])


def _s5_prompt_kernel(x_ref, lag_ref, w1r_ref, w1i_ref, w2r_ref, w2i_ref, ar_ref, ai_ref, d_ref,
                      y_ref, hr_ref, hi_ref, toep_ref, u_ref, pre_ref, pim_ref):
    nch, batch, seq = x_ref.shape
    L = S5_CHUNK
    n_chunks = seq // L

    s_i = lax.broadcasted_iota(I32, (L, L), 0)
    t_i = lax.broadcasted_iota(I32, (L, L), 1)
    causal = t_i >= s_i
    for cp in range(nch):
        for c in range(nch):
            first = jnp.broadcast_to(lag_ref[0, cp * nch + c:cp * nch + c + 1, :], (L, L))
            tile = pltpu.roll(first, 0, 1, stride=1, stride_axis=0)
            toep_ref[cp * L:(cp + 1) * L, c * L:(c + 1) * L] = jnp.where(causal, tile, 0.0).astype(BF16)

    for k in range(n_chunks):
        for cp in range(nch):
            u_ref[k * batch:(k + 1) * batch, cp * L:(cp + 1) * L] = x_ref[cp, :, k * L:(k + 1) * L]
    ub = u_ref[...].astype(BF16)

    d_re = _dot(ub, w1r_ref[0])
    d_im = _dot(ub, w1i_ref[0])
    a_re = ar_ref[0]
    a_im = ai_ref[0]
    h_re = jnp.zeros((batch, STATE_DIM), F32)
    h_im = jnp.zeros((batch, STATE_DIM), F32)
    for k in range(n_chunks):
        rows = slice(k * batch, (k + 1) * batch)
        pre_ref[rows, :] = h_re
        pim_ref[rows, :] = h_im
        h_re, h_im = (a_re * h_re - a_im * h_im + d_re[rows, :], a_re * h_im + a_im * h_re + d_im[rows, :])
    hr_ref[0] = h_re
    hi_ref[0] = h_im

    y = (_dot(ub, toep_ref[...]) + _dot(pre_ref[...].astype(BF16), w2r_ref[0])
         + _dot(pim_ref[...].astype(BF16), w2i_ref[0]))
    for k in range(n_chunks):
        for c in range(nch):
            lanes = slice(k * L, (k + 1) * L)
            y_ref[c, :, lanes] = (y[k * batch:(k + 1) * batch, c * L:(c + 1) * L]
                                  + d_ref[c] * x_ref[c, :, lanes])


def _s5_prompt(x, lags, w1_re, w1_im, w2_re, w2_im, al_re, al_im, d_rep):
    width, batch, seq = x.shape
    g = width // GROUP_CH
    cols = GROUP_CH * S5_CHUNK
    rows = seq // S5_CHUNK * batch
    blk = lambda s: pl.BlockSpec((1,) + s, lambda i: (i, 0, 0))
    chan = pl.BlockSpec((GROUP_CH, batch, seq), lambda i: (i, 0, 0))
    return pl.pallas_call(
        _s5_prompt_kernel,
        grid=(g,),
        in_specs=[chan, blk((GROUP_CH * GROUP_CH, S5_CHUNK)), blk((cols, STATE_DIM)), blk((cols, STATE_DIM)),
                  blk((STATE_DIM, cols)), blk((STATE_DIM, cols)), blk((1, STATE_DIM)), blk((1, STATE_DIM)),
                  pl.BlockSpec((GROUP_CH, batch, S5_CHUNK), lambda i: (i, 0, 0))],
        out_specs=[chan, blk((batch, STATE_DIM)), blk((batch, STATE_DIM))],
        out_shape=[jax.ShapeDtypeStruct((width, batch, seq), F32),
                   jax.ShapeDtypeStruct((g, batch, STATE_DIM), F32),
                   jax.ShapeDtypeStruct((g, batch, STATE_DIM), F32)],
        scratch_shapes=[pltpu.VMEM((cols, cols), BF16), pltpu.VMEM((rows, cols), F32),
                        pltpu.VMEM((rows, STATE_DIM), F32), pltpu.VMEM((rows, STATE_DIM), F32)],
        compiler_params=_cparams(("parallel",)),
        name="s5_prompt",
    )(x, lags, w1_re, w1_im, w2_re, w2_im, al_re, al_im, d_rep)


def _s5_sample_kernel(u_ref, h0r_ref, h0i_ref, bdr_ref, bdi_ref, cdr_ref, cdi_ref, ar_ref, ai_ref, d_ref,
                      y_ref, hr_ref, hi_ref):
    u = u_ref[...]
    ub = u.astype(BF16)
    a_re = ar_ref[...]
    a_im = ai_ref[...]
    h0r = h0r_ref[...]
    h0i = h0i_ref[...]
    h_re = a_re * h0r - a_im * h0i + _dot(ub, bdr_ref[...])
    h_im = a_re * h0i + a_im * h0r + _dot(ub, bdi_ref[...])
    hr_ref[...] = h_re
    hi_ref[...] = h_im
    y_ref[...] = (_dot(h_re.astype(BF16), cdr_ref[...]) - _dot(h_im.astype(BF16), cdi_ref[...])
                  + d_ref[...] * u)


def _s5_sample(u_s, h0_re, h0_im, bd_re, bd_im, cd_re, cd_im, ab_re, ab_im, d_skip):
    n = u_s.shape[0]
    width = N_GROUPS * STATE_DIM
    return pl.pallas_call(
        _s5_sample_kernel,
        out_shape=[jax.ShapeDtypeStruct((n, SSM_W), F32), jax.ShapeDtypeStruct((n, width), F32),
                   jax.ShapeDtypeStruct((n, width), F32)],
        compiler_params=pltpu.CompilerParams(vmem_limit_bytes=VMEM_LIMIT),
        name="s5_sample",
    )(u_s, h0_re, h0_im, bd_re, bd_im, cd_re, cd_im, ab_re, ab_im, d_skip)


def _block_diag(m):
    g, r, c = m.shape
    eye = jnp.eye(g, dtype=m.dtype)
    return (m[:, :, None, :] * eye[:, None, :, None]).reshape(g * r, g * c)


def _merge_kernel(x_ref, attn_ref, yssm_ref, ga_ref, gs_ref, wau_ref, wglu_ref, wout_ref, g2_ref,
                  wr_ref, br_ref, su_ref, cnt_in_ref,
                  h_ref, xn_ref, idx_ref, rank_ref, w_ref, cnt_ref, run_ref):
    @pl.when(pl.program_id(0) == 0)
    def _():
        run_ref[...] = cnt_in_ref[...]

    branch_attn = _dot(attn_ref[...].astype(BF16), wau_ref[...])
    glu = _dot(jax.nn.gelu(yssm_ref[0].T).astype(BF16), wglu_ref[...])
    branch_ssm = glu[:, :D_MODEL] * jax.nn.sigmoid(glu[:, D_MODEL:])
    merged = jax.nn.sigmoid(ga_ref[...]) * branch_attn + jax.nn.sigmoid(gs_ref[...]) * branch_ssm
    h = x_ref[...] + _dot(merged.astype(BF16), wout_ref[...])
    h_ref[...] = h
    xn = h * lax.rsqrt(jnp.mean(h * h, axis=-1, keepdims=True) + EPS) * g2_ref[...]
    xn_ref[...] = xn

    x_hi, x_lo = _split_hi_lo(xn)
    w_hi, w_lo = _split_hi_lo(wr_ref[...])
    logits = _dot(x_hi, w_hi) + _dot(x_lo, w_hi) + _dot(x_hi, w_lo)
    lt = logits.T[0:N_EXPERTS, :] + br_ref[...]
    rows = lt.shape[1]
    e_iota = lax.broadcasted_iota(I32, lt.shape, 0).astype(F32)
    tops, idxs, hots = [], [], []
    for _ in range(TOP_K):
        m = jnp.max(lt, axis=0, keepdims=True)
        idx = jnp.min(jnp.where(lt == m, e_iota, float(N_EXPERTS)), axis=0, keepdims=True)
        hot = e_iota == idx
        tops.append(m)
        idxs.append(idx)
        hots.append(hot)
        lt = jnp.where(hot, -jnp.inf, lt)
    exps = [jnp.exp(t - tops[0]) for t in tops]
    denom = exps[0] + exps[1] + exps[2] + exps[3]

    member = jnp.zeros(lt.shape, F32)
    for hot in hots:
        member = member + jnp.where(hot, 1.0, 0.0)
    before = run_ref[:, 0:1] + _dot(member.astype(BF16), su_ref[...])
    ranks = [jnp.sum(jnp.where(hot, before, 0.0), axis=0, keepdims=True) for hot in hots]
    run_ref[...] = run_ref[...] + jnp.sum(member, axis=1, keepdims=True)
    cnt_ref[...] = run_ref[...]

    zrows = jnp.zeros((SUBLANES - TOP_K, rows), F32)
    idx_ref[...] = jnp.concatenate(idxs + [zrows], axis=0).astype(I32)
    rank_ref[...] = jnp.concatenate(ranks + [zrows], axis=0).astype(I32)
    w_t = jnp.concatenate([e / denom for e in exps] + [jnp.zeros((LANES - TOP_K, rows), F32)], axis=0)
    w_ref[...] = w_t.T


def _merge(x, attn, y_ssm_t, g_attn, g_ssm, w_attn_up, w_ssm_glu, w_out, norm2_g, w_router_pad,
           b_router_col, cnt_in, tile):
    t = x.shape[0]
    period = y_ssm_t.shape[2] // tile
    row = lambda w: pl.BlockSpec((tile, w), lambda i: (i, 0))
    col = pl.BlockSpec((SUBLANES, tile), lambda i: (0, i))
    const = lambda shape: pl.BlockSpec(shape, lambda i: (0, 0))
    feature_major = pl.BlockSpec((1, SSM_W, tile), lambda i: (i // period, 0, i % period))
    strict_upper = jnp.triu(jnp.ones((tile, tile), F32), 1).astype(BF16)
    return pl.pallas_call(
        _merge_kernel,
        grid=(t // tile,),
        in_specs=[row(D_MODEL), row(ATTN_W), feature_major, row(D_MODEL), row(D_MODEL),
                  const((ATTN_W, D_MODEL)), const((SSM_W, 2 * D_MODEL)), const((D_MODEL, D_MODEL)),
                  const((1, D_MODEL)), const((D_MODEL, LANES)), const((N_EXPERTS, 1)),
                  const((tile, tile)), const((N_EXPERTS, LANES))],
        out_specs=[row(D_MODEL), row(D_MODEL), col, col, row(LANES), const((N_EXPERTS, LANES))],
        out_shape=[jax.ShapeDtypeStruct((t, D_MODEL), F32), jax.ShapeDtypeStruct((t, D_MODEL), F32),
                   jax.ShapeDtypeStruct((SUBLANES, t), I32), jax.ShapeDtypeStruct((SUBLANES, t), I32),
                   jax.ShapeDtypeStruct((t, LANES), F32), jax.ShapeDtypeStruct((N_EXPERTS, LANES), F32)],
        scratch_shapes=[pltpu.VMEM((N_EXPERTS, LANES), F32)],
        compiler_params=_cparams(("arbitrary",)),
        name="merge_router",
    )(x, attn, y_ssm_t, g_attn, g_ssm, w_attn_up, w_ssm_glu, w_out, norm2_g, w_router_pad, b_router_col,
      strict_upper, cnt_in)


def _dispatch_kernel(tok, pos_ref, x_ref, xs_in, xs_out, sem):
    del xs_in

    def issue(r, c):
        for k in range(TOP_K):
            pltpu.make_async_copy(x_ref.at[pl.ds(r, 1)], xs_out.at[pl.ds(pos_ref[k, r], 1)], sem).start()
        return c

    lax.fori_loop(0, tok, issue, 0, unroll=8)
    for _ in range(TOP_K):
        pltpu.make_async_copy(x_ref, xs_out.at[pl.ds(0, tok)], sem).wait()


def _dispatch(pos, xn, xs, tok):
    t, d_model = xn.shape
    return pl.pallas_call(
        functools.partial(_dispatch_kernel, tok),
        grid=(t // tok,),
        in_specs=[pl.BlockSpec((SUBLANES, tok), lambda i: (0, i), memory_space=pltpu.SMEM),
                  pl.BlockSpec((tok, d_model), lambda i: (i, 0)), pl.BlockSpec(memory_space=pl.ANY)],
        out_specs=pl.BlockSpec(memory_space=pl.ANY),
        out_shape=jax.ShapeDtypeStruct(xs.shape, xs.dtype),
        scratch_shapes=[pltpu.SemaphoreType.DMA(())],
        input_output_aliases={2: 0},
        compiler_params=_cparams(("arbitrary",)),
        name="moe_dispatch",
    )(pos, xn, xs)


def _moe_group_kernel(te_ref, nu_ref, x_ref, wg_ref, bg_ref, wu_ref, bu_ref, wd_ref, bd_ref, y_ref,
                      wgb_ref, wub_ref, wdb_ref):
    i = pl.program_id(0)

    @pl.when(i < nu_ref[0])
    def _():
        @pl.when((i == 0) | (te_ref[i] != te_ref[jnp.maximum(i - 1, 0)]))
        def _():
            wgb_ref[...] = wg_ref[0].astype(BF16)
            wub_ref[...] = wu_ref[0].astype(BF16)
            wdb_ref[...] = wd_ref[0].astype(BF16)

        x = x_ref[...].astype(BF16)
        d_ff = wgb_ref.shape[1]
        for c in range(d_ff // FF_CHUNK):
            cols = slice(c * FF_CHUNK, (c + 1) * FF_CHUNK)
            gate = _dot(x, wgb_ref[:, cols]) + bg_ref[0, :, cols]
            up = _dot(x, wub_ref[:, cols]) + bu_ref[0, :, cols]
            gate = jnp.minimum(gate, SWIGLU_LIMIT)
            up = jnp.clip(up, -SWIGLU_LIMIT, SWIGLU_LIMIT)
            hid = (up + 1.0) * (gate * jax.nn.sigmoid(SWIGLU_ALPHA * gate))
            part = _dot(hid.astype(BF16), wdb_ref[cols, :])
            if c == 0:
                y_ref[...] = part + bd_ref[0]
            else:
                y_ref[...] += part

    @pl.when(i >= nu_ref[0])
    def _():
        y_ref[...] = jnp.zeros(y_ref.shape, F32)


def _moe_grouped(tile_expert, n_used, xs, w_gate, b_gate, w_up, b_up, w_down, b_down):
    n_exp, d_model, d_ff = w_gate.shape
    n_tiles = xs.shape[0] // MOE_TM
    wspec = lambda r, c: pl.BlockSpec((1, r, c), lambda i, te, nu: (te[i], 0, 0))
    grid_spec = pltpu.PrefetchScalarGridSpec(
        num_scalar_prefetch=2,
        grid=(n_tiles,),
        in_specs=[pl.BlockSpec((MOE_TM, d_model), lambda i, te, nu: (jnp.minimum(i, nu[0] - 1), 0)),
                  wspec(d_model, d_ff), wspec(1, d_ff), wspec(d_model, d_ff), wspec(1, d_ff),
                  wspec(d_ff, d_model), wspec(1, d_model)],
        out_specs=pl.BlockSpec((MOE_TM, d_model), lambda i, te, nu: (i, 0)),
        scratch_shapes=[pltpu.VMEM((d_model, d_ff), BF16), pltpu.VMEM((d_model, d_ff), BF16),
                        pltpu.VMEM((d_ff, d_model), BF16)],
    )
    return pl.pallas_call(
        _moe_group_kernel,
        grid_spec=grid_spec,
        out_shape=jax.ShapeDtypeStruct((xs.shape[0], d_model), F32),
        compiler_params=_cparams(("arbitrary",)),
        name="moe_grouped",
    )(tile_expert, n_used, xs, w_gate, b_gate.reshape(n_exp, 1, d_ff), w_up, b_up.reshape(n_exp, 1, d_ff),
      w_down, b_down.reshape(n_exp, 1, d_model))


def _combine_kernel(tok, pos_ref, posn_ref, h_ref, w_ref, ys_hbm, o_ref, buf, sem):
    i = pl.program_id(0)
    n = pl.num_programs(0)
    slot = i % 2

    def gather(p_ref, s):
        def issue(r, c):
            for k in range(TOP_K):
                pltpu.make_async_copy(ys_hbm.at[pl.ds(p_ref[k, r], 1)], buf.at[s, k, pl.ds(r, 1)],
                                      sem.at[s]).start()
            return c
        lax.fori_loop(0, tok, issue, 0, unroll=8)

    @pl.when(i == 0)
    def _():
        gather(pos_ref, 0)

    @pl.when(i + 1 < n)
    def _():
        gather(posn_ref, 1 - slot)

    for k in range(TOP_K):
        pltpu.make_async_copy(ys_hbm.at[pl.ds(0, tok)], buf.at[slot, k], sem.at[slot]).wait()
    w = w_ref[...]
    out = h_ref[...]
    for k in range(TOP_K):
        out = out + w[:, k:k + 1] * buf[slot, k]
    o_ref[...] = out


def _combine(pos, h, w_rows, ys, tok):
    t, d_model = h.shape
    n = t // tok
    smem = lambda imap: pl.BlockSpec((SUBLANES, tok), imap, memory_space=pltpu.SMEM)
    row = lambda w: pl.BlockSpec((tok, w), lambda i: (i, 0))
    return pl.pallas_call(
        functools.partial(_combine_kernel, tok),
        grid=(n,),
        in_specs=[smem(lambda i: (0, i)), smem(lambda i: (0, jnp.minimum(i + 1, n - 1))),
                  row(d_model), row(LANES), pl.BlockSpec(memory_space=pl.ANY)],
        out_specs=row(d_model),
        out_shape=jax.ShapeDtypeStruct((t, d_model), F32),
        scratch_shapes=[pltpu.VMEM((2, TOP_K, tok, d_model), F32), pltpu.SemaphoreType.DMA((2,))],
        compiler_params=_cparams(("arbitrary",)),
        name="moe_combine",
    )(pos, pos, h, w_rows, ys)


def _routing_tables(counts, n_tiles):
    padded = (counts + MOE_TM - 1) // MOE_TM * MOE_TM
    ends = jnp.cumsum(padded)
    starts = ends - padded
    tile_row = jnp.arange(n_tiles, dtype=I32) * MOE_TM
    tile_expert = jnp.minimum(jnp.sum((tile_row[:, None] >= ends[None, :]).astype(I32), axis=1), N_EXPERTS - 1)
    n_used = jnp.maximum(ends[-1:] // MOE_TM, 1)
    return starts.astype(I32), tile_expert.astype(I32), n_used.astype(I32)


def _rope_tables(positions):
    half = HEAD_DIM // 2
    inv_freq = ROPE_THETA ** (-jnp.arange(half, dtype=F32) / half)
    ang = positions.astype(F32)[:, None] * inv_freq[None, :]
    cos = jnp.cos(ang)
    sin = jnp.sin(ang)
    cos_h = jnp.concatenate([cos, cos], axis=1)
    sin_h = jnp.concatenate([-sin, sin], axis=1)
    return jnp.tile(cos_h, (1, N_HEADS)), jnp.tile(sin_h, (1, N_HEADS))


def kernel(x_prompt, x_sample, cache_k, cache_v, state_ssm_re, state_ssm_im, page_table, norm1_g, w_in,
           q_norm_g, k_norm_g, ssm_a_re, ssm_a_im, ssm_log_dt, ssm_b_re, ssm_b_im, ssm_c_re, ssm_c_im,
           ssm_d, w_attn_up, w_ssm_glu, w_out, norm2_g, w_router, b_router, w_gate, b_gate, w_up, b_up,
           w_down, b_down):
    batch, seq, d_model = x_prompt.shape
    dec_batch = x_sample.shape[0]
    n_pages = page_table.shape[1]
    page = cache_k.shape[2]
    past_len = n_pages * page
    assert x_sample.shape[1] == 1 and w_in.shape[0] == 1
    assert seq % ROW_TILE == 0 and seq % MOBA_BLOCK == 0 and past_len % MOBA_BLOCK == 0
    assert page == LANES, "the sample attention keeps one page per lane tile"
    t_prompt = batch * seq
    heads = (N_HEADS, HEAD_DIM)

    xp = x_prompt.reshape(t_prompt, d_model)
    xs = x_sample.reshape(dec_batch, d_model)
    hd, hdt = _head_indicator()
    w_in_b = w_in[0].astype(BF16)
    q_gain = jnp.tile(q_norm_g, (1, N_HEADS))
    k_gain = jnp.tile(k_norm_g, (1, N_HEADS))
    cos_p, sin_p = _rope_tables(jnp.arange(seq, dtype=I32))
    cos_s, sin_s = _rope_tables(jnp.full((dec_batch,), past_len, I32))
    k_p, _, ga_p, gs_p, qt_p, kt_p, vt_p, ut_p = _inproj(xp, norm1_g, w_in_b, cos_p, sin_p, q_gain, k_gain,
                                                         hd, hdt, ROW_TILE)
    _, u_s, ga_s, gs_s, qt_s, kt_s, vt_s, _ = _inproj(xs, norm1_g, w_in_b, cos_s, sin_s, q_gain, k_gain,
                                                      hd, hdt, dec_batch)

    attn_p = _attn_prompt(qt_p, k_p, vt_p, batch, seq)
    feature_major = lambda c: c[0].transpose(0, 2, 3, 1).reshape(c.shape[1], ATTN_W, page)
    attn_s = _attn_sample(page_table, qt_s[0], kt_s[0], vt_s[0], feature_major(cache_k),
                          feature_major(cache_v))

    assert seq % S5_CHUNK == 0
    lam_re, lam_im, dt, ab_re, ab_im, bb_re, bb_im = _s5_discretize(
        ssm_a_re[0], ssm_a_im[0], ssm_log_dt[0], ssm_b_re[0], ssm_b_im[0])
    lags, w1_re, w1_im, w2_re, w2_im, al_re, al_im = _s5_chunk_operators(
        lam_re, lam_im, dt, bb_re, bb_im, ssm_c_re[0], ssm_c_im[0])
    d_rep = jnp.broadcast_to(ssm_d[0][:, None, None], (SSM_W, batch, S5_CHUNK))
    y_t, hp_re, hp_im = _s5_prompt(ut_p.transpose(1, 0, 2), lags, w1_re, w1_im, w2_re, w2_im, al_re, al_im,
                                   d_rep)
    yt_p = y_t.transpose(1, 0, 2)
    width = N_GROUPS * STATE_DIM
    y_s, hs_re, hs_im = _s5_sample(
        u_s, state_ssm_re[0].reshape(dec_batch, width), state_ssm_im[0].reshape(dec_batch, width),
        _block_diag(bb_re.transpose(0, 2, 1)).astype(BF16), _block_diag(bb_im.transpose(0, 2, 1)).astype(BF16),
        _block_diag(ssm_c_re[0].transpose(0, 2, 1)).astype(BF16),
        _block_diag(ssm_c_im[0].transpose(0, 2, 1)).astype(BF16),
        ab_re.reshape(1, width), ab_im.reshape(1, width), ssm_d)

    w_router_pad = jnp.concatenate([w_router[0], jnp.zeros((d_model, LANES - N_EXPERTS), F32)], axis=1)
    merge_w = (w_attn_up[0].astype(BF16), w_ssm_glu[0].astype(BF16), w_out[0].astype(BF16), norm2_g,
               w_router_pad, b_router[0].reshape(N_EXPERTS, 1))
    h_p, xn_p, idx_p, rank_p, wr_p, cnt_p = _merge(xp, attn_p, yt_p, ga_p, gs_p, *merge_w,
                                                   jnp.zeros((N_EXPERTS, LANES), F32), ROW_TILE)
    h_s, xn_s, idx_s, rank_s, wr_s, cnt_all = _merge(xs, attn_s, y_s.T[None], ga_s, gs_s, *merge_w, cnt_p,
                                                     dec_batch)

    n_pairs = (t_prompt + dec_batch) * TOP_K
    n_tiles = n_pairs // MOE_TM + N_EXPERTS
    starts, tile_expert, n_used = _routing_tables(cnt_all[:, 0].astype(I32), n_tiles)
    pos_p, pos_s = rank_p, rank_s
    for e in range(N_EXPERTS):
        pos_p = pos_p + jnp.where(idx_p == e, starts[e], 0)
        pos_s = pos_s + jnp.where(idx_s == e, starts[e], 0)
    x_sorted = jnp.zeros((n_tiles * MOE_TM, d_model), F32)
    x_sorted = _dispatch(pos_p, xn_p, x_sorted, MOE_TOK)
    x_sorted = _dispatch(pos_s, xn_s, x_sorted, dec_batch)
    y_sorted = _moe_grouped(tile_expert, n_used, x_sorted, w_gate[0], b_gate[0], w_up[0], b_up[0],
                            w_down[0], b_down[0])
    out_p = _combine(pos_p, h_p, wr_p, y_sorted, MOE_TOK)
    out_s = _combine(pos_s, h_s, wr_s, y_sorted, dec_batch)

    per_seq = lambda a, n, s: a.reshape((1, n) + heads + (s,)).transpose(0, 1, 4, 2, 3)
    new_rows = lambda a: per_seq(a, 1, dec_batch).transpose(0, 2, 1, 3, 4)
    return (out_p.reshape(batch, seq, d_model),
            out_s.reshape(dec_batch, 1, d_model),
            per_seq(kt_p, batch, seq),
            per_seq(vt_p, batch, seq),
            hp_re.transpose(1, 0, 2)[None],
            hp_im.transpose(1, 0, 2)[None],
            new_rows(kt_s),
            new_rows(vt_s),
            hs_re.reshape(1, dec_batch, N_GROUPS, STATE_DIM),
            hs_im.reshape(1, dec_batch, N_GROUPS, STATE_DIM))
```

```python
import functools
import math

import jax
import jax.numpy as jnp
from jax import lax
from jax.experimental import pallas as pl
from jax.experimental.pallas import tpu as pltpu

F32 = jnp.float32
BF16 = jnp.bfloat16
I32 = jnp.int32

D_MODEL = 1024
N_HEADS = 8
HEAD_DIM = 64
ATTN_W = N_HEADS * HEAD_DIM
MOBA_BLOCK = 256
MOBA_TOP = 3
ROPE_THETA = 10000.0
SSM_W = 512
GROUP_CH = 16
N_GROUPS = 32
STATE_DIM = 64
N_EXPERTS = 32
TOP_K = 4
SWIGLU_ALPHA = 1.702
SWIGLU_LIMIT = 7.0
EPS = 1e-6
IN_W = 4096
NEG_BIG = -1e9

LANES = 128
SUBLANES = 8
ROW_TILE = 512
S5_CHUNK = LANES
MOE_TM = 512
MOE_TOK = 256
FF_CHUNK = 512
VMEM_LIMIT = 56 * 1024 * 1024


def _cparams(sem):
    return pltpu.CompilerParams(dimension_semantics=sem, vmem_limit_bytes=VMEM_LIMIT)


def _dot(a, b):
    return jnp.dot(a, b, preferred_element_type=F32)


def _split_hi_lo(a):
    hi = a.astype(BF16)
    lo = (a - hi.astype(F32)).astype(BF16)
    return hi, lo


def _dot_hilo(a, b_bf16):
    hi, lo = _split_hi_lo(a)
    return _dot(hi, b_bf16) + _dot(lo, b_bf16)


def _head_indicator():
    c = jnp.arange(ATTN_W)[:, None] // HEAD_DIM
    h = jnp.arange(LANES)[None, :]
    hd = (c == h).astype(BF16)
    return hd, hd.T


def _head_norm_rope(t, gain, cos, sin_signed, hd, hdt):
    ssq = _dot_hilo(t * t, hd)
    rstd = lax.rsqrt(ssq * (1.0 / HEAD_DIM) + EPS)
    t = t * _dot_hilo(rstd, hdt) * gain
    lane = lax.broadcasted_iota(I32, (t.shape[0], LANES), 1)
    first_half = (lane % HEAD_DIM) < (HEAD_DIM // 2)
    outs = []
    for s in range(ATTN_W // LANES):
        ts = t[:, s * LANES:(s + 1) * LANES]
        partner = jnp.where(first_half,
                            pltpu.roll(ts, LANES - HEAD_DIM // 2, axis=1),
                            pltpu.roll(ts, HEAD_DIM // 2, axis=1))
        outs.append(ts * cos[:, s * LANES:(s + 1) * LANES]
                    + partner * sin_signed[:, s * LANES:(s + 1) * LANES])
    return jnp.concatenate(outs, axis=1)


def _inproj_kernel(x_ref, g1_ref, w_ref, cos_ref, sin_ref, qg_ref, kg_ref, hd_ref, hdt_ref,
                   k_ref, u_ref, ga_ref, gs_ref, qt_ref, kt_ref, vt_ref, ut_ref):
    x = x_ref[...]
    xn = x * lax.rsqrt(jnp.mean(x * x, axis=-1, keepdims=True) + EPS) * g1_ref[...]
    xb = xn.astype(BF16)
    hd = hd_ref[...]
    hdt = hdt_ref[...]
    cos = cos_ref[...]
    sin = sin_ref[...]
    q = _dot(xb, w_ref[:, 0:ATTN_W])
    qt_ref[0] = _head_norm_rope(q, qg_ref[...], cos, sin, hd, hdt).T
    k = _head_norm_rope(_dot(xb, w_ref[:, ATTN_W:2 * ATTN_W]), kg_ref[...], cos, sin, hd, hdt)
    k_ref[...] = k
    kt_ref[0] = k.T
    vt_ref[0] = _dot(xb, w_ref[:, 2 * ATTN_W:3 * ATTN_W]).T
    u = _dot(xb, w_ref[:, 3 * ATTN_W:3 * ATTN_W + SSM_W])
    u_ref[...] = u
    ut_ref[0] = u.T
    ga_ref[...] = _dot(xb, w_ref[:, 2048:3072])
    gs_ref[...] = _dot(xb, w_ref[:, 3072:4096])


def _inproj(x, norm1_g, w_in_bf16, cos_tab, sin_tab, q_gain, k_gain, hd, hdt, tile):
    t = x.shape[0]
    period = cos_tab.shape[0] // tile
    row = lambda w: pl.BlockSpec((tile, w), lambda i: (i, 0))
    const = lambda shape: pl.BlockSpec(shape, lambda i: (0, 0))
    pos = pl.BlockSpec((tile, ATTN_W), lambda i: (i % period, 0))
    feature_major = pl.BlockSpec((1, ATTN_W, tile), lambda i: (i // period, 0, i % period))
    out_shape = ([jax.ShapeDtypeStruct((t, ATTN_W), F32)] * 2
                 + [jax.ShapeDtypeStruct((t, D_MODEL), F32)] * 2
                 + [jax.ShapeDtypeStruct((t // (period * tile), ATTN_W, period * tile), F32)] * 4)
    return pl.pallas_call(
        _inproj_kernel,
        grid=(t // tile,),
        in_specs=[row(D_MODEL), const((1, D_MODEL)), const((D_MODEL, IN_W)), pos, pos,
                  const((1, ATTN_W)), const((1, ATTN_W)), const((ATTN_W, LANES)), const((LANES, ATTN_W))],
        out_specs=[row(ATTN_W)] * 2 + [row(D_MODEL)] * 2 + [feature_major] * 4,
        out_shape=out_shape,
        compiler_params=_cparams(("parallel",)),
        name="inproj",
    )(x, norm1_g, w_in_bf16, cos_tab, sin_tab, q_gain, k_gain, hd, hdt)


def _block_penalty_t(gate_t, n_past, own_blk):
    blk = lax.broadcasted_iota(I32, gate_t.shape, 0)
    valid = blk < n_past
    g = jnp.where(valid, gate_t, -jnp.inf)
    cnt = jnp.zeros(gate_t.shape, I32)
    for m in range(gate_t.shape[0]):
        gm = g[m:m + 1, :]
        beats = jnp.where(gm > g, 1, jnp.where((gm == g) & (blk > m), 1, 0))
        cnt = cnt + beats
    keep = (valid & (cnt < MOBA_TOP)) | (blk == own_blk)
    return jnp.where(keep, 0.0, NEG_BIG)


def _attn_prompt_kernel(qt_ref, k_ref, vt_ref, o_ref, kaug_ref, vtb_ref, kmean_ref):
    j = pl.program_id(2)
    blk = MOBA_BLOCK
    nblk = k_ref.shape[0] // blk

    @pl.when(j == 0)
    def _():
        lane = lax.broadcasted_iota(I32, (blk, LANES), 1)
        for n in range(nblk):
            k = k_ref[n * blk:(n + 1) * blk, :]
            kaug_ref[n * blk:(n + 1) * blk, 0:LANES] = k.astype(BF16)
            kaug_ref[n * blk:(n + 1) * blk, LANES:2 * LANES] = jnp.where(lane == n, 1.0, 0.0).astype(BF16)
            kmean_ref[n:n + 1, :] = jnp.mean(k, axis=0, keepdims=True)
        vtb_ref[...] = vt_ref[0].astype(BF16)

    q_t = qt_ref[0]
    feat = lax.broadcasted_iota(I32, q_t.shape, 0)
    km_hi, km_lo = _split_hi_lo(kmean_ref[...])
    zpad = jnp.zeros((LANES - nblk, blk), F32)
    qaug = []
    for h in range(2):
        qh = jnp.where(feat >= HEAD_DIM if h == 1 else feat < HEAD_DIM, q_t, 0.0)
        q_hi, q_lo = _split_hi_lo(qh)
        gate_t = _dot(km_hi, q_hi) + _dot(km_hi, q_lo) + _dot(km_lo, q_hi)
        pen_t = jnp.concatenate([_block_penalty_t(gate_t, j, j), zpad], axis=0)
        qaug.append(jnp.concatenate([(qh * (1.0 / math.sqrt(HEAD_DIM))).astype(BF16),
                                     pen_t.astype(BF16)], axis=0))

    key_i = lax.broadcasted_iota(I32, (blk, blk), 0)
    qry_i = lax.broadcasted_iota(I32, (blk, blk), 1)
    future = key_i > qry_i

    def attend(jq):
        past = jq * blk
        outs = []
        for h in range(2):
            s_diag = jnp.where(future, NEG_BIG, _dot(kaug_ref[past:past + blk, :], qaug[h]))
            m = jnp.max(s_diag, axis=0, keepdims=True)
            if jq > 0:
                s_past = _dot(kaug_ref[0:past, :], qaug[h])
                m = jnp.maximum(m, jnp.max(s_past, axis=0, keepdims=True))
                p_past = jnp.exp(s_past - m)
            p_diag = jnp.exp(s_diag - m)
            l = jnp.sum(p_diag, axis=0, keepdims=True)
            feats = slice(h * HEAD_DIM, (h + 1) * HEAD_DIM)
            pv = _dot(vtb_ref[feats, past:past + blk], p_diag.astype(BF16))
            if jq > 0:
                l = l + jnp.sum(p_past, axis=0, keepdims=True)
                pv = pv + _dot(vtb_ref[feats, 0:past], p_past.astype(BF16))
            outs.append(pv / l)
        o_ref[...] = jnp.concatenate(outs, axis=0).T

    for jq in range(nblk):
        pl.when(j == jq)(functools.partial(attend, jq))


def _attn_prompt(qt, k, vt, batch, seq):
    nq = seq // MOBA_BLOCK
    return pl.pallas_call(
        _attn_prompt_kernel,
        grid=(batch, ATTN_W // LANES, nq),
        in_specs=[pl.BlockSpec((1, LANES, MOBA_BLOCK), lambda b, hp, j: (b, hp, j)),
                  pl.BlockSpec((seq, LANES), lambda b, hp, j: (b, hp)),
                  pl.BlockSpec((1, LANES, seq), lambda b, hp, j: (b, hp, 0))],
        out_specs=pl.BlockSpec((MOBA_BLOCK, LANES), lambda b, hp, j: (b * nq + j, hp)),
        out_shape=jax.ShapeDtypeStruct((batch * seq, ATTN_W), F32),
        scratch_shapes=[pltpu.VMEM((seq, 2 * LANES), BF16), pltpu.VMEM((LANES, seq), BF16),
                        pltpu.VMEM((nq, LANES), F32)],
        compiler_params=_cparams(("parallel", "parallel", "arbitrary")),
        name="attn_prompt",
    )(qt, k, vt)


def _attn_sample_kernel(n_pages, pt_ref, qt_ref, kt_ref, vt_ref, *rest):
    del pt_ref
    k_pages = rest[:n_pages]
    v_pages = rest[n_pages:2 * n_pages]
    o_ref = rest[2 * n_pages]
    b = pl.program_id(0)
    page = k_pages[0].shape[2]
    ppb = MOBA_BLOCK // page
    nblk = n_pages // ppb
    seq_i = lax.broadcasted_iota(I32, (qt_ref.shape[1], page), 0)
    lane_i = lax.broadcasted_iota(I32, (qt_ref.shape[1], page), 1)
    pick_all = jnp.where(seq_i == b, 1.0, 0.0).astype(BF16)
    pick_lane0 = jnp.where((seq_i == b) & (lane_i == 0), 1.0, 0.0).astype(BF16)
    qb = _dot_hilo(qt_ref[...], pick_all) * (1.0 / math.sqrt(HEAD_DIM))
    k_new = _dot_hilo(kt_ref[...], pick_lane0)
    v_new = _dot_hilo(vt_ref[...], pick_lane0)

    def head_sums(x):
        return jnp.sum(x.reshape(N_HEADS, HEAD_DIM, x.shape[1]), axis=1)

    def per_feature(p):
        return jnp.broadcast_to(p[:, None, :], (N_HEADS, HEAD_DIM, p.shape[1])).reshape(ATTN_W, p.shape[1])

    s_pages = [head_sums(k_pages[p][0] * qb) for p in range(n_pages)]
    lane = lax.broadcasted_iota(I32, s_pages[0].shape, 1)
    s_new = jnp.where(lane == 0, head_sums(k_new * qb), NEG_BIG)

    gates = []
    for n in range(nblk):
        blk_sum = s_pages[n * ppb]
        for i in range(1, ppb):
            blk_sum = blk_sum + s_pages[n * ppb + i]
        gates.append(jnp.sum(blk_sum, axis=1, keepdims=True))
    pens = []
    for n in range(nblk):
        cnt = jnp.zeros(gates[n].shape, I32)
        for m in range(nblk):
            if m == n:
                continue
            beats = (gates[m] >= gates[n]) if m < n else (gates[m] > gates[n])
            cnt = cnt + jnp.where(beats, 1, 0)
        pens.append(jnp.where(cnt < MOBA_TOP, 0.0, NEG_BIG))

    s_adj = [s_pages[p] + pens[p // ppb] for p in range(n_pages)]
    m = s_new
    for s in s_adj:
        m = jnp.maximum(m, s)
    m = jnp.max(m, axis=1, keepdims=True)
    e_new = jnp.exp(s_new - m)
    es = [jnp.exp(s - m) for s in s_adj]
    total = e_new
    for e in es:
        total = total + e
    inv = 1.0 / jnp.sum(total, axis=1, keepdims=True)
    acc = v_new * per_feature(e_new * inv)
    for p in range(n_pages):
        acc = acc + v_pages[p][0] * per_feature(es[p] * inv)
    a_hi, a_lo = _split_hi_lo(acc)
    ones = jnp.ones((SUBLANES, page), BF16)
    nt = (((1,), (1,)), ((), ()))
    row = (lax.dot_general(ones, a_hi, nt, preferred_element_type=F32)
           + lax.dot_general(ones, a_lo, nt, preferred_element_type=F32))
    o_ref[pl.ds(b, 1), :] = row[0:1, :]


def _attn_sample(page_table, qt, kt, vt, cache_kt, cache_vt):
    dec_batch, n_pages = page_table.shape
    page = cache_kt.shape[2]

    def page_spec(p):
        return pl.BlockSpec((1, ATTN_W, page), lambda b, pt: (pt[b, p], 0, 0))

    own = pl.BlockSpec((ATTN_W, dec_batch), lambda b, pt: (0, 0))
    grid_spec = pltpu.PrefetchScalarGridSpec(
        num_scalar_prefetch=1,
        grid=(dec_batch,),
        in_specs=[own] * 3 + [page_spec(p) for p in range(n_pages)] * 2,
        out_specs=pl.BlockSpec((dec_batch, ATTN_W), lambda b, pt: (0, 0)),
    )
    return pl.pallas_call(
        functools.partial(_attn_sample_kernel, n_pages),
        grid_spec=grid_spec,
        out_shape=jax.ShapeDtypeStruct((dec_batch, ATTN_W), F32),
        compiler_params=_cparams(("arbitrary",)),
        name="attn_sample",
    )(page_table, qt, kt, vt, *([cache_kt] * n_pages), *([cache_vt] * n_pages))


def _s5_discretize(a_re, a_im, log_dt, b_re, b_im):
    lam_re = a_re.astype(F32)
    lam_im = a_im.astype(F32)
    dt = jnp.exp(log_dt.astype(F32))[:, None]
    mag = jnp.exp(lam_re * dt)
    ab_re = mag * jnp.cos(lam_im * dt)
    ab_im = mag * jnp.sin(lam_im * dt)
    den = lam_re * lam_re + lam_im * lam_im
    n_re, n_im = ab_re - 1.0, ab_im
    f_re = (n_re * lam_re + n_im * lam_im) / den
    f_im = (n_im * lam_re - n_re * lam_im) / den
    br, bi = b_re.astype(F32), b_im.astype(F32)
    bb_re = f_re[..., None] * br - f_im[..., None] * bi
    bb_im = f_re[..., None] * bi + f_im[..., None] * br
    return lam_re, lam_im, dt, ab_re, ab_im, bb_re, bb_im


def _s5_chunk_operators(lam_re, lam_im, dt, bb_re, bb_im, c_re, c_im):
    L = S5_CHUNK
    hp = lax.Precision.HIGHEST
    steps = jnp.arange(L + 1, dtype=F32)[None, None, :]
    rate = dt[:, :, None] * steps
    pmag = jnp.exp(lam_re[:, :, None] * rate)
    pw_re = pmag * jnp.cos(lam_im[:, :, None] * rate)
    pw_im = pmag * jnp.sin(lam_im[:, :, None] * rate)
    cr, ci = c_re.astype(F32), c_im.astype(F32)
    btr, bti = bb_re.transpose(0, 2, 1), bb_im.transpose(0, 2, 1)
    bc_re = btr[:, :, None, :] * cr[:, None, :, :] - bti[:, :, None, :] * ci[:, None, :, :]
    bc_im = btr[:, :, None, :] * ci[:, None, :, :] + bti[:, :, None, :] * cr[:, None, :, :]
    pairs = GROUP_CH * GROUP_CH
    lags = (jnp.einsum('gxp,gpj->gxj', bc_re.reshape(N_GROUPS, pairs, STATE_DIM), pw_re[:, :, :L], precision=hp)
            - jnp.einsum('gxp,gpj->gxj', bc_im.reshape(N_GROUPS, pairs, STATE_DIM), pw_im[:, :, :L],
                         precision=hp))
    rev_re = pw_re[:, :, L - 1::-1].transpose(0, 2, 1)
    rev_im = pw_im[:, :, L - 1::-1].transpose(0, 2, 1)
    w1_re = rev_re[:, None, :, :] * btr[:, :, None, :] - rev_im[:, None, :, :] * bti[:, :, None, :]
    w1_im = rev_re[:, None, :, :] * bti[:, :, None, :] + rev_im[:, None, :, :] * btr[:, :, None, :]
    w1_re = w1_re.reshape(N_GROUPS, GROUP_CH * L, STATE_DIM)
    w1_im = w1_im.reshape(N_GROUPS, GROUP_CH * L, STATE_DIM)
    ctr, cti = cr.transpose(0, 2, 1), ci.transpose(0, 2, 1)
    nx_re, nx_im = pw_re[:, :, 1:], pw_im[:, :, 1:]
    w2_re = ctr[:, :, :, None] * nx_re[:, :, None, :] - cti[:, :, :, None] * nx_im[:, :, None, :]
    w2_im = -(ctr[:, :, :, None] * nx_im[:, :, None, :] + cti[:, :, :, None] * nx_re[:, :, None, :])
    w2_re = w2_re.reshape(N_GROUPS, STATE_DIM, GROUP_CH * L)
    w2_im = w2_im.reshape(N_GROUPS, STATE_DIM, GROUP_CH * L)
    return (lags, w1_re.astype(BF16), w1_im.astype(BF16), w2_re.astype(BF16), w2_im.astype(BF16),
            pw_re[:, None, :, L], pw_im[:, None, :, L])


def _s5_prompt_kernel(x_ref, lag_ref, w1r_ref, w1i_ref, w2r_ref, w2i_ref, ar_ref, ai_ref, d_ref,
                      y_ref, hr_ref, hi_ref, toep_ref, u_ref, pre_ref, pim_ref):
    nch, batch, seq = x_ref.shape
    L = S5_CHUNK
    n_chunks = seq // L

    s_i = lax.broadcasted_iota(I32, (L, L), 0)
    t_i = lax.broadcasted_iota(I32, (L, L), 1)
    causal = t_i >= s_i
    for cp in range(nch):
        for c in range(nch):
            first = jnp.broadcast_to(lag_ref[0, cp * nch + c:cp * nch + c + 1, :], (L, L))
            tile = pltpu.roll(first, 0, 1, stride=1, stride_axis=0)
            toep_ref[cp * L:(cp + 1) * L, c * L:(c + 1) * L] = jnp.where(causal, tile, 0.0).astype(BF16)

    for k in range(n_chunks):
        for cp in range(nch):
            u_ref[k * batch:(k + 1) * batch, cp * L:(cp + 1) * L] = x_ref[cp, :, k * L:(k + 1) * L]
    ub = u_ref[...].astype(BF16)

    d_re = _dot(ub, w1r_ref[0])
    d_im = _dot(ub, w1i_ref[0])
    a_re = ar_ref[0]
    a_im = ai_ref[0]
    h_re = jnp.zeros((batch, STATE_DIM), F32)
    h_im = jnp.zeros((batch, STATE_DIM), F32)
    for k in range(n_chunks):
        rows = slice(k * batch, (k + 1) * batch)
        pre_ref[rows, :] = h_re
        pim_ref[rows, :] = h_im
        h_re, h_im = (a_re * h_re - a_im * h_im + d_re[rows, :], a_re * h_im + a_im * h_re + d_im[rows, :])
    hr_ref[0] = h_re
    hi_ref[0] = h_im

    y = (_dot(ub, toep_ref[...]) + _dot(pre_ref[...].astype(BF16), w2r_ref[0])
         + _dot(pim_ref[...].astype(BF16), w2i_ref[0]))
    for k in range(n_chunks):
        for c in range(nch):
            lanes = slice(k * L, (k + 1) * L)
            y_ref[c, :, lanes] = (y[k * batch:(k + 1) * batch, c * L:(c + 1) * L]
                                  + d_ref[c] * x_ref[c, :, lanes])


def _s5_prompt(x, lags, w1_re, w1_im, w2_re, w2_im, al_re, al_im, d_rep):
    width, batch, seq = x.shape
    g = width // GROUP_CH
    cols = GROUP_CH * S5_CHUNK
    rows = seq // S5_CHUNK * batch
    blk = lambda s: pl.BlockSpec((1,) + s, lambda i: (i, 0, 0))
    chan = pl.BlockSpec((GROUP_CH, batch, seq), lambda i: (i, 0, 0))
    return pl.pallas_call(
        _s5_prompt_kernel,
        grid=(g,),
        in_specs=[chan, blk((GROUP_CH * GROUP_CH, S5_CHUNK)), blk((cols, STATE_DIM)), blk((cols, STATE_DIM)),
                  blk((STATE_DIM, cols)), blk((STATE_DIM, cols)), blk((1, STATE_DIM)), blk((1, STATE_DIM)),
                  pl.BlockSpec((GROUP_CH, batch, S5_CHUNK), lambda i: (i, 0, 0))],
        out_specs=[chan, blk((batch, STATE_DIM)), blk((batch, STATE_DIM))],
        out_shape=[jax.ShapeDtypeStruct((width, batch, seq), F32),
                   jax.ShapeDtypeStruct((g, batch, STATE_DIM), F32),
                   jax.ShapeDtypeStruct((g, batch, STATE_DIM), F32)],
        scratch_shapes=[pltpu.VMEM((cols, cols), BF16), pltpu.VMEM((rows, cols), F32),
                        pltpu.VMEM((rows, STATE_DIM), F32), pltpu.VMEM((rows, STATE_DIM), F32)],
        compiler_params=_cparams(("parallel",)),
        name="s5_prompt",
    )(x, lags, w1_re, w1_im, w2_re, w2_im, al_re, al_im, d_rep)


def _s5_sample_kernel(u_ref, h0r_ref, h0i_ref, bdr_ref, bdi_ref, cdr_ref, cdi_ref, ar_ref, ai_ref, d_ref,
                      y_ref, hr_ref, hi_ref):
    u = u_ref[...]
    ub = u.astype(BF16)
    a_re = ar_ref[...]
    a_im = ai_ref[...]
    h0r = h0r_ref[...]
    h0i = h0i_ref[...]
    h_re = a_re * h0r - a_im * h0i + _dot(ub, bdr_ref[...])
    h_im = a_re * h0i + a_im * h0r + _dot(ub, bdi_ref[...])
    hr_ref[...] = h_re
    hi_ref[...] = h_im
    y_ref[...] = (_dot(h_re.astype(BF16), cdr_ref[...]) - _dot(h_im.astype(BF16), cdi_ref[...])
                  + d_ref[...] * u)


def _s5_sample(u_s, h0_re, h0_im, bd_re, bd_im, cd_re, cd_im, ab_re, ab_im, d_skip):
    n = u_s.shape[0]
    width = N_GROUPS * STATE_DIM
    return pl.pallas_call(
        _s5_sample_kernel,
        out_shape=[jax.ShapeDtypeStruct((n, SSM_W), F32), jax.ShapeDtypeStruct((n, width), F32),
                   jax.ShapeDtypeStruct((n, width), F32)],
        compiler_params=pltpu.CompilerParams(vmem_limit_bytes=VMEM_LIMIT),
        name="s5_sample",
    )(u_s, h0_re, h0_im, bd_re, bd_im, cd_re, cd_im, ab_re, ab_im, d_skip)


def _block_diag(m):
    g, r, c = m.shape
    eye = jnp.eye(g, dtype=m.dtype)
    return (m[:, :, None, :] * eye[:, None, :, None]).reshape(g * r, g * c)


def _merge_kernel(x_ref, attn_ref, yssm_ref, ga_ref, gs_ref, wau_ref, wglu_ref, wout_ref, g2_ref,
                  wr_ref, br_ref, su_ref, cnt_in_ref,
                  h_ref, xn_ref, idx_ref, rank_ref, w_ref, cnt_ref, run_ref):
    @pl.when(pl.program_id(0) == 0)
    def _():
        run_ref[...] = cnt_in_ref[...]

    branch_attn = _dot(attn_ref[...].astype(BF16), wau_ref[...])
    glu = _dot(jax.nn.gelu(yssm_ref[0].T).astype(BF16), wglu_ref[...])
    branch_ssm = glu[:, :D_MODEL] * jax.nn.sigmoid(glu[:, D_MODEL:])
    merged = jax.nn.sigmoid(ga_ref[...]) * branch_attn + jax.nn.sigmoid(gs_ref[...]) * branch_ssm
    h = x_ref[...] + _dot(merged.astype(BF16), wout_ref[...])
    h_ref[...] = h
    xn = h * lax.rsqrt(jnp.mean(h * h, axis=-1, keepdims=True) + EPS) * g2_ref[...]
    xn_ref[...] = xn

    x_hi, x_lo = _split_hi_lo(xn)
    w_hi, w_lo = _split_hi_lo(wr_ref[...])
    logits = _dot(x_hi, w_hi) + _dot(x_lo, w_hi) + _dot(x_hi, w_lo)
    lt = logits.T[0:N_EXPERTS, :] + br_ref[...]
    rows = lt.shape[1]
    e_iota = lax.broadcasted_iota(I32, lt.shape, 0).astype(F32)
    tops, idxs, hots = [], [], []
    for _ in range(TOP_K):
        m = jnp.max(lt, axis=0, keepdims=True)
        idx = jnp.min(jnp.where(lt == m, e_iota, float(N_EXPERTS)), axis=0, keepdims=True)
        hot = e_iota == idx
        tops.append(m)
        idxs.append(idx)
        hots.append(hot)
        lt = jnp.where(hot, -jnp.inf, lt)
    exps = [jnp.exp(t - tops[0]) for t in tops]
    denom = exps[0] + exps[1] + exps[2] + exps[3]

    member = jnp.zeros(lt.shape, F32)
    for hot in hots:
        member = member + jnp.where(hot, 1.0, 0.0)
    before = run_ref[:, 0:1] + _dot(member.astype(BF16), su_ref[...])
    ranks = [jnp.sum(jnp.where(hot, before, 0.0), axis=0, keepdims=True) for hot in hots]
    run_ref[...] = run_ref[...] + jnp.sum(member, axis=1, keepdims=True)
    cnt_ref[...] = run_ref[...]

    zrows = jnp.zeros((SUBLANES - TOP_K, rows), F32)
    idx_ref[...] = jnp.concatenate(idxs + [zrows], axis=0).astype(I32)
    rank_ref[...] = jnp.concatenate(ranks + [zrows], axis=0).astype(I32)
    w_t = jnp.concatenate([e / denom for e in exps] + [jnp.zeros((LANES - TOP_K, rows), F32)], axis=0)
    w_ref[...] = w_t.T


def _merge(x, attn, y_ssm_t, g_attn, g_ssm, w_attn_up, w_ssm_glu, w_out, norm2_g, w_router_pad,
           b_router_col, cnt_in, tile):
    t = x.shape[0]
    period = y_ssm_t.shape[2] // tile
    row = lambda w: pl.BlockSpec((tile, w), lambda i: (i, 0))
    col = pl.BlockSpec((SUBLANES, tile), lambda i: (0, i))
    const = lambda shape: pl.BlockSpec(shape, lambda i: (0, 0))
    feature_major = pl.BlockSpec((1, SSM_W, tile), lambda i: (i // period, 0, i % period))
    strict_upper = jnp.triu(jnp.ones((tile, tile), F32), 1).astype(BF16)
    return pl.pallas_call(
        _merge_kernel,
        grid=(t // tile,),
        in_specs=[row(D_MODEL), row(ATTN_W), feature_major, row(D_MODEL), row(D_MODEL),
                  const((ATTN_W, D_MODEL)), const((SSM_W, 2 * D_MODEL)), const((D_MODEL, D_MODEL)),
                  const((1, D_MODEL)), const((D_MODEL, LANES)), const((N_EXPERTS, 1)),
                  const((tile, tile)), const((N_EXPERTS, LANES))],
        out_specs=[row(D_MODEL), row(D_MODEL), col, col, row(LANES), const((N_EXPERTS, LANES))],
        out_shape=[jax.ShapeDtypeStruct((t, D_MODEL), F32), jax.ShapeDtypeStruct((t, D_MODEL), F32),
                   jax.ShapeDtypeStruct((SUBLANES, t), I32), jax.ShapeDtypeStruct((SUBLANES, t), I32),
                   jax.ShapeDtypeStruct((t, LANES), F32), jax.ShapeDtypeStruct((N_EXPERTS, LANES), F32)],
        scratch_shapes=[pltpu.VMEM((N_EXPERTS, LANES), F32)],
        compiler_params=_cparams(("arbitrary",)),
        name="merge_router",
    )(x, attn, y_ssm_t, g_attn, g_ssm, w_attn_up, w_ssm_glu, w_out, norm2_g, w_router_pad, b_router_col,
      strict_upper, cnt_in)


def _dispatch_kernel(tok, pos_ref, x_ref, xs_in, xs_out, sem):
    del xs_in

    for r in range(tok):
        for k in range(TOP_K):
            pltpu.make_async_copy(x_ref.at[pl.ds(r, 1)], xs_out.at[pl.ds(pos_ref[k, r], 1)], sem).start()
    for _ in range(TOP_K):
        pltpu.make_async_copy(x_ref, xs_out.at[pl.ds(0, tok)], sem).wait()


def _dispatch(pos, xn, xs, tok):
    t, d_model = xn.shape
    return pl.pallas_call(
        functools.partial(_dispatch_kernel, tok),
        grid=(t // tok,),
        in_specs=[pl.BlockSpec((SUBLANES, tok), lambda i: (0, i), memory_space=pltpu.SMEM),
                  pl.BlockSpec((tok, d_model), lambda i: (i, 0)), pl.BlockSpec(memory_space=pl.ANY)],
        out_specs=pl.BlockSpec(memory_space=pl.ANY),
        out_shape=jax.ShapeDtypeStruct(xs.shape, xs.dtype),
        scratch_shapes=[pltpu.SemaphoreType.DMA(())],
        input_output_aliases={2: 0},
        compiler_params=_cparams(("arbitrary",)),
        name="moe_dispatch",
    )(pos, xn, xs)


def _moe_group_kernel(te_ref, nu_ref, x_ref, wg_ref, bg_ref, wu_ref, bu_ref, wd_ref, bd_ref, y_ref,
                      wgb_ref, wub_ref, wdb_ref):
    i = pl.program_id(0)

    @pl.when(i < nu_ref[0])
    def _():
        @pl.when((i == 0) | (te_ref[i] != te_ref[jnp.maximum(i - 1, 0)]))
        def _():
            wgb_ref[...] = wg_ref[0].astype(BF16)
            wub_ref[...] = wu_ref[0].astype(BF16)
            wdb_ref[...] = wd_ref[0].astype(BF16)

        x = x_ref[...].astype(BF16)
        d_ff = wgb_ref.shape[1]
        for c in range(d_ff // FF_CHUNK):
            cols = slice(c * FF_CHUNK, (c + 1) * FF_CHUNK)
            gate = _dot(x, wgb_ref[:, cols]) + bg_ref[0, :, cols]
            up = _dot(x, wub_ref[:, cols]) + bu_ref[0, :, cols]
            gate = jnp.minimum(gate, SWIGLU_LIMIT)
            up = jnp.clip(up, -SWIGLU_LIMIT, SWIGLU_LIMIT)
            hid = (up + 1.0) * (gate * jax.nn.sigmoid(SWIGLU_ALPHA * gate))
            part = _dot(hid.astype(BF16), wdb_ref[cols, :])
            if c == 0:
                y_ref[...] = part + bd_ref[0]
            else:
                y_ref[...] += part

    @pl.when(i >= nu_ref[0])
    def _():
        y_ref[...] = jnp.zeros(y_ref.shape, F32)


def _moe_grouped(tile_expert, n_used, xs, w_gate, b_gate, w_up, b_up, w_down, b_down):
    n_exp, d_model, d_ff = w_gate.shape
    n_tiles = xs.shape[0] // MOE_TM
    wspec = lambda r, c: pl.BlockSpec((1, r, c), lambda i, te, nu: (te[i], 0, 0))
    grid_spec = pltpu.PrefetchScalarGridSpec(
        num_scalar_prefetch=2,
        grid=(n_tiles,),
        in_specs=[pl.BlockSpec((MOE_TM, d_model), lambda i, te, nu: (jnp.minimum(i, nu[0] - 1), 0)),
                  wspec(d_model, d_ff), wspec(1, d_ff), wspec(d_model, d_ff), wspec(1, d_ff),
                  wspec(d_ff, d_model), wspec(1, d_model)],
        out_specs=pl.BlockSpec((MOE_TM, d_model), lambda i, te, nu: (i, 0)),
        scratch_shapes=[pltpu.VMEM((d_model, d_ff), BF16), pltpu.VMEM((d_model, d_ff), BF16),
                        pltpu.VMEM((d_ff, d_model), BF16)],
    )
    return pl.pallas_call(
        _moe_group_kernel,
        grid_spec=grid_spec,
        out_shape=jax.ShapeDtypeStruct((xs.shape[0], d_model), F32),
        compiler_params=_cparams(("arbitrary",)),
        name="moe_grouped",
    )(tile_expert, n_used, xs, w_gate, b_gate.reshape(n_exp, 1, d_ff), w_up, b_up.reshape(n_exp, 1, d_ff),
      w_down, b_down.reshape(n_exp, 1, d_model))


def _combine_kernel(tok, pos_ref, posn_ref, h_ref, w_ref, ys_hbm, o_ref, buf, sem):
    i = pl.program_id(0)
    n = pl.num_programs(0)
    slot = i % 2

    def gather(p_ref, s):
        for r in range(tok):
            for k in range(TOP_K):
                pltpu.make_async_copy(ys_hbm.at[pl.ds(p_ref[k, r], 1)], buf.at[s, k, pl.ds(r, 1)],
                                      sem.at[s]).start()

    @pl.when(i == 0)
    def _():
        gather(pos_ref, 0)

    @pl.when(i + 1 < n)
    def _():
        gather(posn_ref, 1 - slot)

    for k in range(TOP_K):
        pltpu.make_async_copy(ys_hbm.at[pl.ds(0, tok)], buf.at[slot, k], sem.at[slot]).wait()
    w = w_ref[...]
    out = h_ref[...]
    for k in range(TOP_K):
        out = out + w[:, k:k + 1] * buf[slot, k]
    o_ref[...] = out


def _combine(pos, h, w_rows, ys, tok):
    t, d_model = h.shape
    n = t // tok
    smem = lambda imap: pl.BlockSpec((SUBLANES, tok), imap, memory_space=pltpu.SMEM)
    row = lambda w: pl.BlockSpec((tok, w), lambda i: (i, 0))
    return pl.pallas_call(
        functools.partial(_combine_kernel, tok),
        grid=(n,),
        in_specs=[smem(lambda i: (0, i)), smem(lambda i: (0, jnp.minimum(i + 1, n - 1))),
                  row(d_model), row(LANES), pl.BlockSpec(memory_space=pl.ANY)],
        out_specs=row(d_model),
        out_shape=jax.ShapeDtypeStruct((t, d_model), F32),
        scratch_shapes=[pltpu.VMEM((2, TOP_K, tok, d_model), F32), pltpu.SemaphoreType.DMA((2,))],
        compiler_params=_cparams(("arbitrary",)),
        name="moe_combine",
    )(pos, pos, h, w_rows, ys)


def _routing_tables(counts, n_tiles):
    padded = (counts + MOE_TM - 1) // MOE_TM * MOE_TM
    ends = jnp.cumsum(padded)
    starts = ends - padded
    tile_row = jnp.arange(n_tiles, dtype=I32) * MOE_TM
    tile_expert = jnp.minimum(jnp.sum((tile_row[:, None] >= ends[None, :]).astype(I32), axis=1), N_EXPERTS - 1)
    n_used = jnp.maximum(ends[-1:] // MOE_TM, 1)
    return starts.astype(I32), tile_expert.astype(I32), n_used.astype(I32)


def _rope_tables(positions):
    half = HEAD_DIM // 2
    inv_freq = ROPE_THETA ** (-jnp.arange(half, dtype=F32) / half)
    ang = positions.astype(F32)[:, None] * inv_freq[None, :]
    cos = jnp.cos(ang)
    sin = jnp.sin(ang)
    cos_h = jnp.concatenate([cos, cos], axis=1)
    sin_h = jnp.concatenate([-sin, sin], axis=1)
    return jnp.tile(cos_h, (1, N_HEADS)), jnp.tile(sin_h, (1, N_HEADS))


def kernel(x_prompt, x_sample, cache_k, cache_v, state_ssm_re, state_ssm_im, page_table, norm1_g, w_in,
           q_norm_g, k_norm_g, ssm_a_re, ssm_a_im, ssm_log_dt, ssm_b_re, ssm_b_im, ssm_c_re, ssm_c_im,
           ssm_d, w_attn_up, w_ssm_glu, w_out, norm2_g, w_router, b_router, w_gate, b_gate, w_up, b_up,
           w_down, b_down):
    batch, seq, d_model = x_prompt.shape
    dec_batch = x_sample.shape[0]
    n_pages = page_table.shape[1]
    page = cache_k.shape[2]
    past_len = n_pages * page
    assert x_sample.shape[1] == 1 and w_in.shape[0] == 1
    assert seq % ROW_TILE == 0 and seq % MOBA_BLOCK == 0 and past_len % MOBA_BLOCK == 0
    assert page == LANES, "the sample attention keeps one page per lane tile"
    t_prompt = batch * seq
    heads = (N_HEADS, HEAD_DIM)

    xp = x_prompt.reshape(t_prompt, d_model)
    xs = x_sample.reshape(dec_batch, d_model)
    hd, hdt = _head_indicator()
    w_in_b = w_in[0].astype(BF16)
    q_gain = jnp.tile(q_norm_g, (1, N_HEADS))
    k_gain = jnp.tile(k_norm_g, (1, N_HEADS))
    cos_p, sin_p = _rope_tables(jnp.arange(seq, dtype=I32))
    cos_s, sin_s = _rope_tables(jnp.full((dec_batch,), past_len, I32))
    k_p, _, ga_p, gs_p, qt_p, kt_p, vt_p, ut_p = _inproj(xp, norm1_g, w_in_b, cos_p, sin_p, q_gain, k_gain,
                                                         hd, hdt, ROW_TILE)
    _, u_s, ga_s, gs_s, qt_s, kt_s, vt_s, _ = _inproj(xs, norm1_g, w_in_b, cos_s, sin_s, q_gain, k_gain,
                                                      hd, hdt, dec_batch)

    attn_p = _attn_prompt(qt_p, k_p, vt_p, batch, seq)
    feature_major = lambda c: c[0].transpose(0, 2, 3, 1).reshape(c.shape[1], ATTN_W, page)
    attn_s = _attn_sample(page_table, qt_s[0], kt_s[0], vt_s[0], feature_major(cache_k),
                          feature_major(cache_v))

    assert seq % S5_CHUNK == 0
    lam_re, lam_im, dt, ab_re, ab_im, bb_re, bb_im = _s5_discretize(
        ssm_a_re[0], ssm_a_im[0], ssm_log_dt[0], ssm_b_re[0], ssm_b_im[0])
    lags, w1_re, w1_im, w2_re, w2_im, al_re, al_im = _s5_chunk_operators(
        lam_re, lam_im, dt, bb_re, bb_im, ssm_c_re[0], ssm_c_im[0])
    d_rep = jnp.broadcast_to(ssm_d[0][:, None, None], (SSM_W, batch, S5_CHUNK))
    y_t, hp_re, hp_im = _s5_prompt(ut_p.transpose(1, 0, 2), lags, w1_re, w1_im, w2_re, w2_im, al_re, al_im,
                                   d_rep)
    yt_p = y_t.transpose(1, 0, 2)
    width = N_GROUPS * STATE_DIM
    y_s, hs_re, hs_im = _s5_sample(
        u_s, state_ssm_re[0].reshape(dec_batch, width), state_ssm_im[0].reshape(dec_batch, width),
        _block_diag(bb_re.transpose(0, 2, 1)).astype(BF16), _block_diag(bb_im.transpose(0, 2, 1)).astype(BF16),
        _block_diag(ssm_c_re[0].transpose(0, 2, 1)).astype(BF16),
        _block_diag(ssm_c_im[0].transpose(0, 2, 1)).astype(BF16),
        ab_re.reshape(1, width), ab_im.reshape(1, width), ssm_d)

    w_router_pad = jnp.concatenate([w_router[0], jnp.zeros((d_model, LANES - N_EXPERTS), F32)], axis=1)
    merge_w = (w_attn_up[0].astype(BF16), w_ssm_glu[0].astype(BF16), w_out[0].astype(BF16), norm2_g,
               w_router_pad, b_router[0].reshape(N_EXPERTS, 1))
    h_p, xn_p, idx_p, rank_p, wr_p, cnt_p = _merge(xp, attn_p, yt_p, ga_p, gs_p, *merge_w,
                                                   jnp.zeros((N_EXPERTS, LANES), F32), ROW_TILE)
    h_s, xn_s, idx_s, rank_s, wr_s, cnt_all = _merge(xs, attn_s, y_s.T[None], ga_s, gs_s, *merge_w, cnt_p,
                                                     dec_batch)

    n_pairs = (t_prompt + dec_batch) * TOP_K
    n_tiles = n_pairs // MOE_TM + N_EXPERTS
    starts, tile_expert, n_used = _routing_tables(cnt_all[:, 0].astype(I32), n_tiles)
    pos_p, pos_s = rank_p, rank_s
    for e in range(N_EXPERTS):
        pos_p = pos_p + jnp.where(idx_p == e, starts[e], 0)
        pos_s = pos_s + jnp.where(idx_s == e, starts[e], 0)
    x_sorted = jnp.zeros((n_tiles * MOE_TM, d_model), F32)
    x_sorted = _dispatch(pos_p, xn_p, x_sorted, MOE_TOK)
    x_sorted = _dispatch(pos_s, xn_s, x_sorted, dec_batch)
    y_sorted = _moe_grouped(tile_expert, n_used, x_sorted, w_gate[0], b_gate[0], w_up[0], b_up[0],
                            w_down[0], b_down[0])
    out_p = _combine(pos_p, h_p, wr_p, y_sorted, MOE_TOK)
    out_s = _combine(pos_s, h_s, wr_s, y_sorted, dec_batch)

    per_seq = lambda a, n, s: a.reshape((1, n) + heads + (s,)).transpose(0, 1, 4, 2, 3)
    new_rows = lambda a: per_seq(a, 1, dec_batch).transpose(0, 2, 1, 3, 4)
    return (out_p.reshape(batch, seq, d_model),
            out_s.reshape(dec_batch, 1, d_model),
            per_seq(kt_p, batch, seq),
            per_seq(vt_p, batch, seq),
            hp_re.transpose(1, 0, 2)[None],
            hp_im.transpose(1, 0, 2)[None],
            new_rows(kt_s),
            new_rows(vt_s),
            hs_re.reshape(1, dec_batch, N_GROUPS, STATE_DIM),
            hs_im.reshape(1, dec_batch, N_GROUPS, STATE_DIM))
```

```python
import functools
import math

import jax
import jax.numpy as jnp
from jax import lax
from jax.experimental import pallas as pl
from jax.experimental.pallas import tpu as pltpu

F32 = jnp.float32
BF16 = jnp.bfloat16
I32 = jnp.int32

D_MODEL = 1024
N_HEADS = 8
HEAD_DIM = 64
ATTN_W = N_HEADS * HEAD_DIM
MOBA_BLOCK = 256
MOBA_TOP = 3
ROPE_THETA = 10000.0
SSM_W = 512
GROUP_CH = 16
N_GROUPS = 32
STATE_DIM = 64
N_EXPERTS = 32
TOP_K = 4
SWIGLU_ALPHA = 1.702
SWIGLU_LIMIT = 7.0
EPS = 1e-6
IN_W = 4096
NEG_BIG = -1e9

LANES = 128
SUBLANES = 8
ROW_TILE = 512
S5_CHUNK = LANES
MOE_TM = 512
MOE_TOK = 256
FF_CHUNK = 512
VMEM_LIMIT = 56 * 1024 * 1024


def _cparams(sem):
    return pltpu.CompilerParams(dimension_semantics=sem, vmem_limit_bytes=VMEM_LIMIT)


def _dot(a, b):
    return jnp.dot(a, b, preferred_element_type=F32)


def _split_hi_lo(a):
    hi = a.astype(BF16)
    lo = (a - hi.astype(F32)).astype(BF16)
    return hi, lo


def _dot_hilo(a, b_bf16):
    hi, lo = _split_hi_lo(a)
    return _dot(hi, b_bf16) + _dot(lo, b_bf16)


def _head_indicator():
    c = jnp.arange(ATTN_W)[:, None] // HEAD_DIM
    h = jnp.arange(LANES)[None, :]
    hd = (c == h).astype(BF16)
    return hd, hd.T


def _head_norm_rope(t, gain, cos, sin_signed, hd, hdt):
    ssq = _dot_hilo(t * t, hd)
    rstd = lax.rsqrt(ssq * (1.0 / HEAD_DIM) + EPS)
    t = t * _dot_hilo(rstd, hdt) * gain
    lane = lax.broadcasted_iota(I32, (t.shape[0], LANES), 1)
    first_half = (lane % HEAD_DIM) < (HEAD_DIM // 2)
    outs = []
    for s in range(ATTN_W // LANES):
        ts = t[:, s * LANES:(s + 1) * LANES]
        partner = jnp.where(first_half,
                            pltpu.roll(ts, LANES - HEAD_DIM // 2, axis=1),
                            pltpu.roll(ts, HEAD_DIM // 2, axis=1))
        outs.append(ts * cos[:, s * LANES:(s + 1) * LANES]
                    + partner * sin_signed[:, s * LANES:(s + 1) * LANES])
    return jnp.concatenate(outs, axis=1)


def _inproj_kernel(x_ref, g1_ref, w_ref, cos_ref, sin_ref, qg_ref, kg_ref, hd_ref, hdt_ref,
                   k_ref, u_ref, ga_ref, gs_ref, qt_ref, kt_ref, vt_ref, ut_ref):
    x = x_ref[...]
    xn = x * lax.rsqrt(jnp.mean(x * x, axis=-1, keepdims=True) + EPS) * g1_ref[...]
    xb = xn.astype(BF16)
    hd = hd_ref[...]
    hdt = hdt_ref[...]
    cos = cos_ref[...]
    sin = sin_ref[...]
    q = _dot(xb, w_ref[:, 0:ATTN_W])
    qt_ref[0] = _head_norm_rope(q, qg_ref[...], cos, sin, hd, hdt).T
    k = _head_norm_rope(_dot(xb, w_ref[:, ATTN_W:2 * ATTN_W]), kg_ref[...], cos, sin, hd, hdt)
    k_ref[...] = k
    kt_ref[0] = k.T
    vt_ref[0] = _dot(xb, w_ref[:, 2 * ATTN_W:3 * ATTN_W]).T
    u = _dot(xb, w_ref[:, 3 * ATTN_W:3 * ATTN_W + SSM_W])
    u_ref[...] = u
    ut_ref[0] = u.T
    ga_ref[...] = _dot(xb, w_ref[:, 2048:3072])
    gs_ref[...] = _dot(xb, w_ref[:, 3072:4096])


def _inproj(x, norm1_g, w_in_bf16, cos_tab, sin_tab, q_gain, k_gain, hd, hdt, tile):
    t = x.shape[0]
    period = cos_tab.shape[0] // tile
    row = lambda w: pl.BlockSpec((tile, w), lambda i: (i, 0))
    const = lambda shape: pl.BlockSpec(shape, lambda i: (0, 0))
    pos = pl.BlockSpec((tile, ATTN_W), lambda i: (i % period, 0))
    feature_major = pl.BlockSpec((1, ATTN_W, tile), lambda i: (i // period, 0, i % period))
    out_shape = ([jax.ShapeDtypeStruct((t, ATTN_W), F32)] * 2
                 + [jax.ShapeDtypeStruct((t, D_MODEL), F32)] * 2
                 + [jax.ShapeDtypeStruct((t // (period * tile), ATTN_W, period * tile), F32)] * 4)
    return pl.pallas_call(
        _inproj_kernel,
        grid=(t // tile,),
        in_specs=[row(D_MODEL), const((1, D_MODEL)), const((D_MODEL, IN_W)), pos, pos,
                  const((1, ATTN_W)), const((1, ATTN_W)), const((ATTN_W, LANES)), const((LANES, ATTN_W))],
        out_specs=[row(ATTN_W)] * 2 + [row(D_MODEL)] * 2 + [feature_major] * 4,
        out_shape=out_shape,
        compiler_params=_cparams(("parallel",)),
        name="inproj",
    )(x, norm1_g, w_in_bf16, cos_tab, sin_tab, q_gain, k_gain, hd, hdt)


def _block_penalty_t(gate_t, n_past, own_blk):
    blk = lax.broadcasted_iota(I32, gate_t.shape, 0)
    valid = blk < n_past
    g = jnp.where(valid, gate_t, -jnp.inf)
    cnt = jnp.zeros(gate_t.shape, I32)
    for m in range(gate_t.shape[0]):
        gm = g[m:m + 1, :]
        beats = jnp.where(gm > g, 1, jnp.where((gm == g) & (blk > m), 1, 0))
        cnt = cnt + beats
    keep = (valid & (cnt < MOBA_TOP)) | (blk == own_blk)
    return jnp.where(keep, 0.0, NEG_BIG)


def _attn_prompt_kernel(qt_ref, k_ref, vt_ref, o_ref, kaug_ref, vtb_ref, kmean_ref):
    j = pl.program_id(2)
    blk = MOBA_BLOCK
    nblk = k_ref.shape[0] // blk

    @pl.when(j == 0)
    def _():
        lane = lax.broadcasted_iota(I32, (blk, LANES), 1)
        for n in range(nblk):
            k = k_ref[n * blk:(n + 1) * blk, :]
            kaug_ref[n * blk:(n + 1) * blk, 0:LANES] = k.astype(BF16)
            kaug_ref[n * blk:(n + 1) * blk, LANES:2 * LANES] = jnp.where(lane == n, 1.0, 0.0).astype(BF16)
            kmean_ref[n:n + 1, :] = jnp.mean(k, axis=0, keepdims=True)
        vtb_ref[...] = vt_ref[0].astype(BF16)

    q_t = qt_ref[0]
    feat = lax.broadcasted_iota(I32, q_t.shape, 0)
    km_hi, km_lo = _split_hi_lo(kmean_ref[...])
    zpad = jnp.zeros((LANES - nblk, blk), F32)
    qaug = []
    for h in range(2):
        qh = jnp.where(feat >= HEAD_DIM if h == 1 else feat < HEAD_DIM, q_t, 0.0)
        q_hi, q_lo = _split_hi_lo(qh)
        gate_t = _dot(km_hi, q_hi) + _dot(km_hi, q_lo) + _dot(km_lo, q_hi)
        pen_t = jnp.concatenate([_block_penalty_t(gate_t, j, j), zpad], axis=0)
        qaug.append(jnp.concatenate([(qh * (1.0 / math.sqrt(HEAD_DIM))).astype(BF16),
                                     pen_t.astype(BF16)], axis=0))

    key_i = lax.broadcasted_iota(I32, (blk, blk), 0)
    qry_i = lax.broadcasted_iota(I32, (blk, blk), 1)
    future = key_i > qry_i

    def attend(jq):
        past = jq * blk
        outs = []
        for h in range(2):
            s_diag = jnp.where(future, NEG_BIG, _dot(kaug_ref[past:past + blk, :], qaug[h]))
            m = jnp.max(s_diag, axis=0, keepdims=True)
            if jq > 0:
                s_past = _dot(kaug_ref[0:past, :], qaug[h])
                m = jnp.maximum(m, jnp.max(s_past, axis=0, keepdims=True))
                p_past = jnp.exp(s_past - m)
            p_diag = jnp.exp(s_diag - m)
            l = jnp.sum(p_diag, axis=0, keepdims=True)
            feats = slice(h * HEAD_DIM, (h + 1) * HEAD_DIM)
            pv = _dot(vtb_ref[feats, past:past + blk], p_diag.astype(BF16))
            if jq > 0:
                l = l + jnp.sum(p_past, axis=0, keepdims=True)
                pv = pv + _dot(vtb_ref[feats, 0:past], p_past.astype(BF16))
            outs.append(pv / l)
        o_ref[...] = jnp.concatenate(outs, axis=0).T

    for jq in range(nblk):
        pl.when(j == jq)(functools.partial(attend, jq))


def _attn_prompt(qt, k, vt, batch, seq):
    nq = seq // MOBA_BLOCK
    return pl.pallas_call(
        _attn_prompt_kernel,
        grid=(batch, ATTN_W // LANES, nq),
        in_specs=[pl.BlockSpec((1, LANES, MOBA_BLOCK), lambda b, hp, j: (b, hp, j)),
                  pl.BlockSpec((seq, LANES), lambda b, hp, j: (b, hp)),
                  pl.BlockSpec((1, LANES, seq), lambda b, hp, j: (b, hp, 0))],
        out_specs=pl.BlockSpec((MOBA_BLOCK, LANES), lambda b, hp, j: (b * nq + j, hp)),
        out_shape=jax.ShapeDtypeStruct((batch * seq, ATTN_W), F32),
        scratch_shapes=[pltpu.VMEM((seq, 2 * LANES), BF16), pltpu.VMEM((LANES, seq), BF16),
                        pltpu.VMEM((nq, LANES), F32)],
        compiler_params=_cparams(("parallel", "parallel", "arbitrary")),
        name="attn_prompt",
    )(qt, k, vt)


def _attn_sample_kernel(n_pages, pt_ref, qt_ref, kt_ref, vt_ref, *rest):
    del pt_ref
    k_pages = rest[:n_pages]
    v_pages = rest[n_pages:2 * n_pages]
    o_ref = rest[2 * n_pages]
    b = pl.program_id(0)
    page = k_pages[0].shape[2]
    ppb = MOBA_BLOCK // page
    nblk = n_pages // ppb
    seq_i = lax.broadcasted_iota(I32, (qt_ref.shape[1], page), 0)
    lane_i = lax.broadcasted_iota(I32, (qt_ref.shape[1], page), 1)
    pick_all = jnp.where(seq_i == b, 1.0, 0.0).astype(BF16)
    pick_lane0 = jnp.where((seq_i == b) & (lane_i == 0), 1.0, 0.0).astype(BF16)
    qb = _dot_hilo(qt_ref[...], pick_all) * (1.0 / math.sqrt(HEAD_DIM))
    k_new = _dot_hilo(kt_ref[...], pick_lane0)
    v_new = _dot_hilo(vt_ref[...], pick_lane0)

    def head_sums(x):
        return jnp.sum(x.reshape(N_HEADS, HEAD_DIM, x.shape[1]), axis=1)

    def per_feature(p):
        return jnp.broadcast_to(p[:, None, :], (N_HEADS, HEAD_DIM, p.shape[1])).reshape(ATTN_W, p.shape[1])

    s_pages = [head_sums(k_pages[p][0] * qb) for p in range(n_pages)]
    lane = lax.broadcasted_iota(I32, s_pages[0].shape, 1)
    s_new = jnp.where(lane == 0, head_sums(k_new * qb), NEG_BIG)

    gates = []
    for n in range(nblk):
        blk_sum = s_pages[n * ppb]
        for i in range(1, ppb):
            blk_sum = blk_sum + s_pages[n * ppb + i]
        gates.append(jnp.sum(blk_sum, axis=1, keepdims=True))
    pens = []
    for n in range(nblk):
        cnt = jnp.zeros(gates[n].shape, I32)
        for m in range(nblk):
            if m == n:
                continue
            beats = (gates[m] >= gates[n]) if m < n else (gates[m] > gates[n])
            cnt = cnt + jnp.where(beats, 1, 0)
        pens.append(jnp.where(cnt < MOBA_TOP, 0.0, NEG_BIG))

    s_adj = [s_pages[p] + pens[p // ppb] for p in range(n_pages)]
    m = s_new
    for s in s_adj:
        m = jnp.maximum(m, s)
    m = jnp.max(m, axis=1, keepdims=True)
    e_new = jnp.exp(s_new - m)
    es = [jnp.exp(s - m) for s in s_adj]
    total = e_new
    for e in es:
        total = total + e
    inv = 1.0 / jnp.sum(total, axis=1, keepdims=True)
    acc = v_new * per_feature(e_new * inv)
    for p in range(n_pages):
        acc = acc + v_pages[p][0] * per_feature(es[p] * inv)
    a_hi, a_lo = _split_hi_lo(acc)
    ones = jnp.ones((SUBLANES, page), BF16)
    nt = (((1,), (1,)), ((), ()))
    row = (lax.dot_general(ones, a_hi, nt, preferred_element_type=F32)
           + lax.dot_general(ones, a_lo, nt, preferred_element_type=F32))
    o_ref[pl.ds(b, 1), :] = row[0:1, :]


def _attn_sample(page_table, qt, kt, vt, cache_kt, cache_vt):
    dec_batch, n_pages = page_table.shape
    page = cache_kt.shape[2]

    def page_spec(p):
        return pl.BlockSpec((1, ATTN_W, page), lambda b, pt: (pt[b, p], 0, 0))

    own = pl.BlockSpec((ATTN_W, dec_batch), lambda b, pt: (0, 0))
    grid_spec = pltpu.PrefetchScalarGridSpec(
        num_scalar_prefetch=1,
        grid=(dec_batch,),
        in_specs=[own] * 3 + [page_spec(p) for p in range(n_pages)] * 2,
        out_specs=pl.BlockSpec((dec_batch, ATTN_W), lambda b, pt: (0, 0)),
    )
    return pl.pallas_call(
        functools.partial(_attn_sample_kernel, n_pages),
        grid_spec=grid_spec,
        out_shape=jax.ShapeDtypeStruct((dec_batch, ATTN_W), F32),
        compiler_params=_cparams(("arbitrary",)),
        name="attn_sample",
    )(page_table, qt, kt, vt, *([cache_kt] * n_pages), *([cache_vt] * n_pages))


def _s5_discretize(a_re, a_im, log_dt, b_re, b_im):
    lam_re = a_re.astype(F32)
    lam_im = a_im.astype(F32)
    dt = jnp.exp(log_dt.astype(F32))[:, None]
    mag = jnp.exp(lam_re * dt)
    ab_re = mag * jnp.cos(lam_im * dt)
    ab_im = mag * jnp.sin(lam_im * dt)
    den = lam_re * lam_re + lam_im * lam_im
    n_re, n_im = ab_re - 1.0, ab_im
    f_re = (n_re * lam_re + n_im * lam_im) / den
    f_im = (n_im * lam_re - n_re * lam_im) / den
    br, bi = b_re.astype(F32), b_im.astype(F32)
    bb_re = f_re[..., None] * br - f_im[..., None] * bi
    bb_im = f_re[..., None] * bi + f_im[..., None] * br
    return lam_re, lam_im, dt, ab_re, ab_im, bb_re, bb_im


def _s5_chunk_operators(lam_re, lam_im, dt, bb_re, bb_im, c_re, c_im):
    L = S5_CHUNK
    hp = lax.Precision.HIGHEST
    steps = jnp.arange(L + 1, dtype=F32)[None, None, :]
    rate = dt[:, :, None] * steps
    pmag = jnp.exp(lam_re[:, :, None] * rate)
    pw_re = pmag * jnp.cos(lam_im[:, :, None] * rate)
    pw_im = pmag * jnp.sin(lam_im[:, :, None] * rate)
    cr, ci = c_re.astype(F32), c_im.astype(F32)
    btr, bti = bb_re.transpose(0, 2, 1), bb_im.transpose(0, 2, 1)
    bc_re = btr[:, :, None, :] * cr[:, None, :, :] - bti[:, :, None, :] * ci[:, None, :, :]
    bc_im = btr[:, :, None, :] * ci[:, None, :, :] + bti[:, :, None, :] * cr[:, None, :, :]
    pairs = GROUP_CH * GROUP_CH
    lags = (jnp.einsum('gxp,gpj->gxj', bc_re.reshape(N_GROUPS, pairs, STATE_DIM), pw_re[:, :, :L], precision=hp)
            - jnp.einsum('gxp,gpj->gxj', bc_im.reshape(N_GROUPS, pairs, STATE_DIM), pw_im[:, :, :L],
                         precision=hp))
    rev_re = pw_re[:, :, L - 1::-1].transpose(0, 2, 1)
    rev_im = pw_im[:, :, L - 1::-1].transpose(0, 2, 1)
    w1_re = rev_re[:, None, :, :] * btr[:, :, None, :] - rev_im[:, None, :, :] * bti[:, :, None, :]
    w1_im = rev_re[:, None, :, :] * bti[:, :, None, :] + rev_im[:, None, :, :] * btr[:, :, None, :]
    w1_re = w1_re.reshape(N_GROUPS, GROUP_CH * L, STATE_DIM)
    w1_im = w1_im.reshape(N_GROUPS, GROUP_CH * L, STATE_DIM)
    ctr, cti = cr.transpose(0, 2, 1), ci.transpose(0, 2, 1)
    nx_re, nx_im = pw_re[:, :, 1:], pw_im[:, :, 1:]
    w2_re = ctr[:, :, :, None] * nx_re[:, :, None, :] - cti[:, :, :, None] * nx_im[:, :, None, :]
    w2_im = -(ctr[:, :, :, None] * nx_im[:, :, None, :] + cti[:, :, :, None] * nx_re[:, :, None, :])
    w2_re = w2_re.reshape(N_GROUPS, STATE_DIM, GROUP_CH * L)
    w2_im = w2_im.reshape(N_GROUPS, STATE_DIM, GROUP_CH * L)
    return (lags, w1_re.astype(BF16), w1_im.astype(BF16), w2_re.astype(BF16), w2_im.astype(BF16),
            pw_re[:, None, :, L], pw_im[:, None, :, L])


def _s5_prompt_kernel(x_ref, lag_ref, w1r_ref, w1i_ref, w2r_ref, w2i_ref, ar_ref, ai_ref, d_ref,
                      y_ref, hr_ref, hi_ref, toep_ref, u_ref, pre_ref, pim_ref):
    nch, batch, seq = x_ref.shape
    L = S5_CHUNK
    n_chunks = seq // L

    s_i = lax.broadcasted_iota(I32, (L, L), 0)
    t_i = lax.broadcasted_iota(I32, (L, L), 1)
    causal = t_i >= s_i
    for cp in range(nch):
        for c in range(nch):
            first = jnp.broadcast_to(lag_ref[0, cp * nch + c:cp * nch + c + 1, :], (L, L))
            tile = pltpu.roll(first, 0, 1, stride=1, stride_axis=0)
            toep_ref[cp * L:(cp + 1) * L, c * L:(c + 1) * L] = jnp.where(causal, tile, 0.0).astype(BF16)

    for k in range(n_chunks):
        for cp in range(nch):
            u_ref[k * batch:(k + 1) * batch, cp * L:(cp + 1) * L] = x_ref[cp, :, k * L:(k + 1) * L]
    ub = u_ref[...].astype(BF16)

    d_re = _dot(ub, w1r_ref[0])
    d_im = _dot(ub, w1i_ref[0])
    a_re = ar_ref[0]
    a_im = ai_ref[0]
    h_re = jnp.zeros((batch, STATE_DIM), F32)
    h_im = jnp.zeros((batch, STATE_DIM), F32)
    for k in range(n_chunks):
        rows = slice(k * batch, (k + 1) * batch)
        pre_ref[rows, :] = h_re
        pim_ref[rows, :] = h_im
        h_re, h_im = (a_re * h_re - a_im * h_im + d_re[rows, :], a_re * h_im + a_im * h_re + d_im[rows, :])
    hr_ref[0] = h_re
    hi_ref[0] = h_im

    y = (_dot(ub, toep_ref[...]) + _dot(pre_ref[...].astype(BF16), w2r_ref[0])
         + _dot(pim_ref[...].astype(BF16), w2i_ref[0]))
    for k in range(n_chunks):
        for c in range(nch):
            lanes = slice(k * L, (k + 1) * L)
            y_ref[c, :, lanes] = (y[k * batch:(k + 1) * batch, c * L:(c + 1) * L]
                                  + d_ref[c] * x_ref[c, :, lanes])


def _s5_prompt(x, lags, w1_re, w1_im, w2_re, w2_im, al_re, al_im, d_rep):
    width, batch, seq = x.shape
    g = width // GROUP_CH
    cols = GROUP_CH * S5_CHUNK
    rows = seq // S5_CHUNK * batch
    blk = lambda s: pl.BlockSpec((1,) + s, lambda i: (i, 0, 0))
    chan = pl.BlockSpec((GROUP_CH, batch, seq), lambda i: (i, 0, 0))
    return pl.pallas_call(
        _s5_prompt_kernel,
        grid=(g,),
        in_specs=[chan, blk((GROUP_CH * GROUP_CH, S5_CHUNK)), blk((cols, STATE_DIM)), blk((cols, STATE_DIM)),
                  blk((STATE_DIM, cols)), blk((STATE_DIM, cols)), blk((1, STATE_DIM)), blk((1, STATE_DIM)),
                  pl.BlockSpec((GROUP_CH, batch, S5_CHUNK), lambda i: (i, 0, 0))],
        out_specs=[chan, blk((batch, STATE_DIM)), blk((batch, STATE_DIM))],
        out_shape=[jax.ShapeDtypeStruct((width, batch, seq), F32),
                   jax.ShapeDtypeStruct((g, batch, STATE_DIM), F32),
                   jax.ShapeDtypeStruct((g, batch, STATE_DIM), F32)],
        scratch_shapes=[pltpu.VMEM((cols, cols), BF16), pltpu.VMEM((rows, cols), F32),
                        pltpu.VMEM((rows, STATE_DIM), F32), pltpu.VMEM((rows, STATE_DIM), F32)],
        compiler_params=_cparams(("parallel",)),
        name="s5_prompt",
    )(x, lags, w1_re, w1_im, w2_re, w2_im, al_re, al_im, d_rep)


def _s5_sample_kernel(u_ref, h0r_ref, h0i_ref, bdr_ref, bdi_ref, cdr_ref, cdi_ref, ar_ref, ai_ref, d_ref,
                      y_ref, hr_ref, hi_ref):
    u = u_ref[...]
    ub = u.astype(BF16)
    a_re = ar_ref[...]
    a_im = ai_ref[...]
    h0r = h0r_ref[...]
    h0i = h0i_ref[...]
    h_re = a_re * h0r - a_im * h0i + _dot(ub, bdr_ref[...])
    h_im = a_re * h0i + a_im * h0r + _dot(ub, bdi_ref[...])
    hr_ref[...] = h_re
    hi_ref[...] = h_im
    y_ref[...] = (_dot(h_re.astype(BF16), cdr_ref[...]) - _dot(h_im.astype(BF16), cdi_ref[...])
                  + d_ref[...] * u)


def _s5_sample(u_s, h0_re, h0_im, bd_re, bd_im, cd_re, cd_im, ab_re, ab_im, d_skip):
    n = u_s.shape[0]
    width = N_GROUPS * STATE_DIM
    return pl.pallas_call(
        _s5_sample_kernel,
        out_shape=[jax.ShapeDtypeStruct((n, SSM_W), F32), jax.ShapeDtypeStruct((n, width), F32),
                   jax.ShapeDtypeStruct((n, width), F32)],
        compiler_params=pltpu.CompilerParams(vmem_limit_bytes=VMEM_LIMIT),
        name="s5_sample",
    )(u_s, h0_re, h0_im, bd_re, bd_im, cd_re, cd_im, ab_re, ab_im, d_skip)


def _block_diag(m):
    g, r, c = m.shape
    eye = jnp.eye(g, dtype=m.dtype)
    return (m[:, :, None, :] * eye[:, None, :, None]).reshape(g * r, g * c)


def _merge_kernel(x_ref, attn_ref, yssm_ref, ga_ref, gs_ref, wau_ref, wglu_ref, wout_ref, g2_ref,
                  wr_ref, br_ref, su_ref, cnt_in_ref,
                  h_ref, xn_ref, idx_ref, rank_ref, w_ref, cnt_ref, run_ref):
    @pl.when(pl.program_id(0) == 0)
    def _():
        run_ref[...] = cnt_in_ref[...]

    branch_attn = _dot(attn_ref[...].astype(BF16), wau_ref[...])
    glu = _dot(jax.nn.gelu(yssm_ref[0].T).astype(BF16), wglu_ref[...])
    branch_ssm = glu[:, :D_MODEL] * jax.nn.sigmoid(glu[:, D_MODEL:])
    merged = jax.nn.sigmoid(ga_ref[...]) * branch_attn + jax.nn.sigmoid(gs_ref[...]) * branch_ssm
    h = x_ref[...] + _dot(merged.astype(BF16), wout_ref[...])
    h_ref[...] = h
    xn = h * lax.rsqrt(jnp.mean(h * h, axis=-1, keepdims=True) + EPS) * g2_ref[...]
    xn_ref[...] = xn

    x_hi, x_lo = _split_hi_lo(xn)
    w_hi, w_lo = _split_hi_lo(wr_ref[...])
    logits = _dot(x_hi, w_hi) + _dot(x_lo, w_hi) + _dot(x_hi, w_lo)
    lt = logits.T[0:N_EXPERTS, :] + br_ref[...]
    rows = lt.shape[1]
    e_iota = lax.broadcasted_iota(I32, lt.shape, 0).astype(F32)
    tops, idxs, hots = [], [], []
    for _ in range(TOP_K):
        m = jnp.max(lt, axis=0, keepdims=True)
        idx = jnp.min(jnp.where(lt == m, e_iota, float(N_EXPERTS)), axis=0, keepdims=True)
        hot = e_iota == idx
        tops.append(m)
        idxs.append(idx)
        hots.append(hot)
        lt = jnp.where(hot, -jnp.inf, lt)
    exps = [jnp.exp(t - tops[0]) for t in tops]
    denom = exps[0] + exps[1] + exps[2] + exps[3]

    member = jnp.zeros(lt.shape, F32)
    for hot in hots:
        member = member + jnp.where(hot, 1.0, 0.0)
    before = run_ref[:, 0:1] + _dot(member.astype(BF16), su_ref[...])
    ranks = [jnp.sum(jnp.where(hot, before, 0.0), axis=0, keepdims=True) for hot in hots]
    run_ref[...] = run_ref[...] + jnp.sum(member, axis=1, keepdims=True)
    cnt_ref[...] = run_ref[...]

    zrows = jnp.zeros((SUBLANES - TOP_K, rows), F32)
    idx_ref[...] = jnp.concatenate(idxs + [zrows], axis=0).astype(I32)
    rank_ref[...] = jnp.concatenate(ranks + [zrows], axis=0).astype(I32)
    w_t = jnp.concatenate([e / denom for e in exps] + [jnp.zeros((LANES - TOP_K, rows), F32)], axis=0)
    w_ref[...] = w_t.T


def _merge(x, attn, y_ssm_t, g_attn, g_ssm, w_attn_up, w_ssm_glu, w_out, norm2_g, w_router_pad,
           b_router_col, cnt_in, tile):
    t = x.shape[0]
    period = y_ssm_t.shape[2] // tile
    row = lambda w: pl.BlockSpec((tile, w), lambda i: (i, 0))
    col = pl.BlockSpec((SUBLANES, tile), lambda i: (0, i))
    const = lambda shape: pl.BlockSpec(shape, lambda i: (0, 0))
    feature_major = pl.BlockSpec((1, SSM_W, tile), lambda i: (i // period, 0, i % period))
    strict_upper = jnp.triu(jnp.ones((tile, tile), F32), 1).astype(BF16)
    return pl.pallas_call(
        _merge_kernel,
        grid=(t // tile,),
        in_specs=[row(D_MODEL), row(ATTN_W), feature_major, row(D_MODEL), row(D_MODEL),
                  const((ATTN_W, D_MODEL)), const((SSM_W, 2 * D_MODEL)), const((D_MODEL, D_MODEL)),
                  const((1, D_MODEL)), const((D_MODEL, LANES)), const((N_EXPERTS, 1)),
                  const((tile, tile)), const((N_EXPERTS, LANES))],
        out_specs=[row(D_MODEL), row(D_MODEL), col, col, row(LANES), const((N_EXPERTS, LANES))],
        out_shape=[jax.ShapeDtypeStruct((t, D_MODEL), F32), jax.ShapeDtypeStruct((t, D_MODEL), F32),
                   jax.ShapeDtypeStruct((SUBLANES, t), I32), jax.ShapeDtypeStruct((SUBLANES, t), I32),
                   jax.ShapeDtypeStruct((t, LANES), F32), jax.ShapeDtypeStruct((N_EXPERTS, LANES), F32)],
        scratch_shapes=[pltpu.VMEM((N_EXPERTS, LANES), F32)],
        compiler_params=_cparams(("arbitrary",)),
        name="merge_router",
    )(x, attn, y_ssm_t, g_attn, g_ssm, w_attn_up, w_ssm_glu, w_out, norm2_g, w_router_pad, b_router_col,
      strict_upper, cnt_in)


def _dispatch_kernel(tok, pos_ref, x_ref, xs_in, xs_out, sem):
    del xs_in

    for r in range(tok):
        for k in range(TOP_K):
            pltpu.make_async_copy(x_ref.at[pl.ds(r, 1)], xs_out.at[pl.ds(pos_ref[k, r], 1)],
                                  sem).start(priority=k % 2)
    for _ in range(TOP_K):
        pltpu.make_async_copy(x_ref, xs_out.at[pl.ds(0, tok)], sem).wait()


def _dispatch(pos, xn, xs, tok):
    t, d_model = xn.shape
    return pl.pallas_call(
        functools.partial(_dispatch_kernel, tok),
        grid=(t // tok,),
        in_specs=[pl.BlockSpec((SUBLANES, tok), lambda i: (0, i), memory_space=pltpu.SMEM),
                  pl.BlockSpec((tok, d_model), lambda i: (i, 0)), pl.BlockSpec(memory_space=pl.ANY)],
        out_specs=pl.BlockSpec(memory_space=pl.ANY),
        out_shape=jax.ShapeDtypeStruct(xs.shape, xs.dtype),
        scratch_shapes=[pltpu.SemaphoreType.DMA(())],
        input_output_aliases={2: 0},
        compiler_params=_cparams(("arbitrary",)),
        name="moe_dispatch",
    )(pos, xn, xs)


def _moe_group_kernel(te_ref, nu_ref, x_ref, wg_ref, bg_ref, wu_ref, bu_ref, wd_ref, bd_ref, y_ref,
                      wgb_ref, wub_ref, wdb_ref):
    i = pl.program_id(0)

    @pl.when(i < nu_ref[0])
    def _():
        @pl.when((i == 0) | (te_ref[i] != te_ref[jnp.maximum(i - 1, 0)]))
        def _():
            wgb_ref[...] = wg_ref[0].astype(BF16)
            wub_ref[...] = wu_ref[0].astype(BF16)
            wdb_ref[...] = wd_ref[0].astype(BF16)

        x = x_ref[...].astype(BF16)
        d_ff = wgb_ref.shape[1]
        for c in range(d_ff // FF_CHUNK):
            cols = slice(c * FF_CHUNK, (c + 1) * FF_CHUNK)
            gate = _dot(x, wgb_ref[:, cols]) + bg_ref[0, :, cols]
            up = _dot(x, wub_ref[:, cols]) + bu_ref[0, :, cols]
            gate = jnp.minimum(gate, SWIGLU_LIMIT)
            up = jnp.clip(up, -SWIGLU_LIMIT, SWIGLU_LIMIT)
            hid = (up + 1.0) * (gate * jax.nn.sigmoid(SWIGLU_ALPHA * gate))
            part = _dot(hid.astype(BF16), wdb_ref[cols, :])
            if c == 0:
                y_ref[...] = part + bd_ref[0]
            else:
                y_ref[...] += part

    @pl.when(i >= nu_ref[0])
    def _():
        y_ref[...] = jnp.zeros(y_ref.shape, F32)


def _moe_grouped(tile_expert, n_used, xs, w_gate, b_gate, w_up, b_up, w_down, b_down):
    n_exp, d_model, d_ff = w_gate.shape
    n_tiles = xs.shape[0] // MOE_TM
    wspec = lambda r, c: pl.BlockSpec((1, r, c), lambda i, te, nu: (te[i], 0, 0))
    grid_spec = pltpu.PrefetchScalarGridSpec(
        num_scalar_prefetch=2,
        grid=(n_tiles,),
        in_specs=[pl.BlockSpec((MOE_TM, d_model), lambda i, te, nu: (jnp.minimum(i, nu[0] - 1), 0)),
                  wspec(d_model, d_ff), wspec(1, d_ff), wspec(d_model, d_ff), wspec(1, d_ff),
                  wspec(d_ff, d_model), wspec(1, d_model)],
        out_specs=pl.BlockSpec((MOE_TM, d_model), lambda i, te, nu: (i, 0)),
        scratch_shapes=[pltpu.VMEM((d_model, d_ff), BF16), pltpu.VMEM((d_model, d_ff), BF16),
                        pltpu.VMEM((d_ff, d_model), BF16)],
    )
    return pl.pallas_call(
        _moe_group_kernel,
        grid_spec=grid_spec,
        out_shape=jax.ShapeDtypeStruct((xs.shape[0], d_model), F32),
        compiler_params=_cparams(("arbitrary",)),
        name="moe_grouped",
    )(tile_expert, n_used, xs, w_gate, b_gate.reshape(n_exp, 1, d_ff), w_up, b_up.reshape(n_exp, 1, d_ff),
      w_down, b_down.reshape(n_exp, 1, d_model))


def _combine_kernel(tok, pos_ref, posn_ref, h_ref, w_ref, ys_hbm, o_ref, buf, sem):
    i = pl.program_id(0)
    n = pl.num_programs(0)
    slot = i % 2

    def gather(p_ref, s):
        for r in range(tok):
            for k in range(TOP_K):
                pltpu.make_async_copy(ys_hbm.at[pl.ds(p_ref[k, r], 1)], buf.at[s, k, pl.ds(r, 1)],
                                      sem.at[s]).start(priority=k % 2)

    @pl.when(i == 0)
    def _():
        gather(pos_ref, 0)

    @pl.when(i + 1 < n)
    def _():
        gather(posn_ref, 1 - slot)

    for k in range(TOP_K):
        pltpu.make_async_copy(ys_hbm.at[pl.ds(0, tok)], buf.at[slot, k], sem.at[slot]).wait()
    w = w_ref[...]
    out = h_ref[...]
    for k in range(TOP_K):
        out = out + w[:, k:k + 1] * buf[slot, k]
    o_ref[...] = out


def _combine(pos, h, w_rows, ys, tok):
    t, d_model = h.shape
    n = t // tok
    smem = lambda imap: pl.BlockSpec((SUBLANES, tok), imap, memory_space=pltpu.SMEM)
    row = lambda w: pl.BlockSpec((tok, w), lambda i: (i, 0))
    return pl.pallas_call(
        functools.partial(_combine_kernel, tok),
        grid=(n,),
        in_specs=[smem(lambda i: (0, i)), smem(lambda i: (0, jnp.minimum(i + 1, n - 1))),
                  row(d_model), row(LANES), pl.BlockSpec(memory_space=pl.ANY)],
        out_specs=row(d_model),
        out_shape=jax.ShapeDtypeStruct((t, d_model), F32),
        scratch_shapes=[pltpu.VMEM((2, TOP_K, tok, d_model), F32), pltpu.SemaphoreType.DMA((2,))],
        compiler_params=_cparams(("arbitrary",)),
        name="moe_combine",
    )(pos, pos, h, w_rows, ys)


def _routing_tables(counts, n_tiles):
    padded = (counts + MOE_TM - 1) // MOE_TM * MOE_TM
    ends = jnp.cumsum(padded)
    starts = ends - padded
    tile_row = jnp.arange(n_tiles, dtype=I32) * MOE_TM
    tile_expert = jnp.minimum(jnp.sum((tile_row[:, None] >= ends[None, :]).astype(I32), axis=1), N_EXPERTS - 1)
    n_used = jnp.maximum(ends[-1:] // MOE_TM, 1)
    return starts.astype(I32), tile_expert.astype(I32), n_used.astype(I32)


def _rope_tables(positions):
    half = HEAD_DIM // 2
    inv_freq = ROPE_THETA ** (-jnp.arange(half, dtype=F32) / half)
    ang = positions.astype(F32)[:, None] * inv_freq[None, :]
    cos = jnp.cos(ang)
    sin = jnp.sin(ang)
    cos_h = jnp.concatenate([cos, cos], axis=1)
    sin_h = jnp.concatenate([-sin, sin], axis=1)
    return jnp.tile(cos_h, (1, N_HEADS)), jnp.tile(sin_h, (1, N_HEADS))


def kernel(x_prompt, x_sample, cache_k, cache_v, state_ssm_re, state_ssm_im, page_table, norm1_g, w_in,
           q_norm_g, k_norm_g, ssm_a_re, ssm_a_im, ssm_log_dt, ssm_b_re, ssm_b_im, ssm_c_re, ssm_c_im,
           ssm_d, w_attn_up, w_ssm_glu, w_out, norm2_g, w_router, b_router, w_gate, b_gate, w_up, b_up,
           w_down, b_down):
    batch, seq, d_model = x_prompt.shape
    dec_batch = x_sample.shape[0]
    n_pages = page_table.shape[1]
    page = cache_k.shape[2]
    past_len = n_pages * page
    assert x_sample.shape[1] == 1 and w_in.shape[0] == 1
    assert seq % ROW_TILE == 0 and seq % MOBA_BLOCK == 0 and past_len % MOBA_BLOCK == 0
    assert page == LANES, "the sample attention keeps one page per lane tile"
    t_prompt = batch * seq
    heads = (N_HEADS, HEAD_DIM)

    xp = x_prompt.reshape(t_prompt, d_model)
    xs = x_sample.reshape(dec_batch, d_model)
    hd, hdt = _head_indicator()
    w_in_b = w_in[0].astype(BF16)
    q_gain = jnp.tile(q_norm_g, (1, N_HEADS))
    k_gain = jnp.tile(k_norm_g, (1, N_HEADS))
    cos_p, sin_p = _rope_tables(jnp.arange(seq, dtype=I32))
    cos_s, sin_s = _rope_tables(jnp.full((dec_batch,), past_len, I32))
    k_p, _, ga_p, gs_p, qt_p, kt_p, vt_p, ut_p = _inproj(xp, norm1_g, w_in_b, cos_p, sin_p, q_gain, k_gain,
                                                         hd, hdt, ROW_TILE)
    _, u_s, ga_s, gs_s, qt_s, kt_s, vt_s, _ = _inproj(xs, norm1_g, w_in_b, cos_s, sin_s, q_gain, k_gain,
                                                      hd, hdt, dec_batch)

    attn_p = _attn_prompt(qt_p, k_p, vt_p, batch, seq)
    feature_major = lambda c: c[0].transpose(0, 2, 3, 1).reshape(c.shape[1], ATTN_W, page)
    attn_s = _attn_sample(page_table, qt_s[0], kt_s[0], vt_s[0], feature_major(cache_k),
                          feature_major(cache_v))

    assert seq % S5_CHUNK == 0
    lam_re, lam_im, dt, ab_re, ab_im, bb_re, bb_im = _s5_discretize(
        ssm_a_re[0], ssm_a_im[0], ssm_log_dt[0], ssm_b_re[0], ssm_b_im[0])
    lags, w1_re, w1_im, w2_re, w2_im, al_re, al_im = _s5_chunk_operators(
        lam_re, lam_im, dt, bb_re, bb_im, ssm_c_re[0], ssm_c_im[0])
    d_rep = jnp.broadcast_to(ssm_d[0][:, None, None], (SSM_W, batch, S5_CHUNK))
    y_t, hp_re, hp_im = _s5_prompt(ut_p.transpose(1, 0, 2), lags, w1_re, w1_im, w2_re, w2_im, al_re, al_im,
                                   d_rep)
    yt_p = y_t.transpose(1, 0, 2)
    width = N_GROUPS * STATE_DIM
    y_s, hs_re, hs_im = _s5_sample(
        u_s, state_ssm_re[0].reshape(dec_batch, width), state_ssm_im[0].reshape(dec_batch, width),
        _block_diag(bb_re.transpose(0, 2, 1)).astype(BF16), _block_diag(bb_im.transpose(0, 2, 1)).astype(BF16),
        _block_diag(ssm_c_re[0].transpose(0, 2, 1)).astype(BF16),
        _block_diag(ssm_c_im[0].transpose(0, 2, 1)).astype(BF16),
        ab_re.reshape(1, width), ab_im.reshape(1, width), ssm_d)

    w_router_pad = jnp.concatenate([w_router[0], jnp.zeros((d_model, LANES - N_EXPERTS), F32)], axis=1)
    merge_w = (w_attn_up[0].astype(BF16), w_ssm_glu[0].astype(BF16), w_out[0].astype(BF16), norm2_g,
               w_router_pad, b_router[0].reshape(N_EXPERTS, 1))
    h_p, xn_p, idx_p, rank_p, wr_p, cnt_p = _merge(xp, attn_p, yt_p, ga_p, gs_p, *merge_w,
                                                   jnp.zeros((N_EXPERTS, LANES), F32), ROW_TILE)
    h_s, xn_s, idx_s, rank_s, wr_s, cnt_all = _merge(xs, attn_s, y_s.T[None], ga_s, gs_s, *merge_w, cnt_p,
                                                     dec_batch)

    n_pairs = (t_prompt + dec_batch) * TOP_K
    n_tiles = n_pairs // MOE_TM + N_EXPERTS
    starts, tile_expert, n_used = _routing_tables(cnt_all[:, 0].astype(I32), n_tiles)
    pos_p, pos_s = rank_p, rank_s
    for e in range(N_EXPERTS):
        pos_p = pos_p + jnp.where(idx_p == e, starts[e], 0)
        pos_s = pos_s + jnp.where(idx_s == e, starts[e], 0)
    x_sorted = jnp.zeros((n_tiles * MOE_TM, d_model), F32)
    x_sorted = _dispatch(pos_p, xn_p, x_sorted, MOE_TOK)
    x_sorted = _dispatch(pos_s, xn_s, x_sorted, dec_batch)
    y_sorted = _moe_grouped(tile_expert, n_used, x_sorted, w_gate[0], b_gate[0], w_up[0], b_up[0],
                            w_down[0], b_down[0])
    out_p = _combine(pos_p, h_p, wr_p, y_sorted, MOE_TOK)
    out_s = _combine(pos_s, h_s, wr_s, y_sorted, dec_batch)

    per_seq = lambda a, n, s: a.reshape((1, n) + heads + (s,)).transpose(0, 1, 4, 2, 3)
    new_rows = lambda a: per_seq(a, 1, dec_batch).transpose(0, 2, 1, 3, 4)
    return (out_p.reshape(batch, seq, d_model),
            out_s.reshape(dec_batch, 1, d_model),
            per_seq(kt_p, batch, seq),
            per_seq(vt_p, batch, seq),
            hp_re.transpose(1, 0, 2)[None],
            hp_im.transpose(1, 0, 2)[None],
            new_rows(kt_s),
            new_rows(vt_s),
            hs_re.reshape(1, dec_batch, N_GROUPS, STATE_DIM),
            hs_im.reshape(1, dec_batch, N_GROUPS, STATE_DIM))
```

```python
import functools
import math

import jax
import jax.numpy as jnp
from jax import lax
from jax.experimental import pallas as pl
from jax.experimental.pallas import tpu as pltpu

F32 = jnp.float32
BF16 = jnp.bfloat16
I32 = jnp.int32

D_MODEL = 1024
N_HEADS = 8
HEAD_DIM = 64
ATTN_W = N_HEADS * HEAD_DIM
MOBA_BLOCK = 256
MOBA_TOP = 3
ROPE_THETA = 10000.0
SSM_W = 512
GROUP_CH = 16
N_GROUPS = 32
STATE_DIM = 64
N_EXPERTS = 32
TOP_K = 4
SWIGLU_ALPHA = 1.702
SWIGLU_LIMIT = 7.0
EPS = 1e-6
IN_W = 4096
NEG_BIG = -1e9

LANES = 128
SUBLANES = 8
ROW_TILE = 512
S5_CHUNK = LANES
MOE_TM = 512
MOE_TOK = 256
FF_CHUNK = 512
VMEM_LIMIT = 56 * 1024 * 1024


def _cparams(sem):
    return pltpu.CompilerParams(dimension_semantics=sem, vmem_limit_bytes=VMEM_LIMIT)


def _dot(a, b):
    return jnp.dot(a, b, preferred_element_type=F32)


def _split_hi_lo(a):
    hi = a.astype(BF16)
    lo = (a - hi.astype(F32)).astype(BF16)
    return hi, lo


def _dot_hilo(a, b_bf16):
    hi, lo = _split_hi_lo(a)
    return _dot(hi, b_bf16) + _dot(lo, b_bf16)


def _head_indicator():
    c = jnp.arange(ATTN_W)[:, None] // HEAD_DIM
    h = jnp.arange(LANES)[None, :]
    hd = (c == h).astype(BF16)
    return hd, hd.T


def _head_norm_rope(t, gain, cos, sin_signed, hd, hdt):
    ssq = _dot_hilo(t * t, hd)
    rstd = lax.rsqrt(ssq * (1.0 / HEAD_DIM) + EPS)
    t = t * _dot_hilo(rstd, hdt) * gain
    lane = lax.broadcasted_iota(I32, (t.shape[0], LANES), 1)
    first_half = (lane % HEAD_DIM) < (HEAD_DIM // 2)
    outs = []
    for s in range(ATTN_W // LANES):
        ts = t[:, s * LANES:(s + 1) * LANES]
        partner = jnp.where(first_half,
                            pltpu.roll(ts, LANES - HEAD_DIM // 2, axis=1),
                            pltpu.roll(ts, HEAD_DIM // 2, axis=1))
        outs.append(ts * cos[:, s * LANES:(s + 1) * LANES]
                    + partner * sin_signed[:, s * LANES:(s + 1) * LANES])
    return jnp.concatenate(outs, axis=1)


def _inproj_kernel(zero_chunks, x_ref, g1_ref, w_ref, cos_ref, sin_ref, qg_ref, kg_ref, hd_ref, hdt_ref,
                   k_ref, u_ref, ga_ref, gs_ref, qt_ref, kt_ref, vt_ref, ut_ref, *zero_refs):
    zero_copies = []
    if zero_chunks:
        z_hbm, zbuf, zsem = zero_refs
        step = pl.program_id(0)
        chunk = zbuf.shape[0]

        @pl.when(step == 0)
        def _():
            zbuf[...] = jnp.zeros(zbuf.shape, F32)

        for c in range(zero_chunks):
            row0 = pl.multiple_of((step * zero_chunks + c) * chunk, SUBLANES)
            zero_copies.append(pltpu.make_async_copy(zbuf, z_hbm.at[pl.ds(row0, chunk)], zsem))
        for cp in zero_copies:
            cp.start()
    x = x_ref[...]
    xn = x * lax.rsqrt(jnp.mean(x * x, axis=-1, keepdims=True) + EPS) * g1_ref[...]
    xb = xn.astype(BF16)
    hd = hd_ref[...]
    hdt = hdt_ref[...]
    cos = cos_ref[...]
    sin = sin_ref[...]
    q = _dot(xb, w_ref[:, 0:ATTN_W])
    qt_ref[0] = _head_norm_rope(q, qg_ref[...], cos, sin, hd, hdt).T
    k = _head_norm_rope(_dot(xb, w_ref[:, ATTN_W:2 * ATTN_W]), kg_ref[...], cos, sin, hd, hdt)
    k_ref[...] = k
    kt_ref[0] = k.T
    vt_ref[0] = _dot(xb, w_ref[:, 2 * ATTN_W:3 * ATTN_W]).T
    u = _dot(xb, w_ref[:, 3 * ATTN_W:3 * ATTN_W + SSM_W])
    u_ref[...] = u
    ut_ref[0] = u.T
    ga_ref[...] = _dot(xb, w_ref[:, 2048:3072])
    gs_ref[...] = _dot(xb, w_ref[:, 3072:4096])
    for cp in zero_copies:
        cp.wait()


def _inproj(x, norm1_g, w_in_bf16, cos_tab, sin_tab, q_gain, k_gain, hd, hdt, tile, zero_rows=0):
    t = x.shape[0]
    steps = t // tile
    period = cos_tab.shape[0] // tile
    row = lambda w: pl.BlockSpec((tile, w), lambda i: (i, 0))
    const = lambda shape: pl.BlockSpec(shape, lambda i: (0, 0))
    pos = pl.BlockSpec((tile, ATTN_W), lambda i: (i % period, 0))
    feature_major = pl.BlockSpec((1, ATTN_W, tile), lambda i: (i // period, 0, i % period))
    out_specs = [row(ATTN_W)] * 2 + [row(D_MODEL)] * 2 + [feature_major] * 4
    out_shape = ([jax.ShapeDtypeStruct((t, ATTN_W), F32)] * 2
                 + [jax.ShapeDtypeStruct((t, D_MODEL), F32)] * 2
                 + [jax.ShapeDtypeStruct((t // (period * tile), ATTN_W, period * tile), F32)] * 4)
    zero_chunks, scratch = 0, []
    if zero_rows:
        assert zero_rows % (steps * SUBLANES) == 0
        per_step = zero_rows // steps
        chunk = max(c for c in range(SUBLANES, ROW_TILE + 1, SUBLANES) if per_step % c == 0)
        zero_chunks = per_step // chunk
        out_specs = out_specs + [pl.BlockSpec(memory_space=pl.ANY)]
        out_shape = out_shape + [jax.ShapeDtypeStruct((zero_rows, D_MODEL), F32)]
        scratch = [pltpu.VMEM((chunk, D_MODEL), F32), pltpu.SemaphoreType.DMA(())]
    return pl.pallas_call(
        functools.partial(_inproj_kernel, zero_chunks),
        grid=(steps,),
        in_specs=[row(D_MODEL), const((1, D_MODEL)), const((D_MODEL, IN_W)), pos, pos,
                  const((1, ATTN_W)), const((1, ATTN_W)), const((ATTN_W, LANES)), const((LANES, ATTN_W))],
        out_specs=out_specs,
        out_shape=out_shape,
        scratch_shapes=scratch,
        compiler_params=_cparams(("arbitrary",)),
        name="inproj",
    )(x, norm1_g, w_in_bf16, cos_tab, sin_tab, q_gain, k_gain, hd, hdt)


def _block_penalty_t(gate_t, n_past, own_blk):
    blk = lax.broadcasted_iota(I32, gate_t.shape, 0)
    valid = blk < n_past
    g = jnp.where(valid, gate_t, -jnp.inf)
    cnt = jnp.zeros(gate_t.shape, I32)
    for m in range(gate_t.shape[0]):
        gm = g[m:m + 1, :]
        beats = jnp.where(gm > g, 1, jnp.where((gm == g) & (blk > m), 1, 0))
        cnt = cnt + beats
    keep = (valid & (cnt < MOBA_TOP)) | (blk == own_blk)
    return jnp.where(keep, 0.0, NEG_BIG)


ATTN_PAIRS = 2


def _attn_prompt_kernel(qt_ref, k_ref, vt_ref, o_ref, kaug_ref, vtb_ref, kmean_ref):
    j = pl.program_id(2)
    blk = MOBA_BLOCK
    nblk = k_ref.shape[0] // blk
    pairs = k_ref.shape[1] // LANES

    @pl.when(j == 0)
    def _():
        lane = lax.broadcasted_iota(I32, (blk, LANES), 1)
        for n in range(nblk):
            rows = slice(n * blk, (n + 1) * blk)
            onehot = jnp.where(lane == n, 1.0, 0.0).astype(BF16)
            for g in range(pairs):
                k = k_ref[rows, g * LANES:(g + 1) * LANES]
                kaug_ref[g, rows, 0:LANES] = k.astype(BF16)
                kaug_ref[g, rows, LANES:2 * LANES] = onehot
                kmean_ref[g, n:n + 1, :] = jnp.mean(k, axis=0, keepdims=True)
        vtb_ref[...] = vt_ref[0].astype(BF16)

    feat = lax.broadcasted_iota(I32, (LANES, blk), 0)
    zpad = jnp.zeros((LANES - nblk, blk), F32)
    qaug = []
    for g in range(pairs):
        q_t = qt_ref[0, g * LANES:(g + 1) * LANES, :]
        km_hi, km_lo = _split_hi_lo(kmean_ref[g])
        for h in range(2):
            qh = jnp.where(feat >= HEAD_DIM if h == 1 else feat < HEAD_DIM, q_t, 0.0)
            q_hi, q_lo = _split_hi_lo(qh)
            gate_t = _dot(km_hi, q_hi) + _dot(km_hi, q_lo) + _dot(km_lo, q_hi)
            pen_t = jnp.concatenate([_block_penalty_t(gate_t, j, j), zpad], axis=0)
            qaug.append((g, g * LANES + h * HEAD_DIM,
                         jnp.concatenate([(qh * (1.0 / math.sqrt(HEAD_DIM))).astype(BF16),
                                          pen_t.astype(BF16)], axis=0)))

    key_i = lax.broadcasted_iota(I32, (blk, blk), 0)
    qry_i = lax.broadcasted_iota(I32, (blk, blk), 1)
    future = key_i > qry_i

    def attend(jq):
        past = jq * blk
        outs = []
        for g, row0, qa in qaug:
            s_diag = jnp.where(future, NEG_BIG, _dot(kaug_ref[g, past:past + blk, :], qa))
            m = jnp.max(s_diag, axis=0, keepdims=True)
            if jq > 0:
                s_past = _dot(kaug_ref[g, 0:past, :], qa)
                m = jnp.maximum(m, jnp.max(s_past, axis=0, keepdims=True))
                p_past = jnp.exp(s_past - m)
            p_diag = jnp.exp(s_diag - m)
            l = jnp.sum(p_diag, axis=0, keepdims=True)
            feats = slice(row0, row0 + HEAD_DIM)
            pv = _dot(vtb_ref[feats, past:past + blk], p_diag.astype(BF16))
            if jq > 0:
                l = l + jnp.sum(p_past, axis=0, keepdims=True)
                pv = pv + _dot(vtb_ref[feats, 0:past], p_past.astype(BF16))
            outs.append(pv / l)
        o_ref[...] = jnp.concatenate(outs, axis=0).T

    for jq in range(nblk):
        pl.when(j == jq)(functools.partial(attend, jq))


def _attn_prompt(qt, k, vt, batch, seq):
    nq = seq // MOBA_BLOCK
    width = ATTN_PAIRS * LANES
    return pl.pallas_call(
        _attn_prompt_kernel,
        grid=(batch, ATTN_W // width, nq),
        in_specs=[pl.BlockSpec((1, width, MOBA_BLOCK), lambda b, hp, j: (b, hp, j)),
                  pl.BlockSpec((seq, width), lambda b, hp, j: (b, hp)),
                  pl.BlockSpec((1, width, seq), lambda b, hp, j: (b, hp, 0))],
        out_specs=pl.BlockSpec((MOBA_BLOCK, width), lambda b, hp, j: (b * nq + j, hp)),
        out_shape=jax.ShapeDtypeStruct((batch * seq, ATTN_W), F32),
        scratch_shapes=[pltpu.VMEM((ATTN_PAIRS, seq, 2 * LANES), BF16), pltpu.VMEM((width, seq), BF16),
                        pltpu.VMEM((ATTN_PAIRS, nq, LANES), F32)],
        compiler_params=_cparams(("parallel", "parallel", "arbitrary")),
        name="attn_prompt",
    )(qt, k, vt)


def _attn_sample_kernel(n_pages, pt_ref, qt_ref, kt_ref, vt_ref, *rest):
    del pt_ref
    k_pages = rest[:n_pages]
    v_pages = rest[n_pages:2 * n_pages]
    o_ref = rest[2 * n_pages]
    b = pl.program_id(0)
    page = k_pages[0].shape[2]
    ppb = MOBA_BLOCK // page
    nblk = n_pages // ppb
    seq_i = lax.broadcasted_iota(I32, (qt_ref.shape[1], page), 0)
    lane_i = lax.broadcasted_iota(I32, (qt_ref.shape[1], page), 1)
    pick_all = jnp.where(seq_i == b, 1.0, 0.0).astype(BF16)
    pick_lane0 = jnp.where((seq_i == b) & (lane_i == 0), 1.0, 0.0).astype(BF16)
    qb = _dot_hilo(qt_ref[...], pick_all) * (1.0 / math.sqrt(HEAD_DIM))
    k_new = _dot_hilo(kt_ref[...], pick_lane0)
    v_new = _dot_hilo(vt_ref[...], pick_lane0)

    def head_sums(x):
        return jnp.sum(x.reshape(N_HEADS, HEAD_DIM, x.shape[1]), axis=1)

    def per_feature(p):
        return jnp.broadcast_to(p[:, None, :], (N_HEADS, HEAD_DIM, p.shape[1])).reshape(ATTN_W, p.shape[1])

    s_pages = [head_sums(k_pages[p][0] * qb) for p in range(n_pages)]
    lane = lax.broadcasted_iota(I32, s_pages[0].shape, 1)
    s_new = jnp.where(lane == 0, head_sums(k_new * qb), NEG_BIG)

    gates = []
    for n in range(nblk):
        blk_sum = s_pages[n * ppb]
        for i in range(1, ppb):
            blk_sum = blk_sum + s_pages[n * ppb + i]
        gates.append(jnp.sum(blk_sum, axis=1, keepdims=True))
    pens = []
    for n in range(nblk):
        cnt = jnp.zeros(gates[n].shape, I32)
        for m in range(nblk):
            if m == n:
                continue
            beats = (gates[m] >= gates[n]) if m < n else (gates[m] > gates[n])
            cnt = cnt + jnp.where(beats, 1, 0)
        pens.append(jnp.where(cnt < MOBA_TOP, 0.0, NEG_BIG))

    s_adj = [s_pages[p] + pens[p // ppb] for p in range(n_pages)]
    m = s_new
    for s in s_adj:
        m = jnp.maximum(m, s)
    m = jnp.max(m, axis=1, keepdims=True)
    e_new = jnp.exp(s_new - m)
    es = [jnp.exp(s - m) for s in s_adj]
    total = e_new
    for e in es:
        total = total + e
    inv = 1.0 / jnp.sum(total, axis=1, keepdims=True)
    acc = v_new * per_feature(e_new * inv)
    for p in range(n_pages):
        acc = acc + v_pages[p][0] * per_feature(es[p] * inv)
    a_hi, a_lo = _split_hi_lo(acc)
    ones = jnp.ones((SUBLANES, page), BF16)
    nt = (((1,), (1,)), ((), ()))
    row = (lax.dot_general(ones, a_hi, nt, preferred_element_type=F32)
           + lax.dot_general(ones, a_lo, nt, preferred_element_type=F32))
    o_ref[pl.ds(b, 1), :] = row[0:1, :]


def _attn_sample(page_table, qt, kt, vt, cache_kt, cache_vt):
    dec_batch, n_pages = page_table.shape
    page = cache_kt.shape[2]

    def page_spec(p):
        return pl.BlockSpec((1, ATTN_W, page), lambda b, pt: (pt[b, p], 0, 0))

    own = pl.BlockSpec((ATTN_W, dec_batch), lambda b, pt: (0, 0))
    grid_spec = pltpu.PrefetchScalarGridSpec(
        num_scalar_prefetch=1,
        grid=(dec_batch,),
        in_specs=[own] * 3 + [page_spec(p) for p in range(n_pages)] * 2,
        out_specs=pl.BlockSpec((dec_batch, ATTN_W), lambda b, pt: (0, 0)),
    )
    return pl.pallas_call(
        functools.partial(_attn_sample_kernel, n_pages),
        grid_spec=grid_spec,
        out_shape=jax.ShapeDtypeStruct((dec_batch, ATTN_W), F32),
        compiler_params=_cparams(("arbitrary",)),
        name="attn_sample",
    )(page_table, qt, kt, vt, *([cache_kt] * n_pages), *([cache_vt] * n_pages))


def _s5_discretize(a_re, a_im, log_dt, b_re, b_im):
    lam_re = a_re.astype(F32)
    lam_im = a_im.astype(F32)
    dt = jnp.exp(log_dt.astype(F32))[:, None]
    mag = jnp.exp(lam_re * dt)
    ab_re = mag * jnp.cos(lam_im * dt)
    ab_im = mag * jnp.sin(lam_im * dt)
    den = lam_re * lam_re + lam_im * lam_im
    n_re, n_im = ab_re - 1.0, ab_im
    f_re = (n_re * lam_re + n_im * lam_im) / den
    f_im = (n_im * lam_re - n_re * lam_im) / den
    br, bi = b_re.astype(F32), b_im.astype(F32)
    bb_re = f_re[..., None] * br - f_im[..., None] * bi
    bb_im = f_re[..., None] * bi + f_im[..., None] * br
    return lam_re, lam_im, dt, ab_re, ab_im, bb_re, bb_im


def _s5_chunk_operators(lam_re, lam_im, dt, bb_re, bb_im, c_re, c_im):
    L = S5_CHUNK
    hp = lax.Precision.HIGHEST
    steps = jnp.arange(L + 1, dtype=F32)[None, None, :]
    rate = dt[:, :, None] * steps
    pmag = jnp.exp(lam_re[:, :, None] * rate)
    pw_re = pmag * jnp.cos(lam_im[:, :, None] * rate)
    pw_im = pmag * jnp.sin(lam_im[:, :, None] * rate)
    cr, ci = c_re.astype(F32), c_im.astype(F32)
    btr, bti = bb_re.transpose(0, 2, 1), bb_im.transpose(0, 2, 1)
    bc_re = btr[:, :, None, :] * cr[:, None, :, :] - bti[:, :, None, :] * ci[:, None, :, :]
    bc_im = btr[:, :, None, :] * ci[:, None, :, :] + bti[:, :, None, :] * cr[:, None, :, :]
    pairs = GROUP_CH * GROUP_CH
    lags = (jnp.einsum('gxp,gpj->gxj', bc_re.reshape(N_GROUPS, pairs, STATE_DIM), pw_re[:, :, :L], precision=hp)
            - jnp.einsum('gxp,gpj->gxj', bc_im.reshape(N_GROUPS, pairs, STATE_DIM), pw_im[:, :, :L],
                         precision=hp))
    rev_re = pw_re[:, :, L - 1::-1].transpose(0, 2, 1)
    rev_im = pw_im[:, :, L - 1::-1].transpose(0, 2, 1)
    w1_re = rev_re[:, None, :, :] * btr[:, :, None, :] - rev_im[:, None, :, :] * bti[:, :, None, :]
    w1_im = rev_re[:, None, :, :] * bti[:, :, None, :] + rev_im[:, None, :, :] * btr[:, :, None, :]
    w1_re = w1_re.reshape(N_GROUPS, GROUP_CH * L, STATE_DIM)
    w1_im = w1_im.reshape(N_GROUPS, GROUP_CH * L, STATE_DIM)
    ctr, cti = cr.transpose(0, 2, 1), ci.transpose(0, 2, 1)
    nx_re, nx_im = pw_re[:, :, 1:], pw_im[:, :, 1:]
    w2_re = ctr[:, :, :, None] * nx_re[:, :, None, :] - cti[:, :, :, None] * nx_im[:, :, None, :]
    w2_im = -(ctr[:, :, :, None] * nx_im[:, :, None, :] + cti[:, :, :, None] * nx_re[:, :, None, :])
    w2_re = w2_re.reshape(N_GROUPS, STATE_DIM, GROUP_CH * L)
    w2_im = w2_im.reshape(N_GROUPS, STATE_DIM, GROUP_CH * L)
    return (lags, w1_re.astype(BF16), w1_im.astype(BF16), w2_re.astype(BF16), w2_im.astype(BF16),
            pw_re[:, None, :, L], pw_im[:, None, :, L])


def _s5_prompt_kernel(x_ref, lag_ref, w1r_ref, w1i_ref, w2r_ref, w2i_ref, ar_ref, ai_ref, d_ref,
                      y_ref, hr_ref, hi_ref, toep_ref, u_ref, pre_ref, pim_ref):
    nch, batch, seq = x_ref.shape
    L = S5_CHUNK
    n_chunks = seq // L

    s_i = lax.broadcasted_iota(I32, (L, L), 0)
    t_i = lax.broadcasted_iota(I32, (L, L), 1)
    causal = t_i >= s_i
    for cp in range(nch):
        for c in range(nch):
            first = jnp.broadcast_to(lag_ref[0, cp * nch + c:cp * nch + c + 1, :], (L, L))
            tile = pltpu.roll(first, 0, 1, stride=1, stride_axis=0)
            toep_ref[cp * L:(cp + 1) * L, c * L:(c + 1) * L] = jnp.where(causal, tile, 0.0).astype(BF16)

    for k in range(n_chunks):
        for cp in range(nch):
            u_ref[k * batch:(k + 1) * batch, cp * L:(cp + 1) * L] = x_ref[cp, :, k * L:(k + 1) * L]
    ub = u_ref[...].astype(BF16)

    d_re = _dot(ub, w1r_ref[0])
    d_im = _dot(ub, w1i_ref[0])
    a_re = ar_ref[0]
    a_im = ai_ref[0]
    h_re = jnp.zeros((batch, STATE_DIM), F32)
    h_im = jnp.zeros((batch, STATE_DIM), F32)
    for k in range(n_chunks):
        rows = slice(k * batch, (k + 1) * batch)
        pre_ref[rows, :] = h_re
        pim_ref[rows, :] = h_im
        h_re, h_im = (a_re * h_re - a_im * h_im + d_re[rows, :], a_re * h_im + a_im * h_re + d_im[rows, :])
    hr_ref[0] = h_re
    hi_ref[0] = h_im

    y = (_dot(ub, toep_ref[...]) + _dot(pre_ref[...].astype(BF16), w2r_ref[0])
         + _dot(pim_ref[...].astype(BF16), w2i_ref[0]))
    for k in range(n_chunks):
        for c in range(nch):
            lanes = slice(k * L, (k + 1) * L)
            y_ref[c, :, lanes] = (y[k * batch:(k + 1) * batch, c * L:(c + 1) * L]
                                  + d_ref[c] * x_ref[c, :, lanes])


def _s5_prompt(x, lags, w1_re, w1_im, w2_re, w2_im, al_re, al_im, d_rep):
    width, batch, seq = x.shape
    g = width // GROUP_CH
    cols = GROUP_CH * S5_CHUNK
    rows = seq // S5_CHUNK * batch
    blk = lambda s: pl.BlockSpec((1,) + s, lambda i: (i, 0, 0))
    chan = pl.BlockSpec((GROUP_CH, batch, seq), lambda i: (i, 0, 0))
    return pl.pallas_call(
        _s5_prompt_kernel,
        grid=(g,),
        in_specs=[chan, blk((GROUP_CH * GROUP_CH, S5_CHUNK)), blk((cols, STATE_DIM)), blk((cols, STATE_DIM)),
                  blk((STATE_DIM, cols)), blk((STATE_DIM, cols)), blk((1, STATE_DIM)), blk((1, STATE_DIM)),
                  pl.BlockSpec((GROUP_CH, batch, S5_CHUNK), lambda i: (i, 0, 0))],
        out_specs=[chan, blk((batch, STATE_DIM)), blk((batch, STATE_DIM))],
        out_shape=[jax.ShapeDtypeStruct((width, batch, seq), F32),
                   jax.ShapeDtypeStruct((g, batch, STATE_DIM), F32),
                   jax.ShapeDtypeStruct((g, batch, STATE_DIM), F32)],
        scratch_shapes=[pltpu.VMEM((cols, cols), BF16), pltpu.VMEM((rows, cols), F32),
                        pltpu.VMEM((rows, STATE_DIM), F32), pltpu.VMEM((rows, STATE_DIM), F32)],
        compiler_params=_cparams(("parallel",)),
        name="s5_prompt",
    )(x, lags, w1_re, w1_im, w2_re, w2_im, al_re, al_im, d_rep)


def _s5_sample_kernel(u_ref, h0r_ref, h0i_ref, bdr_ref, bdi_ref, cdr_ref, cdi_ref, ar_ref, ai_ref, d_ref,
                      y_ref, hr_ref, hi_ref):
    u = u_ref[...]
    ub = u.astype(BF16)
    a_re = ar_ref[...]
    a_im = ai_ref[...]
    h0r = h0r_ref[...]
    h0i = h0i_ref[...]
    h_re = a_re * h0r - a_im * h0i + _dot(ub, bdr_ref[...])
    h_im = a_re * h0i + a_im * h0r + _dot(ub, bdi_ref[...])
    hr_ref[...] = h_re
    hi_ref[...] = h_im
    y_ref[...] = (_dot(h_re.astype(BF16), cdr_ref[...]) - _dot(h_im.astype(BF16), cdi_ref[...])
                  + d_ref[...] * u)


def _s5_sample(u_s, h0_re, h0_im, bd_re, bd_im, cd_re, cd_im, ab_re, ab_im, d_skip):
    n = u_s.shape[0]
    width = N_GROUPS * STATE_DIM
    return pl.pallas_call(
        _s5_sample_kernel,
        out_shape=[jax.ShapeDtypeStruct((n, SSM_W), F32), jax.ShapeDtypeStruct((n, width), F32),
                   jax.ShapeDtypeStruct((n, width), F32)],
        compiler_params=pltpu.CompilerParams(vmem_limit_bytes=VMEM_LIMIT),
        name="s5_sample",
    )(u_s, h0_re, h0_im, bd_re, bd_im, cd_re, cd_im, ab_re, ab_im, d_skip)


def _block_diag(m):
    g, r, c = m.shape
    eye = jnp.eye(g, dtype=m.dtype)
    return (m[:, :, None, :] * eye[:, None, :, None]).reshape(g * r, g * c)


def _merge_kernel(x_ref, attn_ref, yssm_ref, ga_ref, gs_ref, wau_ref, wglu_ref, wout_ref, g2_ref,
                  wr_ref, br_ref, su_ref, cnt_in_ref,
                  h_ref, xn_ref, idx_ref, rank_ref, w_ref, cnt_ref, run_ref):
    @pl.when(pl.program_id(0) == 0)
    def _():
        run_ref[...] = cnt_in_ref[...]

    branch_attn = _dot(attn_ref[...].astype(BF16), wau_ref[...])
    glu = _dot(jax.nn.gelu(yssm_ref[0].T).astype(BF16), wglu_ref[...])
    branch_ssm = glu[:, :D_MODEL] * jax.nn.sigmoid(glu[:, D_MODEL:])
    merged = jax.nn.sigmoid(ga_ref[...]) * branch_attn + jax.nn.sigmoid(gs_ref[...]) * branch_ssm
    h = x_ref[...] + _dot(merged.astype(BF16), wout_ref[...])
    h_ref[...] = h
    xn = h * lax.rsqrt(jnp.mean(h * h, axis=-1, keepdims=True) + EPS) * g2_ref[...]
    xn_ref[...] = xn

    x_hi, x_lo = _split_hi_lo(xn)
    w_hi, w_lo = _split_hi_lo(wr_ref[...])
    logits = _dot(x_hi, w_hi) + _dot(x_lo, w_hi) + _dot(x_hi, w_lo)
    lt = logits.T[0:N_EXPERTS, :] + br_ref[...]
    rows = lt.shape[1]
    e_iota = lax.broadcasted_iota(I32, lt.shape, 0).astype(F32)
    tops, idxs, hots = [], [], []
    for _ in range(TOP_K):
        m = jnp.max(lt, axis=0, keepdims=True)
        idx = jnp.min(jnp.where(lt == m, e_iota, float(N_EXPERTS)), axis=0, keepdims=True)
        hot = e_iota == idx
        tops.append(m)
        idxs.append(idx)
        hots.append(hot)
        lt = jnp.where(hot, -jnp.inf, lt)
    exps = [jnp.exp(t - tops[0]) for t in tops]
    denom = exps[0] + exps[1] + exps[2] + exps[3]

    member = jnp.zeros(lt.shape, F32)
    for hot in hots:
        member = member + jnp.where(hot, 1.0, 0.0)
    before = run_ref[:, 0:1] + _dot(member.astype(BF16), su_ref[...])
    ranks = [jnp.sum(jnp.where(hot, before, 0.0), axis=0, keepdims=True) for hot in hots]
    run_ref[...] = run_ref[...] + jnp.sum(member, axis=1, keepdims=True)
    cnt_ref[...] = run_ref[...]

    zrows = jnp.zeros((SUBLANES - TOP_K, rows), F32)
    idx_ref[...] = jnp.concatenate(idxs + [zrows], axis=0).astype(I32)
    rank_ref[...] = jnp.concatenate(ranks + [zrows], axis=0).astype(I32)
    w_t = jnp.concatenate([e / denom for e in exps] + [jnp.zeros((LANES - TOP_K, rows), F32)], axis=0)
    w_ref[...] = w_t.T


def _merge(x, attn, y_ssm_t, g_attn, g_ssm, w_attn_up, w_ssm_glu, w_out, norm2_g, w_router_pad,
           b_router_col, cnt_in, tile):
    t = x.shape[0]
    period = y_ssm_t.shape[2] // tile
    row = lambda w: pl.BlockSpec((tile, w), lambda i: (i, 0))
    col = pl.BlockSpec((SUBLANES, tile), lambda i: (0, i))
    const = lambda shape: pl.BlockSpec(shape, lambda i: (0, 0))
    feature_major = pl.BlockSpec((1, SSM_W, tile), lambda i: (i // period, 0, i % period))
    strict_upper = jnp.triu(jnp.ones((tile, tile), F32), 1).astype(BF16)
    return pl.pallas_call(
        _merge_kernel,
        grid=(t // tile,),
        in_specs=[row(D_MODEL), row(ATTN_W), feature_major, row(D_MODEL), row(D_MODEL),
                  const((ATTN_W, D_MODEL)), const((SSM_W, 2 * D_MODEL)), const((D_MODEL, D_MODEL)),
                  const((1, D_MODEL)), const((D_MODEL, LANES)), const((N_EXPERTS, 1)),
                  const((tile, tile)), const((N_EXPERTS, LANES))],
        out_specs=[row(D_MODEL), row(D_MODEL), col, col, row(LANES), const((N_EXPERTS, LANES))],
        out_shape=[jax.ShapeDtypeStruct((t, D_MODEL), F32), jax.ShapeDtypeStruct((t, D_MODEL), F32),
                   jax.ShapeDtypeStruct((SUBLANES, t), I32), jax.ShapeDtypeStruct((SUBLANES, t), I32),
                   jax.ShapeDtypeStruct((t, LANES), F32), jax.ShapeDtypeStruct((N_EXPERTS, LANES), F32)],
        scratch_shapes=[pltpu.VMEM((N_EXPERTS, LANES), F32)],
        compiler_params=_cparams(("arbitrary",)),
        name="merge_router",
    )(x, attn, y_ssm_t, g_attn, g_ssm, w_attn_up, w_ssm_glu, w_out, norm2_g, w_router_pad, b_router_col,
      strict_upper, cnt_in)


def _dispatch_kernel(tok, pos_ref, x_ref, xs_in, xs_out, sem):
    del xs_in

    for r in range(tok):
        for k in range(TOP_K):
            pltpu.make_async_copy(x_ref.at[pl.ds(r, 1)], xs_out.at[pl.ds(pos_ref[k, r], 1)],
                                  sem).start(priority=k % 2)
    for _ in range(TOP_K):
        pltpu.make_async_copy(x_ref, xs_out.at[pl.ds(0, tok)], sem).wait()


def _dispatch(pos, xn, xs, tok):
    t, d_model = xn.shape
    return pl.pallas_call(
        functools.partial(_dispatch_kernel, tok),
        grid=(t // tok,),
        in_specs=[pl.BlockSpec((SUBLANES, tok), lambda i: (0, i), memory_space=pltpu.SMEM),
                  pl.BlockSpec((tok, d_model), lambda i: (i, 0)), pl.BlockSpec(memory_space=pl.ANY)],
        out_specs=pl.BlockSpec(memory_space=pl.ANY),
        out_shape=jax.ShapeDtypeStruct(xs.shape, xs.dtype),
        scratch_shapes=[pltpu.SemaphoreType.DMA(())],
        input_output_aliases={2: 0},
        compiler_params=_cparams(("arbitrary",)),
        name="moe_dispatch",
    )(pos, xn, xs)


def _moe_group_kernel(te_ref, nu_ref, x_ref, wg_ref, bg_ref, wu_ref, bu_ref, wd_ref, bd_ref, y_ref,
                      wgb_ref, wub_ref, wdb_ref):
    i = pl.program_id(0)

    @pl.when(i < nu_ref[0])
    def _():
        @pl.when((i == 0) | (te_ref[i] != te_ref[jnp.maximum(i - 1, 0)]))
        def _():
            wgb_ref[...] = wg_ref[0].astype(BF16)
            wub_ref[...] = wu_ref[0].astype(BF16)
            wdb_ref[...] = wd_ref[0].astype(BF16)

        x = x_ref[...].astype(BF16)
        d_ff = wgb_ref.shape[1]
        for c in range(d_ff // FF_CHUNK):
            cols = slice(c * FF_CHUNK, (c + 1) * FF_CHUNK)
            gate = _dot(x, wgb_ref[:, cols]) + bg_ref[0, :, cols]
            up = _dot(x, wub_ref[:, cols]) + bu_ref[0, :, cols]
            gate = jnp.minimum(gate, SWIGLU_LIMIT)
            up = jnp.clip(up, -SWIGLU_LIMIT, SWIGLU_LIMIT)
            hid = (up + 1.0) * (gate * jax.nn.sigmoid(SWIGLU_ALPHA * gate))
            part = _dot(hid.astype(BF16), wdb_ref[cols, :])
            if c == 0:
                y_ref[...] = part + bd_ref[0]
            else:
                y_ref[...] += part

    @pl.when(i >= nu_ref[0])
    def _():
        y_ref[...] = jnp.zeros(y_ref.shape, F32)


def _moe_grouped(tile_expert, n_used, xs, w_gate, b_gate, w_up, b_up, w_down, b_down):
    n_exp, d_model, d_ff = w_gate.shape
    n_tiles = xs.shape[0] // MOE_TM
    wspec = lambda r, c: pl.BlockSpec((1, r, c), lambda i, te, nu: (te[i], 0, 0))
    grid_spec = pltpu.PrefetchScalarGridSpec(
        num_scalar_prefetch=2,
        grid=(n_tiles,),
        in_specs=[pl.BlockSpec((MOE_TM, d_model), lambda i, te, nu: (jnp.minimum(i, nu[0] - 1), 0)),
                  wspec(d_model, d_ff), wspec(1, d_ff), wspec(d_model, d_ff), wspec(1, d_ff),
                  wspec(d_ff, d_model), wspec(1, d_model)],
        out_specs=pl.BlockSpec((MOE_TM, d_model), lambda i, te, nu: (i, 0)),
        scratch_shapes=[pltpu.VMEM((d_model, d_ff), BF16), pltpu.VMEM((d_model, d_ff), BF16),
                        pltpu.VMEM((d_ff, d_model), BF16)],
    )
    return pl.pallas_call(
        _moe_group_kernel,
        grid_spec=grid_spec,
        out_shape=jax.ShapeDtypeStruct((xs.shape[0], d_model), F32),
        compiler_params=_cparams(("arbitrary",)),
        name="moe_grouped",
    )(tile_expert, n_used, xs, w_gate, b_gate.reshape(n_exp, 1, d_ff), w_up, b_up.reshape(n_exp, 1, d_ff),
      w_down, b_down.reshape(n_exp, 1, d_model))


def _combine_kernel(tok, pos_ref, posn_ref, h_ref, w_ref, ys_hbm, o_ref, buf, sem):
    i = pl.program_id(0)
    n = pl.num_programs(0)
    slot = i % 2

    def gather(p_ref, s):
        for r in range(tok):
            for k in range(TOP_K):
                pltpu.make_async_copy(ys_hbm.at[pl.ds(p_ref[k, r], 1)], buf.at[s, k, pl.ds(r, 1)],
                                      sem.at[s]).start(priority=k % 2)

    @pl.when(i == 0)
    def _():
        gather(pos_ref, 0)

    @pl.when(i + 1 < n)
    def _():
        gather(posn_ref, 1 - slot)

    for k in range(TOP_K):
        pltpu.make_async_copy(ys_hbm.at[pl.ds(0, tok)], buf.at[slot, k], sem.at[slot]).wait()
    w = w_ref[...]
    out = h_ref[...]
    for k in range(TOP_K):
        out = out + w[:, k:k + 1] * buf[slot, k]
    o_ref[...] = out


def _combine(pos, h, w_rows, ys, tok):
    t, d_model = h.shape
    n = t // tok
    smem = lambda imap: pl.BlockSpec((SUBLANES, tok), imap, memory_space=pltpu.SMEM)
    row = lambda w: pl.BlockSpec((tok, w), lambda i: (i, 0))
    return pl.pallas_call(
        functools.partial(_combine_kernel, tok),
        grid=(n,),
        in_specs=[smem(lambda i: (0, i)), smem(lambda i: (0, jnp.minimum(i + 1, n - 1))),
                  row(d_model), row(LANES), pl.BlockSpec(memory_space=pl.ANY)],
        out_specs=row(d_model),
        out_shape=jax.ShapeDtypeStruct((t, d_model), F32),
        scratch_shapes=[pltpu.VMEM((2, TOP_K, tok, d_model), F32), pltpu.SemaphoreType.DMA((2,))],
        compiler_params=_cparams(("arbitrary",)),
        name="moe_combine",
    )(pos, pos, h, w_rows, ys)


def _routing_tables(counts, n_tiles):
    padded = (counts + MOE_TM - 1) // MOE_TM * MOE_TM
    ends = jnp.cumsum(padded)
    starts = ends - padded
    tile_row = jnp.arange(n_tiles, dtype=I32) * MOE_TM
    tile_expert = jnp.minimum(jnp.sum((tile_row[:, None] >= ends[None, :]).astype(I32), axis=1), N_EXPERTS - 1)
    n_used = jnp.maximum(ends[-1:] // MOE_TM, 1)
    return starts.astype(I32), tile_expert.astype(I32), n_used.astype(I32)


def _rope_tables(positions):
    half = HEAD_DIM // 2
    inv_freq = ROPE_THETA ** (-jnp.arange(half, dtype=F32) / half)
    ang = positions.astype(F32)[:, None] * inv_freq[None, :]
    cos = jnp.cos(ang)
    sin = jnp.sin(ang)
    cos_h = jnp.concatenate([cos, cos], axis=1)
    sin_h = jnp.concatenate([-sin, sin], axis=1)
    return jnp.tile(cos_h, (1, N_HEADS)), jnp.tile(sin_h, (1, N_HEADS))


def kernel(x_prompt, x_sample, cache_k, cache_v, state_ssm_re, state_ssm_im, page_table, norm1_g, w_in,
           q_norm_g, k_norm_g, ssm_a_re, ssm_a_im, ssm_log_dt, ssm_b_re, ssm_b_im, ssm_c_re, ssm_c_im,
           ssm_d, w_attn_up, w_ssm_glu, w_out, norm2_g, w_router, b_router, w_gate, b_gate, w_up, b_up,
           w_down, b_down):
    batch, seq, d_model = x_prompt.shape
    dec_batch = x_sample.shape[0]
    n_pages = page_table.shape[1]
    page = cache_k.shape[2]
    past_len = n_pages * page
    assert x_sample.shape[1] == 1 and w_in.shape[0] == 1
    assert seq % ROW_TILE == 0 and seq % MOBA_BLOCK == 0 and past_len % MOBA_BLOCK == 0
    assert page == LANES, "the sample attention keeps one page per lane tile"
    t_prompt = batch * seq
    heads = (N_HEADS, HEAD_DIM)

    xp = x_prompt.reshape(t_prompt, d_model)
    xs = x_sample.reshape(dec_batch, d_model)
    hd, hdt = _head_indicator()
    w_in_b = w_in[0].astype(BF16)
    q_gain = jnp.tile(q_norm_g, (1, N_HEADS))
    k_gain = jnp.tile(k_norm_g, (1, N_HEADS))
    cos_p, sin_p = _rope_tables(jnp.arange(seq, dtype=I32))
    cos_s, sin_s = _rope_tables(jnp.full((dec_batch,), past_len, I32))
    n_pairs = (t_prompt + dec_batch) * TOP_K
    n_tiles = n_pairs // MOE_TM + N_EXPERTS
    k_p, _, ga_p, gs_p, qt_p, kt_p, vt_p, ut_p, x_sorted = _inproj(
        xp, norm1_g, w_in_b, cos_p, sin_p, q_gain, k_gain, hd, hdt, ROW_TILE, zero_rows=n_tiles * MOE_TM)
    _, u_s, ga_s, gs_s, qt_s, kt_s, vt_s, _ = _inproj(xs, norm1_g, w_in_b, cos_s, sin_s, q_gain, k_gain,
                                                      hd, hdt, dec_batch)

    attn_p = _attn_prompt(qt_p, k_p, vt_p, batch, seq)
    feature_major = lambda c: c[0].transpose(0, 2, 3, 1).reshape(c.shape[1], ATTN_W, page)
    attn_s = _attn_sample(page_table, qt_s[0], kt_s[0], vt_s[0], feature_major(cache_k),
                          feature_major(cache_v))

    assert seq % S5_CHUNK == 0
    lam_re, lam_im, dt, ab_re, ab_im, bb_re, bb_im = _s5_discretize(
        ssm_a_re[0], ssm_a_im[0], ssm_log_dt[0], ssm_b_re[0], ssm_b_im[0])
    lags, w1_re, w1_im, w2_re, w2_im, al_re, al_im = _s5_chunk_operators(
        lam_re, lam_im, dt, bb_re, bb_im, ssm_c_re[0], ssm_c_im[0])
    d_rep = jnp.broadcast_to(ssm_d[0][:, None, None], (SSM_W, batch, S5_CHUNK))
    y_t, hp_re, hp_im = _s5_prompt(ut_p.transpose(1, 0, 2), lags, w1_re, w1_im, w2_re, w2_im, al_re, al_im,
                                   d_rep)
    yt_p = y_t.transpose(1, 0, 2)
    width = N_GROUPS * STATE_DIM
    y_s, hs_re, hs_im = _s5_sample(
        u_s, state_ssm_re[0].reshape(dec_batch, width), state_ssm_im[0].reshape(dec_batch, width),
        _block_diag(bb_re.transpose(0, 2, 1)).astype(BF16), _block_diag(bb_im.transpose(0, 2, 1)).astype(BF16),
        _block_diag(ssm_c_re[0].transpose(0, 2, 1)).astype(BF16),
        _block_diag(ssm_c_im[0].transpose(0, 2, 1)).astype(BF16),
        ab_re.reshape(1, width), ab_im.reshape(1, width), ssm_d)

    w_router_pad = jnp.concatenate([w_router[0], jnp.zeros((d_model, LANES - N_EXPERTS), F32)], axis=1)
    merge_w = (w_attn_up[0].astype(BF16), w_ssm_glu[0].astype(BF16), w_out[0].astype(BF16), norm2_g,
               w_router_pad, b_router[0].reshape(N_EXPERTS, 1))
    h_p, xn_p, idx_p, rank_p, wr_p, cnt_p = _merge(xp, attn_p, yt_p, ga_p, gs_p, *merge_w,
                                                   jnp.zeros((N_EXPERTS, LANES), F32), ROW_TILE)
    h_s, xn_s, idx_s, rank_s, wr_s, cnt_all = _merge(xs, attn_s, y_s.T[None], ga_s, gs_s, *merge_w, cnt_p,
                                                     dec_batch)

    starts, tile_expert, n_used = _routing_tables(cnt_all[:, 0].astype(I32), n_tiles)
    def sorted_rows(idx, rank):
        idx, pos = idx[:TOP_K], rank[:TOP_K]
        for e in range(N_EXPERTS):
            pos = pos + jnp.where(idx == e, starts[e], 0)
        return jnp.concatenate([pos, jnp.zeros_like(pos)], axis=0)

    pos_p, pos_s = sorted_rows(idx_p, rank_p), sorted_rows(idx_s, rank_s)
    x_sorted = _dispatch(pos_p, xn_p, x_sorted, MOE_TOK)
    x_sorted = _dispatch(pos_s, xn_s, x_sorted, dec_batch)
    y_sorted = _moe_grouped(tile_expert, n_used, x_sorted, w_gate[0], b_gate[0], w_up[0], b_up[0],
                            w_down[0], b_down[0])
    out_p = _combine(pos_p, h_p, wr_p, y_sorted, MOE_TOK)
    out_s = _combine(pos_s, h_s, wr_s, y_sorted, dec_batch)

    per_seq = lambda a, n, s: a.reshape((1, n) + heads + (s,)).transpose(0, 1, 4, 2, 3)
    new_rows = lambda a: per_seq(a, 1, dec_batch).transpose(0, 2, 1, 3, 4)
    return (out_p.reshape(batch, seq, d_model),
            out_s.reshape(dec_batch, 1, d_model),
            per_seq(kt_p, batch, seq),
            per_seq(vt_p, batch, seq),
            hp_re.transpose(1, 0, 2)[None],
            hp_im.transpose(1, 0, 2)[None],
            new_rows(kt_s),
            new_rows(vt_s),
            hs_re.reshape(1, dec_batch, N_GROUPS, STATE_DIM),
            hs_im.reshape(1, dec_batch, N_GROUPS, STATE_DIM))
```

```python
import functools
import math

import jax
import jax.numpy as jnp
from jax import lax
from jax.experimental import pallas as pl
from jax.experimental.pallas import tpu as pltpu

F32 = jnp.float32
BF16 = jnp.bfloat16
I32 = jnp.int32

D_MODEL = 1024
N_HEADS = 8
HEAD_DIM = 64
ATTN_W = N_HEADS * HEAD_DIM
MOBA_BLOCK = 256
MOBA_TOP = 3
ROPE_THETA = 10000.0
SSM_W = 512
GROUP_CH = 16
N_GROUPS = 32
STATE_DIM = 64
N_EXPERTS = 32
TOP_K = 4
SWIGLU_ALPHA = 1.702
SWIGLU_LIMIT = 7.0
EPS = 1e-6
IN_W = 4096
NEG_BIG = -1e9

LANES = 128
SUBLANES = 8
ROW_TILE = 512
S5_CHUNK = LANES
MOE_TM = 512
MOE_TOK = 256
FF_CHUNK = 512
VMEM_LIMIT = 56 * 1024 * 1024


def _cparams(sem):
    return pltpu.CompilerParams(dimension_semantics=sem, vmem_limit_bytes=VMEM_LIMIT)


def _dot(a, b):
    return jnp.dot(a, b, preferred_element_type=F32)


def _split_hi_lo(a):
    hi = a.astype(BF16)
    lo = (a - hi.astype(F32)).astype(BF16)
    return hi, lo


def _dot_hilo(a, b_bf16):
    hi, lo = _split_hi_lo(a)
    return _dot(hi, b_bf16) + _dot(lo, b_bf16)


def _head_indicator():
    c = jnp.arange(ATTN_W)[:, None] // HEAD_DIM
    h = jnp.arange(LANES)[None, :]
    hd = (c == h).astype(BF16)
    return hd, hd.T


def _head_norm_rope(t, gain, cos, sin_signed, hd, hdt):
    ssq = _dot_hilo(t * t, hd)
    rstd = lax.rsqrt(ssq * (1.0 / HEAD_DIM) + EPS)
    t = t * _dot_hilo(rstd, hdt) * gain
    lane = lax.broadcasted_iota(I32, (t.shape[0], LANES), 1)
    first_half = (lane % HEAD_DIM) < (HEAD_DIM // 2)
    outs = []
    for s in range(ATTN_W // LANES):
        ts = t[:, s * LANES:(s + 1) * LANES]
        partner = jnp.where(first_half,
                            pltpu.roll(ts, LANES - HEAD_DIM // 2, axis=1),
                            pltpu.roll(ts, HEAD_DIM // 2, axis=1))
        outs.append(ts * cos[:, s * LANES:(s + 1) * LANES]
                    + partner * sin_signed[:, s * LANES:(s + 1) * LANES])
    return jnp.concatenate(outs, axis=1)


def _inproj_kernel(zero_chunks, x_ref, g1_ref, w_ref, cos_ref, sin_ref, qg_ref, kg_ref, hd_ref, hdt_ref,
                   k_ref, u_ref, ga_ref, gs_ref, qt_ref, kt_ref, vt_ref, ut_ref, *zero_refs):
    zero_copies = []
    if zero_chunks:
        z_hbm, zbuf, zsem = zero_refs
        step = pl.program_id(0)
        chunk = zbuf.shape[0]

        @pl.when(step == 0)
        def _():
            zbuf[...] = jnp.zeros(zbuf.shape, F32)

        for c in range(zero_chunks):
            row0 = pl.multiple_of((step * zero_chunks + c) * chunk, SUBLANES)
            zero_copies.append(pltpu.make_async_copy(zbuf, z_hbm.at[pl.ds(row0, chunk)], zsem))
        for cp in zero_copies:
            cp.start(priority=1)
    x = x_ref[...]
    xn = x * lax.rsqrt(jnp.mean(x * x, axis=-1, keepdims=True) + EPS) * g1_ref[...]
    xb = xn.astype(BF16)
    hd = hd_ref[...]
    hdt = hdt_ref[...]
    cos = cos_ref[...]
    sin = sin_ref[...]
    q = _dot(xb, w_ref[:, 0:ATTN_W])
    qt_ref[0] = _head_norm_rope(q, qg_ref[...], cos, sin, hd, hdt).T
    k = _head_norm_rope(_dot(xb, w_ref[:, ATTN_W:2 * ATTN_W]), kg_ref[...], cos, sin, hd, hdt)
    k_ref[...] = k
    kt_ref[0] = k.T
    vt_ref[0] = _dot(xb, w_ref[:, 2 * ATTN_W:3 * ATTN_W]).T
    u = _dot(xb, w_ref[:, 3 * ATTN_W:3 * ATTN_W + SSM_W])
    u_ref[...] = u
    ut_ref[0] = u.T
    gates = 3 * ATTN_W + SSM_W
    ga_ref[...] = _dot(xb, w_ref[:, gates:gates + D_MODEL])
    gs_ref[...] = _dot(xb, w_ref[:, gates + D_MODEL:gates + 2 * D_MODEL])
    for cp in zero_copies:
        cp.wait()


def _inproj(x, norm1_g, w_in_bf16, cos_tab, sin_tab, q_gain, k_gain, hd, hdt, tile, zero_rows=0):
    t = x.shape[0]
    steps = t // tile
    period = cos_tab.shape[0] // tile
    row = lambda w: pl.BlockSpec((tile, w), lambda i: (i, 0))
    const = lambda shape: pl.BlockSpec(shape, lambda i: (0, 0))
    pos = pl.BlockSpec((tile, ATTN_W), lambda i: (i % period, 0))
    feature_major = pl.BlockSpec((1, ATTN_W, tile), lambda i: (i // period, 0, i % period))
    out_specs = [row(ATTN_W)] * 2 + [row(D_MODEL)] * 2 + [feature_major] * 4
    out_shape = ([jax.ShapeDtypeStruct((t, ATTN_W), F32)] * 2
                 + [jax.ShapeDtypeStruct((t, D_MODEL), F32)] * 2
                 + [jax.ShapeDtypeStruct((t // (period * tile), ATTN_W, period * tile), F32)] * 4)
    zero_chunks, scratch = 0, []
    if zero_rows:
        assert zero_rows % (steps * SUBLANES) == 0
        per_step = zero_rows // steps
        chunk = max(c for c in range(SUBLANES, ROW_TILE + 1, SUBLANES) if per_step % c == 0)
        zero_chunks = per_step // chunk
        out_specs = out_specs + [pl.BlockSpec(memory_space=pl.ANY)]
        out_shape = out_shape + [jax.ShapeDtypeStruct((zero_rows, D_MODEL), F32)]
        scratch = [pltpu.VMEM((chunk, D_MODEL), F32), pltpu.SemaphoreType.DMA(())]
    return pl.pallas_call(
        functools.partial(_inproj_kernel, zero_chunks),
        grid=(steps,),
        in_specs=[row(D_MODEL), const((1, D_MODEL)), const((D_MODEL, IN_W)), pos, pos,
                  const((1, ATTN_W)), const((1, ATTN_W)), const((ATTN_W, LANES)), const((LANES, ATTN_W))],
        out_specs=out_specs,
        out_shape=out_shape,
        scratch_shapes=scratch,
        compiler_params=_cparams(("arbitrary",)),
        name="inproj",
    )(x, norm1_g, w_in_bf16, cos_tab, sin_tab, q_gain, k_gain, hd, hdt)


def _block_penalty_t(gate_t, n_past, own_blk):
    blk = lax.broadcasted_iota(I32, gate_t.shape, 0)
    valid = blk < n_past
    g = jnp.where(valid, gate_t, -jnp.inf)
    cnt = jnp.zeros(gate_t.shape, I32)
    for m in range(gate_t.shape[0]):
        gm = g[m:m + 1, :]
        beats = jnp.where(gm > g, 1, jnp.where((gm == g) & (blk > m), 1, 0))
        cnt = cnt + beats
    keep = (valid & (cnt < MOBA_TOP)) | (blk == own_blk)
    return jnp.where(keep, 0.0, NEG_BIG)


ATTN_PAIRS = 2


def _attn_prompt_kernel(qt_ref, k_ref, vt_ref, o_ref, kaug_ref, vtb_ref, kmean_ref):
    j = pl.program_id(2)
    blk = MOBA_BLOCK
    nblk = k_ref.shape[0] // blk
    pairs = k_ref.shape[1] // LANES

    @pl.when(j == 0)
    def _():
        lane = lax.broadcasted_iota(I32, (blk, LANES), 1)
        for n in range(nblk):
            rows = slice(n * blk, (n + 1) * blk)
            onehot = jnp.where(lane == n, 1.0, 0.0).astype(BF16)
            for g in range(pairs):
                k = k_ref[rows, g * LANES:(g + 1) * LANES]
                kaug_ref[g, rows, 0:LANES] = k.astype(BF16)
                kaug_ref[g, rows, LANES:2 * LANES] = onehot
                kmean_ref[g, n:n + 1, :] = jnp.mean(k, axis=0, keepdims=True)
        vtb_ref[...] = vt_ref[0].astype(BF16)

    feat = lax.broadcasted_iota(I32, (LANES, blk), 0)
    zpad = jnp.zeros((LANES - nblk, blk), F32)
    qaug = []
    for g in range(pairs):
        q_t = qt_ref[0, g * LANES:(g + 1) * LANES, :]
        km_hi, km_lo = _split_hi_lo(kmean_ref[g])
        for h in range(2):
            qh = jnp.where(feat >= HEAD_DIM if h == 1 else feat < HEAD_DIM, q_t, 0.0)
            q_hi, q_lo = _split_hi_lo(qh)
            gate_t = _dot(km_hi, q_hi) + _dot(km_hi, q_lo) + _dot(km_lo, q_hi)
            pen_t = jnp.concatenate([_block_penalty_t(gate_t, j, j), zpad], axis=0)
            qaug.append((g, g * LANES + h * HEAD_DIM,
                         jnp.concatenate([(qh * (1.0 / math.sqrt(HEAD_DIM))).astype(BF16),
                                          pen_t.astype(BF16)], axis=0)))

    key_i = lax.broadcasted_iota(I32, (blk, blk), 0)
    qry_i = lax.broadcasted_iota(I32, (blk, blk), 1)
    future = key_i > qry_i

    def attend(jq):
        past = jq * blk
        outs = []
        for g, row0, qa in qaug:
            s_diag = jnp.where(future, NEG_BIG, _dot(kaug_ref[g, past:past + blk, :], qa))
            m = jnp.max(s_diag, axis=0, keepdims=True)
            if jq > 0:
                s_past = _dot(kaug_ref[g, 0:past, :], qa)
                m = jnp.maximum(m, jnp.max(s_past, axis=0, keepdims=True))
                p_past = jnp.exp(s_past - m)
            p_diag = jnp.exp(s_diag - m)
            l = jnp.sum(p_diag, axis=0, keepdims=True)
            feats = slice(row0, row0 + HEAD_DIM)
            pv = _dot(vtb_ref[feats, past:past + blk], p_diag.astype(BF16))
            if jq > 0:
                l = l + jnp.sum(p_past, axis=0, keepdims=True)
                pv = pv + _dot(vtb_ref[feats, 0:past], p_past.astype(BF16))
            outs.append(pv / l)
        o_ref[...] = jnp.concatenate(outs, axis=0).T

    for jq in range(nblk):
        pl.when(j == jq)(functools.partial(attend, jq))


def _attn_prompt(qt, k, vt, batch, seq):
    nq = seq // MOBA_BLOCK
    width = ATTN_PAIRS * LANES
    return pl.pallas_call(
        _attn_prompt_kernel,
        grid=(batch, ATTN_W // width, nq),
        in_specs=[pl.BlockSpec((1, width, MOBA_BLOCK), lambda b, hp, j: (b, hp, j)),
                  pl.BlockSpec((seq, width), lambda b, hp, j: (b, hp)),
                  pl.BlockSpec((1, width, seq), lambda b, hp, j: (b, hp, 0))],
        out_specs=pl.BlockSpec((MOBA_BLOCK, width), lambda b, hp, j: (b * nq + j, hp)),
        out_shape=jax.ShapeDtypeStruct((batch * seq, ATTN_W), F32),
        scratch_shapes=[pltpu.VMEM((ATTN_PAIRS, seq, 2 * LANES), BF16), pltpu.VMEM((width, seq), BF16),
                        pltpu.VMEM((ATTN_PAIRS, nq, LANES), F32)],
        compiler_params=_cparams(("parallel", "parallel", "arbitrary")),
        name="attn_prompt",
    )(qt, k, vt)


def _attn_sample_kernel(n_pages, pt_ref, qt_ref, kt_ref, vt_ref, *rest):
    del pt_ref
    k_pages = rest[:n_pages]
    v_pages = rest[n_pages:2 * n_pages]
    o_ref = rest[2 * n_pages]
    b = pl.program_id(0)
    page = k_pages[0].shape[2]
    ppb = MOBA_BLOCK // page
    nblk = n_pages // ppb
    seq_i = lax.broadcasted_iota(I32, (qt_ref.shape[1], page), 0)
    lane_i = lax.broadcasted_iota(I32, (qt_ref.shape[1], page), 1)
    pick_all = jnp.where(seq_i == b, 1.0, 0.0).astype(BF16)
    pick_lane0 = jnp.where((seq_i == b) & (lane_i == 0), 1.0, 0.0).astype(BF16)
    qb = _dot_hilo(qt_ref[...], pick_all) * (1.0 / math.sqrt(HEAD_DIM))
    k_new = _dot_hilo(kt_ref[...], pick_lane0)
    v_new = _dot_hilo(vt_ref[...], pick_lane0)

    def head_sums(x):
        return jnp.sum(x.reshape(N_HEADS, HEAD_DIM, x.shape[1]), axis=1)

    def per_feature(p):
        return jnp.broadcast_to(p[:, None, :], (N_HEADS, HEAD_DIM, p.shape[1])).reshape(ATTN_W, p.shape[1])

    s_pages = [head_sums(k_pages[p][0] * qb) for p in range(n_pages)]
    lane = lax.broadcasted_iota(I32, s_pages[0].shape, 1)
    s_new = jnp.where(lane == 0, head_sums(k_new * qb), NEG_BIG)

    gates = []
    for n in range(nblk):
        blk_sum = s_pages[n * ppb]
        for i in range(1, ppb):
            blk_sum = blk_sum + s_pages[n * ppb + i]
        gates.append(jnp.sum(blk_sum, axis=1, keepdims=True))
    pens = []
    for n in range(nblk):
        cnt = jnp.zeros(gates[n].shape, I32)
        for m in range(nblk):
            if m == n:
                continue
            beats = (gates[m] >= gates[n]) if m < n else (gates[m] > gates[n])
            cnt = cnt + jnp.where(beats, 1, 0)
        pens.append(jnp.where(cnt < MOBA_TOP, 0.0, NEG_BIG))

    s_adj = [s_pages[p] + pens[p // ppb] for p in range(n_pages)]
    m = s_new
    for s in s_adj:
        m = jnp.maximum(m, s)
    m = jnp.max(m, axis=1, keepdims=True)
    e_new = jnp.exp(s_new - m)
    es = [jnp.exp(s - m) for s in s_adj]
    total = e_new
    for e in es:
        total = total + e
    inv = 1.0 / jnp.sum(total, axis=1, keepdims=True)
    acc = v_new * per_feature(e_new * inv)
    for p in range(n_pages):
        acc = acc + v_pages[p][0] * per_feature(es[p] * inv)
    a_hi, a_lo = _split_hi_lo(acc)
    ones = jnp.ones((SUBLANES, page), BF16)
    nt = (((1,), (1,)), ((), ()))
    row = (lax.dot_general(ones, a_hi, nt, preferred_element_type=F32)
           + lax.dot_general(ones, a_lo, nt, preferred_element_type=F32))
    o_ref[pl.ds(b, 1), :] = row[0:1, :]


def _attn_sample(page_table, qt, kt, vt, cache_kt, cache_vt):
    dec_batch, n_pages = page_table.shape
    page = cache_kt.shape[2]

    def page_spec(p):
        return pl.BlockSpec((1, ATTN_W, page), lambda b, pt: (pt[b, p], 0, 0))

    own = pl.BlockSpec((ATTN_W, dec_batch), lambda b, pt: (0, 0))
    grid_spec = pltpu.PrefetchScalarGridSpec(
        num_scalar_prefetch=1,
        grid=(dec_batch,),
        in_specs=[own] * 3 + [page_spec(p) for p in range(n_pages)] * 2,
        out_specs=pl.BlockSpec((dec_batch, ATTN_W), lambda b, pt: (0, 0)),
    )
    return pl.pallas_call(
        functools.partial(_attn_sample_kernel, n_pages),
        grid_spec=grid_spec,
        out_shape=jax.ShapeDtypeStruct((dec_batch, ATTN_W), F32),
        compiler_params=_cparams(("arbitrary",)),
        name="attn_sample",
    )(page_table, qt, kt, vt, *([cache_kt] * n_pages), *([cache_vt] * n_pages))


def _s5_discretize(a_re, a_im, log_dt, b_re, b_im):
    lam_re = a_re.astype(F32)
    lam_im = a_im.astype(F32)
    dt = jnp.exp(log_dt.astype(F32))[:, None]
    mag = jnp.exp(lam_re * dt)
    ab_re = mag * jnp.cos(lam_im * dt)
    ab_im = mag * jnp.sin(lam_im * dt)
    den = lam_re * lam_re + lam_im * lam_im
    n_re, n_im = ab_re - 1.0, ab_im
    f_re = (n_re * lam_re + n_im * lam_im) / den
    f_im = (n_im * lam_re - n_re * lam_im) / den
    br, bi = b_re.astype(F32), b_im.astype(F32)
    bb_re = f_re[..., None] * br - f_im[..., None] * bi
    bb_im = f_re[..., None] * bi + f_im[..., None] * br
    return lam_re, lam_im, dt, ab_re, ab_im, bb_re, bb_im


def _s5_chunk_operators(lam_re, lam_im, dt, bb_re, bb_im, c_re, c_im):
    L = S5_CHUNK
    hp = lax.Precision.HIGHEST
    steps = jnp.arange(L + 1, dtype=F32)[None, None, :]
    rate = dt[:, :, None] * steps
    pmag = jnp.exp(lam_re[:, :, None] * rate)
    pw_re = pmag * jnp.cos(lam_im[:, :, None] * rate)
    pw_im = pmag * jnp.sin(lam_im[:, :, None] * rate)
    cr, ci = c_re.astype(F32), c_im.astype(F32)
    btr, bti = bb_re.transpose(0, 2, 1), bb_im.transpose(0, 2, 1)
    bc_re = btr[:, :, None, :] * cr[:, None, :, :] - bti[:, :, None, :] * ci[:, None, :, :]
    bc_im = btr[:, :, None, :] * ci[:, None, :, :] + bti[:, :, None, :] * cr[:, None, :, :]
    pairs = GROUP_CH * GROUP_CH
    lags = (jnp.einsum('gxp,gpj->gxj', bc_re.reshape(N_GROUPS, pairs, STATE_DIM), pw_re[:, :, :L], precision=hp)
            - jnp.einsum('gxp,gpj->gxj', bc_im.reshape(N_GROUPS, pairs, STATE_DIM), pw_im[:, :, :L],
                         precision=hp))
    rev_re = pw_re[:, :, L - 1::-1].transpose(0, 2, 1)
    rev_im = pw_im[:, :, L - 1::-1].transpose(0, 2, 1)
    w1_re = rev_re[:, None, :, :] * btr[:, :, None, :] - rev_im[:, None, :, :] * bti[:, :, None, :]
    w1_im = rev_re[:, None, :, :] * bti[:, :, None, :] + rev_im[:, None, :, :] * btr[:, :, None, :]
    w1_re = w1_re.reshape(N_GROUPS, GROUP_CH * L, STATE_DIM)
    w1_im = w1_im.reshape(N_GROUPS, GROUP_CH * L, STATE_DIM)
    ctr, cti = cr.transpose(0, 2, 1), ci.transpose(0, 2, 1)
    nx_re, nx_im = pw_re[:, :, 1:], pw_im[:, :, 1:]
    w2_re = ctr[:, :, :, None] * nx_re[:, :, None, :] - cti[:, :, :, None] * nx_im[:, :, None, :]
    w2_im = -(ctr[:, :, :, None] * nx_im[:, :, None, :] + cti[:, :, :, None] * nx_re[:, :, None, :])
    w2_re = w2_re.reshape(N_GROUPS, STATE_DIM, GROUP_CH * L)
    w2_im = w2_im.reshape(N_GROUPS, STATE_DIM, GROUP_CH * L)
    return (lags, w1_re.astype(BF16), w1_im.astype(BF16), w2_re.astype(BF16), w2_im.astype(BF16),
            pw_re[:, None, :, L], pw_im[:, None, :, L])


def _s5_prompt_kernel(x_ref, lag_ref, w1r_ref, w1i_ref, w2r_ref, w2i_ref, ar_ref, ai_ref, d_ref,
                      y_ref, hr_ref, hi_ref, toep_ref, u_ref, pre_ref, pim_ref):
    nch, batch, seq = x_ref.shape
    L = S5_CHUNK
    n_chunks = seq // L

    s_i = lax.broadcasted_iota(I32, (L, L), 0)
    t_i = lax.broadcasted_iota(I32, (L, L), 1)
    causal = t_i >= s_i
    for cp in range(nch):
        for c in range(nch):
            first = jnp.broadcast_to(lag_ref[0, cp * nch + c:cp * nch + c + 1, :], (L, L))
            tile = pltpu.roll(first, 0, 1, stride=1, stride_axis=0)
            toep_ref[cp * L:(cp + 1) * L, c * L:(c + 1) * L] = jnp.where(causal, tile, 0.0).astype(BF16)

    for k in range(n_chunks):
        for cp in range(nch):
            u_ref[k * batch:(k + 1) * batch, cp * L:(cp + 1) * L] = x_ref[cp, :, k * L:(k + 1) * L]
    ub = u_ref[...].astype(BF16)

    d_re = _dot(ub, w1r_ref[0])
    d_im = _dot(ub, w1i_ref[0])
    a_re = ar_ref[0]
    a_im = ai_ref[0]
    h_re = jnp.zeros((batch, STATE_DIM), F32)
    h_im = jnp.zeros((batch, STATE_DIM), F32)
    for k in range(n_chunks):
        rows = slice(k * batch, (k + 1) * batch)
        pre_ref[rows, :] = h_re
        pim_ref[rows, :] = h_im
        h_re, h_im = (a_re * h_re - a_im * h_im + d_re[rows, :], a_re * h_im + a_im * h_re + d_im[rows, :])
    hr_ref[0] = h_re
    hi_ref[0] = h_im

    y = (_dot(ub, toep_ref[...]) + _dot(pre_ref[...].astype(BF16), w2r_ref[0])
         + _dot(pim_ref[...].astype(BF16), w2i_ref[0]))
    for k in range(n_chunks):
        for c in range(nch):
            lanes = slice(k * L, (k + 1) * L)
            y_ref[c, :, lanes] = (y[k * batch:(k + 1) * batch, c * L:(c + 1) * L]
                                  + d_ref[c] * x_ref[c, :, lanes])


def _s5_prompt(x, lags, w1_re, w1_im, w2_re, w2_im, al_re, al_im, d_rep):
    width, batch, seq = x.shape
    g = width // GROUP_CH
    cols = GROUP_CH * S5_CHUNK
    rows = seq // S5_CHUNK * batch
    blk = lambda s: pl.BlockSpec((1,) + s, lambda i: (i, 0, 0))
    chan = pl.BlockSpec((GROUP_CH, batch, seq), lambda i: (i, 0, 0))
    return pl.pallas_call(
        _s5_prompt_kernel,
        grid=(g,),
        in_specs=[chan, blk((GROUP_CH * GROUP_CH, S5_CHUNK)), blk((cols, STATE_DIM)), blk((cols, STATE_DIM)),
                  blk((STATE_DIM, cols)), blk((STATE_DIM, cols)), blk((1, STATE_DIM)), blk((1, STATE_DIM)),
                  pl.BlockSpec((GROUP_CH, batch, S5_CHUNK), lambda i: (i, 0, 0))],
        out_specs=[chan, blk((batch, STATE_DIM)), blk((batch, STATE_DIM))],
        out_shape=[jax.ShapeDtypeStruct((width, batch, seq), F32),
                   jax.ShapeDtypeStruct((g, batch, STATE_DIM), F32),
                   jax.ShapeDtypeStruct((g, batch, STATE_DIM), F32)],
        scratch_shapes=[pltpu.VMEM((cols, cols), BF16), pltpu.VMEM((rows, cols), F32),
                        pltpu.VMEM((rows, STATE_DIM), F32), pltpu.VMEM((rows, STATE_DIM), F32)],
        compiler_params=_cparams(("parallel",)),
        name="s5_prompt",
    )(x, lags, w1_re, w1_im, w2_re, w2_im, al_re, al_im, d_rep)


def _s5_sample_kernel(u_ref, h0r_ref, h0i_ref, bdr_ref, bdi_ref, cdr_ref, cdi_ref, ar_ref, ai_ref, d_ref,
                      y_ref, hr_ref, hi_ref):
    u = u_ref[...]
    ub = u.astype(BF16)
    a_re = ar_ref[...]
    a_im = ai_ref[...]
    h0r = h0r_ref[...]
    h0i = h0i_ref[...]
    h_re = a_re * h0r - a_im * h0i + _dot(ub, bdr_ref[...])
    h_im = a_re * h0i + a_im * h0r + _dot(ub, bdi_ref[...])
    hr_ref[...] = h_re
    hi_ref[...] = h_im
    y_ref[...] = (_dot(h_re.astype(BF16), cdr_ref[...]) - _dot(h_im.astype(BF16), cdi_ref[...])
                  + d_ref[...] * u)


def _s5_sample(u_s, h0_re, h0_im, bd_re, bd_im, cd_re, cd_im, ab_re, ab_im, d_skip):
    n = u_s.shape[0]
    width = N_GROUPS * STATE_DIM
    return pl.pallas_call(
        _s5_sample_kernel,
        out_shape=[jax.ShapeDtypeStruct((n, SSM_W), F32), jax.ShapeDtypeStruct((n, width), F32),
                   jax.ShapeDtypeStruct((n, width), F32)],
        compiler_params=pltpu.CompilerParams(vmem_limit_bytes=VMEM_LIMIT),
        name="s5_sample",
    )(u_s, h0_re, h0_im, bd_re, bd_im, cd_re, cd_im, ab_re, ab_im, d_skip)


def _block_diag(m):
    g, r, c = m.shape
    eye = jnp.eye(g, dtype=m.dtype)
    return (m[:, :, None, :] * eye[:, None, :, None]).reshape(g * r, g * c)


def _merge_kernel(x_ref, attn_ref, yssm_ref, ga_ref, gs_ref, wau_ref, wglu_ref, wout_ref, g2_ref,
                  wr_ref, br_ref, su_ref, cnt_in_ref,
                  h_ref, xn_ref, idx_ref, rank_ref, w_ref, cnt_ref, run_ref):
    @pl.when(pl.program_id(0) == 0)
    def _():
        run_ref[...] = cnt_in_ref[...]

    branch_attn = _dot(attn_ref[...].astype(BF16), wau_ref[...])
    glu = _dot(jax.nn.gelu(yssm_ref[0].T).astype(BF16), wglu_ref[...])
    branch_ssm = glu[:, :D_MODEL] * jax.nn.sigmoid(glu[:, D_MODEL:])
    merged = jax.nn.sigmoid(ga_ref[...]) * branch_attn + jax.nn.sigmoid(gs_ref[...]) * branch_ssm
    h = x_ref[...] + _dot(merged.astype(BF16), wout_ref[...])
    h_ref[...] = h
    xn = h * lax.rsqrt(jnp.mean(h * h, axis=-1, keepdims=True) + EPS) * g2_ref[...]
    xn_ref[...] = xn

    x_hi, x_lo = _split_hi_lo(xn)
    w_hi, w_lo = _split_hi_lo(wr_ref[...])
    logits = _dot(x_hi, w_hi) + _dot(x_lo, w_hi) + _dot(x_hi, w_lo)
    lt = logits.T[0:N_EXPERTS, :] + br_ref[...]
    rows = lt.shape[1]
    e_iota = lax.broadcasted_iota(I32, lt.shape, 0).astype(F32)
    tops, idxs, hots = [], [], []
    for _ in range(TOP_K):
        m = jnp.max(lt, axis=0, keepdims=True)
        idx = jnp.min(jnp.where(lt == m, e_iota, float(N_EXPERTS)), axis=0, keepdims=True)
        hot = e_iota == idx
        tops.append(m)
        idxs.append(idx)
        hots.append(hot)
        lt = jnp.where(hot, -jnp.inf, lt)
    exps = [jnp.exp(t - tops[0]) for t in tops]
    denom = exps[0] + exps[1] + exps[2] + exps[3]

    member = jnp.zeros(lt.shape, F32)
    for hot in hots:
        member = member + jnp.where(hot, 1.0, 0.0)
    before = run_ref[:, 0:1] + _dot(member.astype(BF16), su_ref[...])
    ranks = [jnp.sum(jnp.where(hot, before, 0.0), axis=0, keepdims=True) for hot in hots]
    run_ref[...] = run_ref[...] + jnp.sum(member, axis=1, keepdims=True)
    cnt_ref[...] = run_ref[...]

    zrows = jnp.zeros((SUBLANES - TOP_K, rows), F32)
    idx_ref[...] = jnp.concatenate(idxs + [zrows], axis=0).astype(I32)
    rank_ref[...] = jnp.concatenate(ranks + [zrows], axis=0).astype(I32)
    w_t = jnp.concatenate([e / denom for e in exps] + [jnp.zeros((LANES - TOP_K, rows), F32)], axis=0)
    w_ref[...] = w_t.T


def _merge(x, attn, y_ssm_t, g_attn, g_ssm, w_attn_up, w_ssm_glu, w_out, norm2_g, w_router_pad,
           b_router_col, cnt_in, tile):
    t = x.shape[0]
    period = y_ssm_t.shape[2] // tile
    row = lambda w: pl.BlockSpec((tile, w), lambda i: (i, 0))
    col = pl.BlockSpec((SUBLANES, tile), lambda i: (0, i))
    const = lambda shape: pl.BlockSpec(shape, lambda i: (0, 0))
    feature_major = pl.BlockSpec((1, SSM_W, tile), lambda i: (i // period, 0, i % period))
    strict_upper = jnp.triu(jnp.ones((tile, tile), F32), 1).astype(BF16)
    return pl.pallas_call(
        _merge_kernel,
        grid=(t // tile,),
        in_specs=[row(D_MODEL), row(ATTN_W), feature_major, row(D_MODEL), row(D_MODEL),
                  const((ATTN_W, D_MODEL)), const((SSM_W, 2 * D_MODEL)), const((D_MODEL, D_MODEL)),
                  const((1, D_MODEL)), const((D_MODEL, LANES)), const((N_EXPERTS, 1)),
                  const((tile, tile)), const((N_EXPERTS, LANES))],
        out_specs=[row(D_MODEL), row(D_MODEL), col, col, row(LANES), const((N_EXPERTS, LANES))],
        out_shape=[jax.ShapeDtypeStruct((t, D_MODEL), F32), jax.ShapeDtypeStruct((t, D_MODEL), F32),
                   jax.ShapeDtypeStruct((SUBLANES, t), I32), jax.ShapeDtypeStruct((SUBLANES, t), I32),
                   jax.ShapeDtypeStruct((t, LANES), F32), jax.ShapeDtypeStruct((N_EXPERTS, LANES), F32)],
        scratch_shapes=[pltpu.VMEM((N_EXPERTS, LANES), F32)],
        compiler_params=_cparams(("arbitrary",)),
        name="merge_router",
    )(x, attn, y_ssm_t, g_attn, g_ssm, w_attn_up, w_ssm_glu, w_out, norm2_g, w_router_pad, b_router_col,
      strict_upper, cnt_in)


def _dispatch_kernel(tok, pos_ref, x_ref, xs_in, xs_out, sem):
    del xs_in

    for r in range(tok):
        for k in range(TOP_K):
            pltpu.make_async_copy(x_ref.at[pl.ds(r, 1)], xs_out.at[pl.ds(pos_ref[k, r], 1)],
                                  sem).start(priority=k % 2)
    for _ in range(TOP_K):
        pltpu.make_async_copy(x_ref, xs_out.at[pl.ds(0, tok)], sem).wait()


def _dispatch(pos, xn, xs, tok):
    t, d_model = xn.shape
    return pl.pallas_call(
        functools.partial(_dispatch_kernel, tok),
        grid=(t // tok,),
        in_specs=[pl.BlockSpec((SUBLANES, tok), lambda i: (0, i), memory_space=pltpu.SMEM),
                  pl.BlockSpec((tok, d_model), lambda i: (i, 0)), pl.BlockSpec(memory_space=pl.ANY)],
        out_specs=pl.BlockSpec(memory_space=pl.ANY),
        out_shape=jax.ShapeDtypeStruct(xs.shape, xs.dtype),
        scratch_shapes=[pltpu.SemaphoreType.DMA(())],
        input_output_aliases={2: 0},
        compiler_params=_cparams(("arbitrary",)),
        name="moe_dispatch",
    )(pos, xn, xs)


def _moe_group_kernel(te_ref, nu_ref, x_ref, wg_ref, bg_ref, wu_ref, bu_ref, wd_ref, bd_ref, y_ref,
                      wgb_ref, wub_ref, wdb_ref):
    i = pl.program_id(0)

    @pl.when(i < nu_ref[0])
    def _():
        @pl.when((i == 0) | (te_ref[i] != te_ref[jnp.maximum(i - 1, 0)]))
        def _():
            wgb_ref[...] = wg_ref[0].astype(BF16)
            wub_ref[...] = wu_ref[0].astype(BF16)
            wdb_ref[...] = wd_ref[0].astype(BF16)

        x = x_ref[...].astype(BF16)
        d_ff = wgb_ref.shape[1]
        for c in range(d_ff // FF_CHUNK):
            cols = slice(c * FF_CHUNK, (c + 1) * FF_CHUNK)
            gate = _dot(x, wgb_ref[:, cols]) + bg_ref[0, :, cols]
            up = _dot(x, wub_ref[:, cols]) + bu_ref[0, :, cols]
            gate = jnp.minimum(gate, SWIGLU_LIMIT)
            up = jnp.clip(up, -SWIGLU_LIMIT, SWIGLU_LIMIT)
            hid = (up + 1.0) * (gate * jax.nn.sigmoid(SWIGLU_ALPHA * gate))
            part = _dot(hid.astype(BF16), wdb_ref[cols, :])
            if c == 0:
                y_ref[...] = part + bd_ref[0]
            else:
                y_ref[...] += part

    @pl.when(i >= nu_ref[0])
    def _():
        y_ref[...] = jnp.zeros(y_ref.shape, F32)


def _moe_grouped(tile_expert, n_used, xs, w_gate, b_gate, w_up, b_up, w_down, b_down):
    n_exp, d_model, d_ff = w_gate.shape
    n_tiles = xs.shape[0] // MOE_TM
    wspec = lambda r, c: pl.BlockSpec((1, r, c), lambda i, te, nu: (te[i], 0, 0))
    grid_spec = pltpu.PrefetchScalarGridSpec(
        num_scalar_prefetch=2,
        grid=(n_tiles,),
        in_specs=[pl.BlockSpec((MOE_TM, d_model), lambda i, te, nu: (jnp.minimum(i, nu[0] - 1), 0)),
                  wspec(d_model, d_ff), wspec(1, d_ff), wspec(d_model, d_ff), wspec(1, d_ff),
                  wspec(d_ff, d_model), wspec(1, d_model)],
        out_specs=pl.BlockSpec((MOE_TM, d_model), lambda i, te, nu: (i, 0)),
        scratch_shapes=[pltpu.VMEM((d_model, d_ff), BF16), pltpu.VMEM((d_model, d_ff), BF16),
                        pltpu.VMEM((d_ff, d_model), BF16)],
    )
    return pl.pallas_call(
        _moe_group_kernel,
        grid_spec=grid_spec,
        out_shape=jax.ShapeDtypeStruct((xs.shape[0], d_model), F32),
        compiler_params=_cparams(("arbitrary",)),
        name="moe_grouped",
    )(tile_expert, n_used, xs, w_gate, b_gate.reshape(n_exp, 1, d_ff), w_up, b_up.reshape(n_exp, 1, d_ff),
      w_down, b_down.reshape(n_exp, 1, d_model))


def _combine_kernel(tok, pos_ref, posn_ref, h_ref, w_ref, ys_hbm, o_ref, buf, sem):
    i = pl.program_id(0)
    n = pl.num_programs(0)
    slot = i % 2

    def gather(p_ref, s):
        for r in range(tok):
            for k in range(TOP_K):
                pltpu.make_async_copy(ys_hbm.at[pl.ds(p_ref[k, r], 1)], buf.at[s, k, pl.ds(r, 1)],
                                      sem.at[s]).start(priority=k % 2)

    @pl.when(i == 0)
    def _():
        gather(pos_ref, 0)

    @pl.when(i + 1 < n)
    def _():
        gather(posn_ref, 1 - slot)

    for k in range(TOP_K):
        pltpu.make_async_copy(ys_hbm.at[pl.ds(0, tok)], buf.at[slot, k], sem.at[slot]).wait()
    w = w_ref[...]
    out = h_ref[...]
    for k in range(TOP_K):
        out = out + w[:, k:k + 1] * buf[slot, k]
    o_ref[...] = out


def _combine(pos, h, w_rows, ys, tok):
    t, d_model = h.shape
    n = t // tok
    smem = lambda imap: pl.BlockSpec((SUBLANES, tok), imap, memory_space=pltpu.SMEM)
    row = lambda w: pl.BlockSpec((tok, w), lambda i: (i, 0))
    return pl.pallas_call(
        functools.partial(_combine_kernel, tok),
        grid=(n,),
        in_specs=[smem(lambda i: (0, i)), smem(lambda i: (0, jnp.minimum(i + 1, n - 1))),
                  row(d_model), row(LANES), pl.BlockSpec(memory_space=pl.ANY)],
        out_specs=row(d_model),
        out_shape=jax.ShapeDtypeStruct((t, d_model), F32),
        scratch_shapes=[pltpu.VMEM((2, TOP_K, tok, d_model), F32), pltpu.SemaphoreType.DMA((2,))],
        compiler_params=_cparams(("arbitrary",)),
        name="moe_combine",
    )(pos, pos, h, w_rows, ys)


def _routing_tables(counts, n_tiles):
    padded = (counts + MOE_TM - 1) // MOE_TM * MOE_TM
    ends = jnp.cumsum(padded)
    starts = ends - padded
    tile_row = jnp.arange(n_tiles, dtype=I32) * MOE_TM
    tile_expert = jnp.minimum(jnp.sum((tile_row[:, None] >= ends[None, :]).astype(I32), axis=1), N_EXPERTS - 1)
    n_used = jnp.maximum(ends[-1:] // MOE_TM, 1)
    return starts.astype(I32), tile_expert.astype(I32), n_used.astype(I32)


def _rope_tables(positions):
    half = HEAD_DIM // 2
    inv_freq = ROPE_THETA ** (-jnp.arange(half, dtype=F32) / half)
    ang = positions.astype(F32)[:, None] * inv_freq[None, :]
    cos = jnp.cos(ang)
    sin = jnp.sin(ang)
    cos_h = jnp.concatenate([cos, cos], axis=1)
    sin_h = jnp.concatenate([-sin, sin], axis=1)
    return jnp.tile(cos_h, (1, N_HEADS)), jnp.tile(sin_h, (1, N_HEADS))


def kernel(x_prompt, x_sample, cache_k, cache_v, state_ssm_re, state_ssm_im, page_table, norm1_g, w_in,
           q_norm_g, k_norm_g, ssm_a_re, ssm_a_im, ssm_log_dt, ssm_b_re, ssm_b_im, ssm_c_re, ssm_c_im,
           ssm_d, w_attn_up, w_ssm_glu, w_out, norm2_g, w_router, b_router, w_gate, b_gate, w_up, b_up,
           w_down, b_down):
    batch, seq, d_model = x_prompt.shape
    dec_batch = x_sample.shape[0]
    n_pages = page_table.shape[1]
    page = cache_k.shape[2]
    past_len = n_pages * page
    assert x_sample.shape[1] == 1 and w_in.shape[0] == 1
    assert seq % ROW_TILE == 0 and seq % MOBA_BLOCK == 0 and past_len % MOBA_BLOCK == 0
    assert page == LANES, "the sample attention keeps one page per lane tile"
    t_prompt = batch * seq
    heads = (N_HEADS, HEAD_DIM)

    xp = x_prompt.reshape(t_prompt, d_model)
    xs = x_sample.reshape(dec_batch, d_model)
    hd, hdt = _head_indicator()
    w_in_b = w_in[0].astype(BF16)
    q_gain = jnp.tile(q_norm_g, (1, N_HEADS))
    k_gain = jnp.tile(k_norm_g, (1, N_HEADS))
    cos_p, sin_p = _rope_tables(jnp.arange(seq, dtype=I32))
    cos_s, sin_s = _rope_tables(jnp.full((dec_batch,), past_len, I32))
    n_pairs = (t_prompt + dec_batch) * TOP_K
    n_tiles = n_pairs // MOE_TM + N_EXPERTS
    k_p, _, ga_p, gs_p, qt_p, kt_p, vt_p, ut_p, x_sorted = _inproj(
        xp, norm1_g, w_in_b, cos_p, sin_p, q_gain, k_gain, hd, hdt, ROW_TILE, zero_rows=n_tiles * MOE_TM)
    _, u_s, ga_s, gs_s, qt_s, kt_s, vt_s, _ = _inproj(xs, norm1_g, w_in_b, cos_s, sin_s, q_gain, k_gain,
                                                      hd, hdt, dec_batch)

    attn_p = _attn_prompt(qt_p, k_p, vt_p, batch, seq)
    feature_major = lambda c: c[0].transpose(0, 2, 3, 1).reshape(c.shape[1], ATTN_W, page)
    attn_s = _attn_sample(page_table, qt_s[0], kt_s[0], vt_s[0], feature_major(cache_k),
                          feature_major(cache_v))

    assert seq % S5_CHUNK == 0
    lam_re, lam_im, dt, ab_re, ab_im, bb_re, bb_im = _s5_discretize(
        ssm_a_re[0], ssm_a_im[0], ssm_log_dt[0], ssm_b_re[0], ssm_b_im[0])
    lags, w1_re, w1_im, w2_re, w2_im, al_re, al_im = _s5_chunk_operators(
        lam_re, lam_im, dt, bb_re, bb_im, ssm_c_re[0], ssm_c_im[0])
    d_rep = jnp.broadcast_to(ssm_d[0][:, None, None], (SSM_W, batch, S5_CHUNK))
    y_t, hp_re, hp_im = _s5_prompt(ut_p.transpose(1, 0, 2), lags, w1_re, w1_im, w2_re, w2_im, al_re, al_im,
                                   d_rep)
    yt_p = y_t.transpose(1, 0, 2)
    width = N_GROUPS * STATE_DIM
    y_s, hs_re, hs_im = _s5_sample(
        u_s, state_ssm_re[0].reshape(dec_batch, width), state_ssm_im[0].reshape(dec_batch, width),
        _block_diag(bb_re.transpose(0, 2, 1)).astype(BF16), _block_diag(bb_im.transpose(0, 2, 1)).astype(BF16),
        _block_diag(ssm_c_re[0].transpose(0, 2, 1)).astype(BF16),
        _block_diag(ssm_c_im[0].transpose(0, 2, 1)).astype(BF16),
        ab_re.reshape(1, width), ab_im.reshape(1, width), ssm_d)

    w_router_pad = jnp.concatenate([w_router[0], jnp.zeros((d_model, LANES - N_EXPERTS), F32)], axis=1)
    merge_w = (w_attn_up[0].astype(BF16), w_ssm_glu[0].astype(BF16), w_out[0].astype(BF16), norm2_g,
               w_router_pad, b_router[0].reshape(N_EXPERTS, 1))
    h_p, xn_p, idx_p, rank_p, wr_p, cnt_p = _merge(xp, attn_p, yt_p, ga_p, gs_p, *merge_w,
                                                   jnp.zeros((N_EXPERTS, LANES), F32), ROW_TILE)
    h_s, xn_s, idx_s, rank_s, wr_s, cnt_all = _merge(xs, attn_s, y_s.T[None], ga_s, gs_s, *merge_w, cnt_p,
                                                     dec_batch)

    starts, tile_expert, n_used = _routing_tables(cnt_all[:, 0].astype(I32), n_tiles)
    def sorted_rows(idx, rank):
        idx, pos = idx[:TOP_K], rank[:TOP_K]
        for e in range(N_EXPERTS):
            pos = pos + jnp.where(idx == e, starts[e], 0)
        return jnp.concatenate([pos, jnp.zeros_like(pos)], axis=0)

    pos_p, pos_s = sorted_rows(idx_p, rank_p), sorted_rows(idx_s, rank_s)
    x_sorted = _dispatch(pos_p, xn_p, x_sorted, MOE_TOK)
    x_sorted = _dispatch(pos_s, xn_s, x_sorted, dec_batch)
    y_sorted = _moe_grouped(tile_expert, n_used, x_sorted, w_gate[0], b_gate[0], w_up[0], b_up[0],
                            w_down[0], b_down[0])
    out_p = _combine(pos_p, h_p, wr_p, y_sorted, MOE_TOK)
    out_s = _combine(pos_s, h_s, wr_s, y_sorted, dec_batch)

    per_seq = lambda a, n, s: a.reshape((1, n) + heads + (s,)).transpose(0, 1, 4, 2, 3)
    new_rows = lambda a: per_seq(a, 1, dec_batch).transpose(0, 2, 1, 3, 4)
    return (out_p.reshape(batch, seq, d_model),
            out_s.reshape(dec_batch, 1, d_model),
            per_seq(kt_p, batch, seq),
            per_seq(vt_p, batch, seq),
            hp_re.transpose(1, 0, 2)[None],
            hp_im.transpose(1, 0, 2)[None],
            new_rows(kt_s),
            new_rows(vt_s),
            hs_re.reshape(1, dec_batch, N_GROUPS, STATE_DIM),
            hs_im.reshape(1, dec_batch, N_GROUPS, STATE_DIM))
```
